```python
import math
import jax
import jax.numpy as jnp
from jax import lax
import numpy as np

D_MODEL = 4096
BATCH = 4
SEQ = 2048
DEPTH = 1
DEC_BATCH = 32
DEC_SEQ = 4
PAST_LEN = 8192
PAGE_SIZE = 128

HEAD_DIM = 128
GDN_HEADS = D_MODEL // (2 * HEAD_DIM)
NSA_HEADS = D_MODEL // (2 * HEAD_DIM)
NSA_KV_HEADS = 4
NSA_GROUP = NSA_HEADS // NSA_KV_HEADS
GDN_WIDTH = GDN_HEADS * HEAD_DIM
NSA_WIDTH = NSA_HEADS * HEAD_DIM
MIX_WIDTH = GDN_WIDTH + NSA_WIDTH
KV_WIDTH = 2 * NSA_KV_HEADS * HEAD_DIM
IN_DIM = 4 * GDN_WIDTH + 2 * GDN_HEADS + NSA_WIDTH + 3 * KV_WIDTH + 3 * NSA_HEADS
CONV_W = 4
GDN_CHUNK = 64
CMP_BLOCK = 32
CMP_STRIDE = 16
SLC_BLOCK = 64
SLC_TOPK = 16
SLC_LOCAL = 2
WINDOW = 512
WIN_QBLOCK = 128
SLC_QBLOCK = 32
D_FF = 4 * D_MODEL
EPS = 1e-6
FORCE_SCORE = 1e9

kernel_name = 'gdn_nsa_parallel_hybrid_step'


def rms_norm(x, g):
    xf = x.astype(jnp.float32)
    y = xf * lax.rsqrt(jnp.mean(xf * xf, axis=-1, keepdims=True) + EPS)
    return (y * g.astype(jnp.float32)).astype(x.dtype)


def l2_norm(x):
    return x * lax.rsqrt(jnp.sum(x * x, axis=-1, keepdims=True) + EPS)


def masked_softmax(s, mask):
    s = jnp.where(mask, s, -jnp.inf)
    m = jnp.max(s, axis=-1, keepdims=True)
    m = jnp.where(jnp.isfinite(m), m, 0.0)
    p = jnp.exp(s - m)
    return p / jnp.maximum(jnp.sum(p, axis=-1, keepdims=True), jnp.finfo(jnp.float32).tiny)


def _in_splits():
    sizes = [GDN_WIDTH] * 4 + [GDN_HEADS] * 2 + [NSA_WIDTH, KV_WIDTH, KV_WIDTH, KV_WIDTH, 3 * NSA_HEADS]
    return [int(s) for s in np.cumsum(sizes)[:-1]]


def short_conv(x, buf, w):
    t = x.shape[1]
    xp = jnp.concatenate([buf.astype(x.dtype), x], axis=1)
    y = xp[:, 0:t] * w[0]
    for j in range(1, CONV_W):
        y = y + xp[:, j:j + t] * w[j]
    return jax.nn.silu(y), xp[:, t:]


def _to_chunks(a, pad, nc):
    b = a.shape[0]
    a = jnp.pad(a, [(0, 0), (0, pad)] + [(0, 0)] * (a.ndim - 2))
    a = a.reshape((b, nc, GDN_CHUNK) + a.shape[2:])
    return jnp.moveaxis(a, (1, 3), (0, 2))


def gated_delta_rule(q, k, v, g, beta, s0):
    b, t, h, _ = q.shape
    dv = v.shape[-1]
    pad = (-t) % GDN_CHUNK
    nc = (t + pad) // GDN_CHUNK
    qc, kc, vc, gc, bc = [_to_chunks(a, pad, nc) for a in (q, k, v, g, beta)]
    gcum = jnp.cumsum(gc, axis=-1)
    tri = np.tril(np.ones((GDN_CHUNK, GDN_CHUNK), dtype=bool))
    strict = np.tril(np.ones((GDN_CHUNK, GDN_CHUNK), dtype=bool), -1)
    decay = jnp.exp(jnp.where(tri, gcum[..., :, None] - gcum[..., None, :], -jnp.inf))
    kb = kc * bc[..., None]
    lmat = jnp.eye(GDN_CHUNK, dtype=q.dtype) + jnp.where(
        strict, jnp.einsum('nbhid,nbhjd->nbhij', kb, kc) * decay, 0.0)
    u = lax.linalg.triangular_solve(lmat, vc * bc[..., None], left_side=True, lower=True, unit_diagonal=True)
    w = lax.linalg.triangular_solve(lmat, kb * jnp.exp(gcum)[..., None], left_side=True, lower=True,
                                    unit_diagonal=True)
    a_qk = jnp.einsum('nbhid,nbhjd->nbhij', qc, kc) * decay

    def step(s, xs):
        q_i, k_i, u_i, w_i, g_i, a_i = xs
        v_new = u_i - jnp.einsum('bhck,bhkv->bhcv', w_i, s)
        o_i = (jnp.einsum('bhck,bhkv->bhcv', q_i * jnp.exp(g_i)[..., None], s)
               + jnp.einsum('bhij,bhjv->bhiv', a_i, v_new))
        g_last = g_i[..., -1]
        k_dec = k_i * jnp.exp(g_last[..., None] - g_i)[..., None]
        s = s * jnp.exp(g_last)[..., None, None] + jnp.einsum('bhck,bhcv->bhkv', k_dec, v_new)
        return s, o_i

    s_fin, o = lax.scan(step, s0, (qc, kc, u, w, gcum, a_qk))
    o = jnp.moveaxis(o, (0, 2), (1, 3)).reshape(b, nc * GDN_CHUNK, h, dv)[:, :t]
    return o, s_fin


def gdn_mixer(q_raw, k_raw, v_raw, z, b_raw, a_raw, conv_buf, s0, conv_w, a_log, dt_bias, norm_g):
    bsz, t, _ = q_raw.shape
    f32 = jnp.float32
    qkv, conv_new = short_conv(jnp.concatenate([q_raw, k_raw, v_raw], axis=-1), conv_buf, conv_w)
    qkv = qkv.astype(f32).reshape(bsz, t, 3, GDN_HEADS, HEAD_DIM)
    q = l2_norm(qkv[:, :, 0]) * HEAD_DIM ** -0.5
    k = l2_norm(qkv[:, :, 1])
    v = qkv[:, :, 2]
    beta = jax.nn.sigmoid(b_raw.astype(f32))
    g = -jnp.exp(a_log.astype(f32)) * jax.nn.softplus(a_raw.astype(f32) + dt_bias.astype(f32))
    o, s_new = gated_delta_rule(q, k, v, g, beta, s0.astype(f32))
    o = rms_norm(o, norm_g) * jax.nn.silu(z.astype(f32).reshape(bsz, t, GDN_HEADS, HEAD_DIM))
    return o.reshape(bsz, t, GDN_WIDTH).astype(q_raw.dtype), conv_new, s_new.astype(s0.dtype)


def compress_blocks(rows, pos_w, phi):
    bsz, length, grp, d = rows.shape
    n_cmp = (length - CMP_BLOCK) // CMP_STRIDE + 1
    pad = (-length) % CMP_STRIDE
    sub = jnp.pad(rows, ((0, 0), (0, pad), (0, 0), (0, 0))).reshape(
        bsz, (length + pad) // CMP_STRIDE, CMP_STRIDE, grp, d)
    parts = [jnp.einsum('bmjgd,jg->bmgd', sub, pos_w[r * CMP_STRIDE:(r + 1) * CMP_STRIDE])[:, r:r + n_cmp]
             for r in range(CMP_BLOCK // CMP_STRIDE)]
    pooled = parts[0]
    for p in parts[1:]:
        pooled = pooled + p
    return jnp.einsum('bngd,gde->bnge', pooled, phi)


def _slc_overlap(n_cmp, n_slc):
    cs = np.arange(n_cmp) * CMP_STRIDE
    ss = np.arange(n_slc) * SLC_BLOCK
    lo = np.maximum(cs[:, None], ss[None, :])
    hi = np.minimum(cs[:, None] + CMP_BLOCK, ss[None, :] + SLC_BLOCK)
    return (np.maximum(hi - lo, 0) / CMP_BLOCK).astype(np.float32)


def selected_attention(q, kv_all, sel_idx, qpos):
    bsz, t, grp, rep, d = q.shape
    length = kv_all.shape[1]
    n_slc = -(-length // SLC_BLOCK)
    n_sel = sel_idx.shape[-1]
    kvp = jnp.pad(kv_all, ((0, 0), (0, n_slc * SLC_BLOCK - length), (0, 0), (0, 0), (0, 0)))
    kvb = jnp.moveaxis(kvp.reshape(bsz, n_slc, SLC_BLOCK, 2, grp, d), 4, 1)
    qb = math.gcd(t, SLC_QBLOCK)
    nb = t // qb
    qs = jnp.moveaxis(q.reshape(bsz, nb, qb, grp, rep, d), 1, 0)
    ids = jnp.moveaxis(sel_idx.reshape(bsz, grp, nb, qb, n_sel), 2, 0)
    pos = jnp.asarray(qpos.reshape(nb, qb), dtype=jnp.int32)
    bi = jnp.arange(bsz)[:, None, None, None]
    gi = jnp.arange(grp)[None, :, None, None]
    scale = d ** -0.5

    def block(args):
        qq, ii, pp = args
        sel = kvb[bi, gi, ii]
        kpos = ii[..., None] * SLC_BLOCK + jnp.arange(SLC_BLOCK)
        mask = (kpos <= pp[None, None, :, None, None]).reshape(bsz, grp, 1, qb, n_sel * SLC_BLOCK)
        s = jnp.einsum('bqgrd,bgqksd->bgrqks', qq, sel[..., 0, :]).reshape(
            bsz, grp, rep, qb, n_sel * SLC_BLOCK).astype(jnp.float32) * scale
        p = masked_softmax(s, mask).reshape(bsz, grp, rep, qb, n_sel, SLC_BLOCK).astype(qq.dtype)
        return jnp.einsum('bgrqks,bgqksd->bqgrd', p, sel[..., 1, :])

    o = lax.map(block, (qs, ids, pos))
    return jnp.moveaxis(o, 0, 1).reshape(bsz, t, grp, rep, d)


def window_attention(q, kv_all, n_prev):
    bsz, t, grp, rep, d = q.shape
    qb = math.gcd(t, WIN_QBLOCK)
    nb = t // qb
    span = qb + WINDOW
    kv_pad = jnp.pad(kv_all, ((0, 0), (WINDOW, 0), (0, 0), (0, 0), (0, 0)))
    idx = n_prev + np.arange(nb)[:, None] * qb + np.arange(span)[None, :]
    kw = kv_pad[:, idx]
    krel = np.arange(nb)[:, None] * qb - WINDOW + np.arange(span)[None, :]
    qrel = np.arange(nb)[:, None] * qb + np.arange(qb)[None, :]
    diff = qrel[:, :, None] - krel[:, None, :]
    mask = (diff >= 0) & (diff < WINDOW) & (krel[:, None, :] >= -n_prev)
    qq = q.reshape(bsz, nb, qb, grp, rep, d)
    s = jnp.einsum('bnqgrd,bnsgd->bngrqs', qq, kw[:, :, :, 0]).astype(jnp.float32) * d ** -0.5
    p = masked_softmax(s, mask[None, :, None, None]).astype(q.dtype)
    o = jnp.einsum('bngrqs,bnsgd->bnqgrd', p, kw[:, :, :, 1])
    return o.reshape(bsz, t, grp, rep, d)


def _norm_keys(kv, g):
    return jnp.stack([rms_norm(kv[:, :, 0], g), kv[:, :, 1]], axis=2)


def nsa_mixer(q_raw, cmp_raw, slc_raw, win_raw, gate_raw, cmp_past, slc_past, win_past,
              q_norm_g, k_norm_g, cmp_pos_w, cmp_phi):
    bsz, t, _ = q_raw.shape
    grp, rep, d = NSA_KV_HEADS, NSA_GROUP, HEAD_DIM
    offset = cmp_past.shape[1]
    n_prev_win = win_past.shape[1]
    kv_shape = (bsz, t, 2, grp, d)
    q = rms_norm(q_raw.reshape(bsz, t, grp, rep, d), q_norm_g)
    cmp_new = cmp_raw.reshape(kv_shape)
    slc_new = _norm_keys(slc_raw.reshape(kv_shape), k_norm_g[1])
    win_new = _norm_keys(win_raw.reshape(kv_shape), k_norm_g[2])
    cmp_all = jnp.concatenate([cmp_past.astype(cmp_new.dtype), cmp_new], axis=1)
    slc_all = jnp.concatenate([slc_past.astype(slc_new.dtype), slc_new], axis=1)
    win_all = jnp.concatenate([win_past.astype(win_new.dtype), win_new], axis=1)
    qpos = offset + np.arange(t)

    kc = rms_norm(compress_blocks(cmp_all[:, :, 0], cmp_pos_w[0], cmp_phi[0]), k_norm_g[0])
    vc = compress_blocks(cmp_all[:, :, 1], cmp_pos_w[1], cmp_phi[1])
    n_cmp = kc.shape[1]
    blk_end = np.arange(n_cmp) * CMP_STRIDE + CMP_BLOCK - 1
    s = jnp.einsum('btgrd,bngd->bgrtn', q, kc).astype(jnp.float32) * d ** -0.5
    p_cmp = masked_softmax(s, blk_end[None, :] <= qpos[:, None])
    o_cmp = jnp.einsum('bgrtn,bngd->btgrd', p_cmp.astype(vc.dtype), vc)

    n_slc = -(-(offset + t) // SLC_BLOCK)
    imp = jnp.einsum('bgrtn,nj->bgtj', p_cmp, jnp.asarray(_slc_overlap(n_cmp, n_slc)))
    cur = qpos // SLC_BLOCK
    blk = np.arange(n_slc)
    forced = (blk[None] == 0) | ((blk[None] <= cur[:, None]) & (blk[None] > cur[:, None] - SLC_LOCAL))
    future = blk[None] > cur[:, None]
    score = jnp.where(forced, FORCE_SCORE, jnp.where(future, -FORCE_SCORE, imp))
    _, sel_idx = lax.top_k(score, min(SLC_TOPK, n_slc))
    o_slc = selected_attention(q, slc_all, sel_idx, qpos)

    o_win = window_attention(q, win_all, n_prev_win)

    gates = jax.nn.sigmoid(gate_raw.astype(jnp.float32)).reshape(bsz, t, grp, rep, 3).astype(q.dtype)
    o = gates[..., 0:1] * o_cmp + gates[..., 1:2] * o_slc + gates[..., 2:3] * o_win
    keep = min(WINDOW, win_all.shape[1])
    win_state = win_all[:, win_all.shape[1] - keep:]
    return o.reshape(bsz, t, NSA_WIDTH).astype(q_raw.dtype), cmp_new, slc_new, win_state


def hybrid_layer(x, cmp_past, slc_past, win_past, conv_buf, s0,
                 attn_norm_g, w_in, gdn_conv_w, gdn_a_log, gdn_dt_bias, gdn_norm_g,
                 q_norm_g, k_norm_g, cmp_pos_w, cmp_phi, w_o, mlp_norm_g, w_up, w_down):
    h = rms_norm(x, attn_norm_g)
    proj = jnp.einsum('btd,de->bte', h, w_in)
    gq, gk, gv, gz, gb, ga, nq, ncmp, nslc, nwin, ngate = jnp.split(proj, _in_splits(), axis=-1)
    o_gdn, conv_new, s_new = gdn_mixer(gq, gk, gv, gz, gb, ga, conv_buf, s0,
                                       gdn_conv_w, gdn_a_log, gdn_dt_bias, gdn_norm_g)
    o_nsa, cmp_new, slc_new, win_new = nsa_mixer(nq, ncmp, nslc, nwin, ngate, cmp_past, slc_past, win_past,
                                                 q_norm_g, k_norm_g, cmp_pos_w, cmp_phi)
    x = x + jnp.einsum('bte,ed->btd', jnp.concatenate([o_gdn, o_nsa], axis=-1), w_o)
    h2 = rms_norm(x, mlp_norm_g)
    hid = jnp.square(jax.nn.relu(jnp.einsum('btd,df->btf', h2, w_up)))
    x = x + jnp.einsum('btf,fd->btd', hid, w_down)
    return x, cmp_new, slc_new, win_new, conv_new, s_new


def setup_inputs(seed: int = 0) -> dict:
    key = jax.random.key(seed)
    ks = jax.random.split(key, 24)
    n_pages = PAST_LEN // PAGE_SIZE
    n_used = DEC_BATCH * n_pages
    n_phys = n_used + (n_used + 3) // 4
    grp, d = NSA_KV_HEADS, HEAD_DIM
    w_buf = min(WINDOW, PAST_LEN)
    f32 = jnp.float32

    def nrm(k, shape, s=1.0):
        return s * jax.random.normal(k, shape, f32)

    def gain(k, shape):
        return 1.0 + 0.02 * jax.random.normal(k, shape, f32)

    dt = jnp.exp(jax.random.uniform(ks[12], (DEPTH, GDN_HEADS), f32, math.log(1e-3), math.log(1e-1)))
    return {
        'x_prompt': nrm(ks[0], (BATCH, SEQ, D_MODEL)),
        'x_sample': nrm(ks[1], (DEC_BATCH, DEC_SEQ, D_MODEL)),
        'cache_cmp_kv': nrm(ks[2], (DEPTH, n_phys, PAGE_SIZE, 2, grp, d)),
        'cache_slc_kv': nrm(ks[3], (DEPTH, n_phys, PAGE_SIZE, 2, grp, d)),
        'cache_win_kv': nrm(ks[4], (DEPTH, DEC_BATCH, w_buf, 2, grp, d)),
        'cache_gdn_conv': nrm(ks[5], (DEPTH, DEC_BATCH, CONV_W - 1, 3 * GDN_WIDTH)),
        'state_gdn': nrm(ks[6], (DEPTH, DEC_BATCH, GDN_HEADS, d, d), 0.1),
        'page_table': jax.random.permutation(ks[7], n_phys)[:n_used].reshape(DEC_BATCH, n_pages).astype(jnp.int32),
        'attn_norm_g': gain(ks[8], (DEPTH, D_MODEL)),
        'w_in': nrm(ks[9], (DEPTH, D_MODEL, IN_DIM), D_MODEL ** -0.5),
        'gdn_conv_w': nrm(ks[10], (DEPTH, CONV_W, 3 * GDN_WIDTH), CONV_W ** -0.5),
        'gdn_a_log': jnp.log(jax.random.uniform(ks[11], (DEPTH, GDN_HEADS), f32, 1.0, 16.0)),
        'gdn_dt_bias': dt + jnp.log(-jnp.expm1(-dt)),
        'gdn_norm_g': gain(ks[13], (DEPTH, HEAD_DIM)),
        'q_norm_g': gain(ks[14], (DEPTH, HEAD_DIM)),
        'k_norm_g': gain(ks[15], (DEPTH, 3, HEAD_DIM)),
        'cmp_pos_w': (1.0 + 0.1 * jax.random.normal(ks[16], (DEPTH, 2, CMP_BLOCK, grp), f32)) / CMP_BLOCK,
        'cmp_phi': nrm(ks[17], (DEPTH, 2, grp, d, d), d ** -0.5),
        'w_o': nrm(ks[18], (DEPTH, MIX_WIDTH, D_MODEL), MIX_WIDTH ** -0.5),
        'mlp_norm_g': gain(ks[19], (DEPTH, D_MODEL)),
        'w_up': nrm(ks[20], (DEPTH, D_MODEL, D_FF), D_MODEL ** -0.5),
        'w_down': nrm(ks[21], (DEPTH, D_FF, D_MODEL), D_FF ** -0.5),
    }


def reference(x_prompt, x_sample, cache_cmp_kv, cache_slc_kv, cache_win_kv, cache_gdn_conv, state_gdn,
              page_table, attn_norm_g, w_in, gdn_conv_w, gdn_a_log, gdn_dt_bias, gdn_norm_g, q_norm_g,
              k_norm_g, cmp_pos_w, cmp_phi, w_o, mlp_norm_g, w_up, w_down):
    bsz = x_prompt.shape[0]
    dbsz = x_sample.shape[0]
    past = page_table.shape[1] * PAGE_SIZE
    kv_tail = (2, NSA_KV_HEADS, HEAD_DIM)
    empty = jnp.zeros((bsz, 0) + kv_tail, x_prompt.dtype)
    conv0 = jnp.zeros((bsz, CONV_W - 1, 3 * GDN_WIDTH), x_prompt.dtype)
    s_zero = jnp.zeros((bsz, GDN_HEADS, HEAD_DIM, HEAD_DIM), x_prompt.dtype)
    yp, ys = x_prompt, x_sample
    per_layer = []
    for layer in range(DEPTH):
        params = (attn_norm_g[layer], w_in[layer], gdn_conv_w[layer], gdn_a_log[layer], gdn_dt_bias[layer],
                  gdn_norm_g[layer], q_norm_g[layer], k_norm_g[layer], cmp_pos_w[layer], cmp_phi[layer],
                  w_o[layer], mlp_norm_g[layer], w_up[layer], w_down[layer])
        yp, cmp_p, slc_p, win_p, conv_p, s_p = hybrid_layer(yp, empty, empty, empty, conv0, s_zero, *params)
        cmp_past = cache_cmp_kv[layer][page_table].reshape((dbsz, past) + kv_tail)
        slc_past = cache_slc_kv[layer][page_table].reshape((dbsz, past) + kv_tail)
        ys, cmp_s, slc_s, win_s, conv_s, s_s = hybrid_layer(
            ys, cmp_past, slc_past, cache_win_kv[layer], cache_gdn_conv[layer], state_gdn[layer], *params)
        per_layer.append((cmp_p, cmp_s, slc_p, slc_s, win_p, win_s, conv_p, conv_s, s_p, s_s))
    st = [jnp.stack(z, axis=0) for z in zip(*per_layer)]
    return (yp, ys, st[0], st[1], st[2], st[3], st[4], st[5], st[6], st[7], st[8], st[9])
```

```python
import functools
import math

import numpy as np
import jax
import jax.numpy as jnp
from jax import lax
from jax.experimental import pallas as pl
from jax.experimental.pallas import tpu as pltpu

F32 = jnp.float32
BF16 = jnp.bfloat16
HI = lax.Precision.HIGHEST

HEAD_DIM = 128
GDN_HEADS = 16
NSA_HEADS = 16
NSA_KV_HEADS = 4
NSA_GROUP = NSA_HEADS // NSA_KV_HEADS
GDN_WIDTH = GDN_HEADS * HEAD_DIM
NSA_WIDTH = NSA_HEADS * HEAD_DIM
KV_WIDTH = 2 * NSA_KV_HEADS * HEAD_DIM
CONV_W = 4
GDN_CHUNK = 64
CMP_BLOCK = 32
CMP_STRIDE = 16
SLC_BLOCK = 64
SLC_TOPK = 16
SLC_LOCAL = 2
WINDOW = 512
PAGE_SIZE = 128
EPS = 1e-6
FORCE_SCORE = 1e9
NEG = -1e30
SCALE = HEAD_DIM ** -0.5

LANES = 128
SUBLANES = 8
VMEM_LIMIT = 52 * 1024 * 1024

COL_Q, COL_K, COL_V, COL_Z = 0, GDN_WIDTH, 2 * GDN_WIDTH, 3 * GDN_WIDTH
COL_NQ = 4 * GDN_WIDTH
COL_CMP = COL_NQ + NSA_WIDTH
COL_SLC = COL_CMP + KV_WIDTH
COL_WIN = COL_SLC + KV_WIDTH
COL_SMALL = COL_WIN + KV_WIDTH
PROJ_WIDTH = COL_SMALL + LANES
SMALL_B, SMALL_A, SMALL_GATE = 0, GDN_HEADS, 2 * GDN_HEADS


def _cparams(sem):
    return pltpu.CompilerParams(dimension_semantics=sem, vmem_limit_bytes=VMEM_LIMIT)


def _dot(a, b, prec=None):
    return jnp.dot(a, b, preferred_element_type=F32, precision=prec)


def _dot_nt(a, b, prec=None):
    return lax.dot_general(a, b, (((1,), (1,)), ((), ())), preferred_element_type=F32, precision=prec)


def _dot_tn(a, b, prec=None):
    return lax.dot_general(a, b, (((0,), (0,)), ((), ())), preferred_element_type=F32, precision=prec)


def _rms(x, g):
    return x * lax.rsqrt(jnp.mean(x * x, axis=-1, keepdims=True) + EPS) * g


def _silu(x):
    return x * jax.nn.sigmoid(x)


def _softplus(x):
    return jnp.maximum(x, 0.0) + jnp.log1p(jnp.exp(-jnp.abs(x)))


def _rms_mm_kernel(x_ref, g_ref, w_ref, o_ref, xn_ref, *, relu2):
    @pl.when(pl.program_id(1) == 0)
    def _():
        xn_ref[...] = _rms(x_ref[...], g_ref[...]).astype(BF16)

    y = _dot(xn_ref[...], w_ref[...])
    if relu2:
        y = jnp.square(jnp.maximum(y, 0.0))
    o_ref[...] = y.astype(o_ref.dtype)


def _rms_mm(x, g, w, *, tm, tn, relu2=False, out_dtype=F32):
    m, k = x.shape
    n = w.shape[1]
    assert m % tm == 0 and n % tn == 0
    return pl.pallas_call(
        functools.partial(_rms_mm_kernel, relu2=relu2),
        grid=(m // tm, n // tn),
        in_specs=[pl.BlockSpec((tm, k), lambda i, j: (i, 0)),
                  pl.BlockSpec((1, k), lambda i, j: (0, 0)),
                  pl.BlockSpec((k, tn), lambda i, j: (0, j))],
        out_specs=pl.BlockSpec((tm, tn), lambda i, j: (i, j)),
        out_shape=jax.ShapeDtypeStruct((m, n), out_dtype),
        scratch_shapes=[pltpu.VMEM((tm, k), BF16)],
        compiler_params=_cparams(("parallel", "arbitrary")),
        name="rms_matmul",
    )(x, g.reshape(1, k), w)


def _mm_res_kernel(a_ref, w_ref, r_ref, o_ref, acc_ref, *, nk):
    kk = pl.program_id(2)

    @pl.when(kk == 0)
    def _():
        acc_ref[...] = jnp.zeros_like(acc_ref)

    acc_ref[...] += _dot(a_ref[...], w_ref[...])

    @pl.when(kk == nk - 1)
    def _():
        o_ref[...] = r_ref[...] + acc_ref[...]


def _mm_res(a, w, res, *, tm, tn, tk):
    m, k = a.shape
    n = w.shape[1]
    assert m % tm == 0 and n % tn == 0 and k % tk == 0
    nk = k // tk
    return pl.pallas_call(
        functools.partial(_mm_res_kernel, nk=nk),
        grid=(m // tm, n // tn, nk),
        in_specs=[pl.BlockSpec((tm, tk), lambda i, j, kk: (i, kk)),
                  pl.BlockSpec((tk, tn), lambda i, j, kk: (kk, j)),
                  pl.BlockSpec((tm, tn), lambda i, j, kk: (i, j))],
        out_specs=pl.BlockSpec((tm, tn), lambda i, j, kk: (i, j)),
        out_shape=jax.ShapeDtypeStruct((m, n), F32),
        scratch_shapes=[pltpu.VMEM((tm, tn), F32)],
        compiler_params=_cparams(("parallel", "parallel", "arbitrary")),
        name="matmul_residual",
    )(a, w, res)


def _gdn_kernel(q_ref, k_ref, v_ref, z_ref, sm_ref, cbuf_ref, s0_ref, cw_ref, alog_ref, dt_ref, ng_ref,
                o_ref, sout_ref, xp_ref, st_ref, gc_ref, bt_ref, *, rows, nc, t_valid, prec):
    C = GDN_CHUNK
    W = GDN_WIDTH
    c = pl.program_id(1)

    @pl.when(c == 0)
    def _():
        xp_ref[...] = jnp.zeros(xp_ref.shape, F32)
        xp_ref[SUBLANES - (CONV_W - 1):SUBLANES, :] = cbuf_ref[0]
        st_ref[...] = s0_ref[0]

    @pl.when(c > 0)
    def _():
        xp_ref[0:SUBLANES, :] = xp_ref[C:C + SUBLANES, :]

    xp_ref[SUBLANES:SUBLANES + rows, 0:W] = q_ref[...]
    xp_ref[SUBLANES:SUBLANES + rows, W:2 * W] = k_ref[...]
    xp_ref[SUBLANES:SUBLANES + rows, 2 * W:3 * W] = v_ref[...]

    row = lax.broadcasted_iota(jnp.int32, (C, LANES), 0)
    lane = lax.broadcasted_iota(jnp.int32, (C, LANES), 1)
    valid = (c * C + row) < t_valid
    if rows < C:
        sm = jnp.concatenate([sm_ref[...], jnp.zeros((C - rows, LANES), F32)], axis=0)
    else:
        sm = sm_ref[...]
    beta_all = jnp.where(valid, jax.nn.sigmoid(sm), 0.0)
    g_all = jnp.where(valid, -jnp.exp(alog_ref[...]) * _softplus(sm + dt_ref[...]), 0.0)
    ri = lax.broadcasted_iota(jnp.int32, (C, C), 0)
    ci = lax.broadcasted_iota(jnp.int32, (C, C), 1)
    tri = ri >= ci
    strict = ri > ci
    gc_ref[...] = _dot(tri.astype(F32), g_all, HI)
    bt_ref[...] = beta_all
    valid_col = valid[:, 0:1]
    eye = (ri == ci).astype(F32)
    n_sq = int(math.log2(C)) - 1

    def conv(col):
        acc = None
        for j in range(CONV_W):
            term = (xp_ref[pl.ds(SUBLANES - (CONV_W - 1) + j, C), pl.ds(col, LANES)]
                    * cw_ref[pl.ds(j, 1), pl.ds(col, LANES)])
            acc = term if acc is None else acc + term
        return _silu(acc)

    def head(h, carry):
        off = pl.multiple_of(h * HEAD_DIM, HEAD_DIM)
        qh = conv(off)
        kh = conv(pl.multiple_of(W + h * HEAD_DIM, HEAD_DIM))
        vh = conv(pl.multiple_of(2 * W + h * HEAD_DIM, HEAD_DIM))
        qh = qh * lax.rsqrt(jnp.sum(qh * qh, axis=-1, keepdims=True) + EPS) * SCALE
        kh = kh * lax.rsqrt(jnp.sum(kh * kh, axis=-1, keepdims=True) + EPS)
        kh = jnp.where(valid_col, kh, 0.0)

        gcum = gc_ref[...]
        gc = jnp.sum(jnp.where(lane == SMALL_A + h, gcum, 0.0), axis=1, keepdims=True)
        bh = jnp.sum(jnp.where(lane == SMALL_B + h, bt_ref[...], 0.0), axis=1, keepdims=True)
        onehot = (lax.broadcasted_iota(jnp.int32, (SUBLANES, LANES), 1) == SMALL_A + h).astype(F32)
        gr = _dot_nt(onehot, gcum, HI)[0:1, :]
        decay = jnp.where(tri, jnp.exp(jnp.where(tri, gc - gr, 0.0)), 0.0)

        kb = kh * bh
        nmat = jnp.where(strict, -(_dot_nt(kb, kh, prec) * decay), 0.0)
        pinv = eye + nmat
        npow = nmat
        for _ in range(n_sq):
            npow = _dot(npow, npow, prec)
            pinv = pinv + _dot(pinv, npow, prec)
        uw = _dot(pinv, jnp.concatenate([vh * bh, kb * jnp.exp(gc)], axis=1), prec)
        u = uw[:, 0:HEAD_DIM]
        w = uw[:, HEAD_DIM:2 * HEAD_DIM]
        aqk = jnp.where(tri, _dot_nt(qh, kh, prec) * decay, 0.0)

        s = st_ref[h]
        v_new = u - _dot(w, s, prec)
        o = _dot(qh * jnp.exp(gc), s, prec) + _dot(aqk, v_new, prec)
        g_last = gc[C - 1:C, :]
        k_dec = kh * jnp.exp(g_last - gc)
        st_ref[h] = s * jnp.exp(g_last) + _dot_tn(k_dec, v_new, prec)

        o = _rms(o, ng_ref[...])
        zh = z_ref[:, pl.ds(off, HEAD_DIM)]
        o_ref[:, pl.ds(off, HEAD_DIM)] = (o[0:rows] * _silu(zh)).astype(o_ref.dtype)
        return carry

    lax.fori_loop(0, GDN_HEADS, head, 0)

    @pl.when(c == nc - 1)
    def _():
        sout_ref[0] = st_ref[...]


def _gdn(proj, conv_buf, s0, conv_w, a_log, dt_bias, norm_g, *, bsz, rows_per_seq, t_valid, prec=HI):
    C = GDN_CHUNK
    if rows_per_seq >= C:
        assert rows_per_seq % C == 0
        rows, nc = C, rows_per_seq // C
    else:
        assert rows_per_seq % SUBLANES == 0
        rows, nc = rows_per_seq, 1
    wblk = GDN_WIDTH
    zeros = jnp.zeros((LANES - 2 * GDN_HEADS,), F32)
    alog_row = jnp.concatenate([jnp.zeros((GDN_HEADS,), F32), a_log, zeros]).reshape(1, LANES)
    dt_row = jnp.concatenate([jnp.zeros((GDN_HEADS,), F32), dt_bias, zeros]).reshape(1, LANES)
    row_map = lambda b, c: b * nc + c
    in_specs = [
        pl.BlockSpec((rows, wblk), lambda b, c: (row_map(b, c), COL_Q // wblk)),
        pl.BlockSpec((rows, wblk), lambda b, c: (row_map(b, c), COL_K // wblk)),
        pl.BlockSpec((rows, wblk), lambda b, c: (row_map(b, c), COL_V // wblk)),
        pl.BlockSpec((rows, wblk), lambda b, c: (row_map(b, c), COL_Z // wblk)),
        pl.BlockSpec((rows, LANES), lambda b, c: (row_map(b, c), COL_SMALL // LANES)),
        pl.BlockSpec((1, CONV_W - 1, 3 * GDN_WIDTH), lambda b, c: (b, 0, 0)),
        pl.BlockSpec((1, GDN_HEADS, HEAD_DIM, HEAD_DIM), lambda b, c: (b, 0, 0, 0)),
        pl.BlockSpec((CONV_W, 3 * GDN_WIDTH), lambda b, c: (0, 0)),
        pl.BlockSpec((1, LANES), lambda b, c: (0, 0)),
        pl.BlockSpec((1, LANES), lambda b, c: (0, 0)),
        pl.BlockSpec((1, HEAD_DIM), lambda b, c: (0, 0)),
    ]
    out_specs = [
        pl.BlockSpec((rows, wblk), lambda b, c: (row_map(b, c), 0)),
        pl.BlockSpec((1, GDN_HEADS, HEAD_DIM, HEAD_DIM), lambda b, c: (b, 0, 0, 0)),
    ]
    return pl.pallas_call(
        functools.partial(_gdn_kernel, rows=rows, nc=nc, t_valid=t_valid, prec=prec),
        grid=(bsz, nc),
        in_specs=in_specs,
        out_specs=out_specs,
        out_shape=[jax.ShapeDtypeStruct((bsz * rows_per_seq, GDN_WIDTH), BF16),
                   jax.ShapeDtypeStruct((bsz, GDN_HEADS, HEAD_DIM, HEAD_DIM), F32)],
        scratch_shapes=[pltpu.VMEM((C + SUBLANES, 3 * GDN_WIDTH), F32),
                        pltpu.VMEM((GDN_HEADS, HEAD_DIM, HEAD_DIM), F32),
                        pltpu.VMEM((C, LANES), F32),
                        pltpu.VMEM((C, LANES), F32)],
        compiler_params=_cparams(("parallel", "arbitrary")),
        name="gdn_chunk_scan",
    )(proj, proj, proj, proj, proj, conv_buf, s0, conv_w, alog_row, dt_row, norm_g.reshape(1, HEAD_DIM))


def _prep_kernel(nq_ref, slc_ref, win_ref, sm_ref, qg_ref, kg_ref, q_out, slc_out, win_out, gate_out):
    qg = qg_ref[...]
    for h in range(NSA_HEADS):
        sl = slice(h * HEAD_DIM, (h + 1) * HEAD_DIM)
        q_out[:, sl] = _rms(nq_ref[:, sl], qg)
    half = NSA_KV_HEADS * HEAD_DIM
    for src, dst, gi in ((slc_ref, slc_out, 1), (win_ref, win_out, 2)):
        kg = kg_ref[gi:gi + 1, :]
        for g in range(NSA_KV_HEADS):
            sl = slice(g * HEAD_DIM, (g + 1) * HEAD_DIM)
            dst[:, sl] = _rms(src[:, sl], kg)
        dst[:, half:2 * half] = src[:, half:2 * half]
    sig = jax.nn.sigmoid(sm_ref[...])
    per = NSA_GROUP * 3
    for g in range(NSA_KV_HEADS):
        gate_out[g] = pltpu.roll(sig, LANES - (SMALL_GATE + per * g), axis=1)


def _nsa_prep(proj, q_norm_g, k_norm_g, *, tm):
    m = proj.shape[0]
    assert m % tm == 0
    return pl.pallas_call(
        _prep_kernel,
        grid=(m // tm,),
        in_specs=[pl.BlockSpec((tm, NSA_WIDTH), lambda i: (i, COL_NQ // NSA_WIDTH)),
                  pl.BlockSpec((tm, KV_WIDTH), lambda i: (i, COL_SLC // KV_WIDTH)),
                  pl.BlockSpec((tm, KV_WIDTH), lambda i: (i, COL_WIN // KV_WIDTH)),
                  pl.BlockSpec((tm, LANES), lambda i: (i, COL_SMALL // LANES)),
                  pl.BlockSpec((1, HEAD_DIM), lambda i: (0, 0)),
                  pl.BlockSpec((3, HEAD_DIM), lambda i: (0, 0))],
        out_specs=[pl.BlockSpec((tm, NSA_WIDTH), lambda i: (i, 0)),
                   pl.BlockSpec((tm, KV_WIDTH), lambda i: (i, 0)),
                   pl.BlockSpec((tm, KV_WIDTH), lambda i: (i, 0)),
                   pl.BlockSpec((NSA_KV_HEADS, tm, LANES), lambda i: (0, i, 0))],
        out_shape=[jax.ShapeDtypeStruct((m, NSA_WIDTH), F32),
                   jax.ShapeDtypeStruct((m, KV_WIDTH), F32),
                   jax.ShapeDtypeStruct((m, KV_WIDTH), F32),
                   jax.ShapeDtypeStruct((NSA_KV_HEADS, m, LANES), F32)],
        compiler_params=_cparams(("parallel",)),
        name="nsa_prep",
    )(proj, proj, proj, proj, q_norm_g.reshape(1, HEAD_DIM), k_norm_g)


def _pool_kernel(tbl_ref, rows_ref, w_ref, s0_ref, s1_ref):
    del tbl_ref
    sub = PAGE_SIZE // CMP_STRIDE
    x = rows_ref[...].reshape(sub, CMP_STRIDE, KV_WIDTH)
    s0_ref[0] = jnp.sum(x * w_ref[0][None], axis=1)
    s1_ref[0] = jnp.sum(x * w_ref[1][None], axis=1)


def _pool(rows2d, table, col_block, wexp):
    bsz, n_pages = table.shape
    sub = PAGE_SIZE // CMP_STRIDE
    grid_spec = pltpu.PrefetchScalarGridSpec(
        num_scalar_prefetch=1,
        grid=(bsz, n_pages),
        in_specs=[pl.BlockSpec((PAGE_SIZE, KV_WIDTH), lambda b, p, tbl: (tbl[b, p], col_block)),
                  pl.BlockSpec((2, CMP_STRIDE, KV_WIDTH), lambda b, p, tbl: (0, 0, 0))],
        out_specs=[pl.BlockSpec((1, sub, KV_WIDTH), lambda b, p, tbl: (b, p, 0)),
                   pl.BlockSpec((1, sub, KV_WIDTH), lambda b, p, tbl: (b, p, 0))],
    )
    shape = jax.ShapeDtypeStruct((bsz, n_pages * sub, KV_WIDTH), F32)
    return pl.pallas_call(
        _pool_kernel, grid_spec=grid_spec, out_shape=[shape, shape],
        compiler_params=_cparams(("parallel", "arbitrary")), name="cmp_pool",
    )(table, rows2d, wexp)


def _cmp_fin_kernel(s0_ref, s1_ref, phi_ref, kg_ref, kc_ref, vc_ref, *, n_cmp):
    nb = s0_ref.shape[1]
    pooled = s0_ref[0] + pltpu.roll(s1_ref[0], nb - 1, axis=0)
    live = lax.broadcasted_iota(jnp.int32, (nb, HEAD_DIM), 0) < n_cmp
    half = NSA_KV_HEADS * HEAD_DIM
    for g in range(NSA_KV_HEADS):
        pk = pooled[:, g * HEAD_DIM:(g + 1) * HEAD_DIM]
        pv = pooled[:, half + g * HEAD_DIM:half + (g + 1) * HEAD_DIM]
        kc_ref[0, g] = jnp.where(live, _rms(_dot(pk, phi_ref[0, g], HI), kg_ref[...]), 0.0)
        vc_ref[0, g] = jnp.where(live, _dot(pv, phi_ref[1, g], HI), 0.0)


def _cmp_finish(s0, s1, phi, kg, *, n_cmp):
    bsz, nb, _ = s0.shape
    shape = jax.ShapeDtypeStruct((bsz, NSA_KV_HEADS, nb, HEAD_DIM), F32)
    return pl.pallas_call(
        functools.partial(_cmp_fin_kernel, n_cmp=n_cmp),
        grid=(bsz,),
        in_specs=[pl.BlockSpec((1, nb, KV_WIDTH), lambda b: (b, 0, 0)),
                  pl.BlockSpec((1, nb, KV_WIDTH), lambda b: (b, 0, 0)),
                  pl.BlockSpec((2, NSA_KV_HEADS, HEAD_DIM, HEAD_DIM), lambda b: (0, 0, 0, 0)),
                  pl.BlockSpec((1, HEAD_DIM), lambda b: (0, 0))],
        out_specs=[pl.BlockSpec((1, NSA_KV_HEADS, nb, HEAD_DIM), lambda b: (b, 0, 0, 0)),
                   pl.BlockSpec((1, NSA_KV_HEADS, nb, HEAD_DIM), lambda b: (b, 0, 0, 0))],
        out_shape=[shape, shape],
        compiler_params=_cparams(("parallel",)),
        name="cmp_finish",
    )(s0, s1, phi, kg.reshape(1, HEAD_DIM))


def _stack_heads(ref, g, tq, dtype):
    parts = [ref[:, (g * NSA_GROUP + r) * HEAD_DIM:(g * NSA_GROUP + r + 1) * HEAD_DIM].astype(dtype)
             for r in range(NSA_GROUP)]
    return jnp.concatenate(parts, axis=0)


def _cmp_topk_kernel(q_ref, kc_ref, vc_ref, ov_ref, ocmp_ref, sel_ref, *, tq, n_cmp, n_slc, offset):
    i = pl.program_id(1)
    nb = kc_ref.shape[2]
    nsp = ov_ref.shape[1]
    rows = NSA_GROUP * tq
    r4 = lax.broadcasted_iota(jnp.int32, (rows, nb), 0)
    n4 = lax.broadcasted_iota(jnp.int32, (rows, nb), 1)
    tpos = offset + i * tq + (r4 & (tq - 1))
    valid = (n4 * CMP_STRIDE + (CMP_BLOCK - 1) <= tpos) & (n4 < n_cmp)

    blk = lax.broadcasted_iota(jnp.int32, (tq, nsp), 1)
    tq_pos = offset + i * tq + lax.broadcasted_iota(jnp.int32, (tq, nsp), 0)
    cur = tq_pos // SLC_BLOCK
    forced = (blk == 0) | ((blk <= cur) & (blk > cur - SLC_LOCAL))
    future = blk > cur
    in_range = blk < n_slc
    topk = min(SLC_TOPK, n_slc)

    for g in range(NSA_KV_HEADS):
        q4 = _stack_heads(q_ref, g, tq, F32)
        s = _dot_nt(q4, kc_ref[0, g], HI) * SCALE
        s = jnp.where(valid, s, -jnp.inf)
        m = jnp.max(s, axis=-1, keepdims=True)
        m = jnp.where(m == -jnp.inf, 0.0, m)
        p = jnp.exp(s - m)
        p = p / jnp.maximum(jnp.sum(p, axis=-1, keepdims=True), jnp.finfo(jnp.float32).tiny)
        o = _dot(p, vc_ref[0, g], HI)
        psum = p[0:tq]
        for r in range(NSA_GROUP):
            col = (g * NSA_GROUP + r) * HEAD_DIM
            ocmp_ref[:, col:col + HEAD_DIM] = o[r * tq:(r + 1) * tq]
            if r:
                psum = psum + p[r * tq:(r + 1) * tq]
        imp = _dot(psum, ov_ref[...], HI)
        score = jnp.where(forced, FORCE_SCORE, jnp.where(future, -FORCE_SCORE, imp))
        score = jnp.where(in_range, score, -jnp.inf)
        rank = jnp.zeros((tq, nsp), jnp.int32)
        for j in range(n_slc):
            cj = score[:, j:j + 1]
            ahead = (cj > score) | ((cj == score) & (blk > j))
            rank = rank + ahead.astype(jnp.int32)
        sel_ref[g] = ((rank < topk) & in_range).astype(F32)


def _cmp_topk(qn, kc, vc, overlap, *, bsz, rows_per_seq, tq, n_cmp, n_slc, offset):
    m = qn.shape[0]
    nt = rows_per_seq // tq
    nb = kc.shape[2]
    nsp = overlap.shape[1]
    return pl.pallas_call(
        functools.partial(_cmp_topk_kernel, tq=tq, n_cmp=n_cmp, n_slc=n_slc, offset=offset),
        grid=(bsz, nt),
        in_specs=[pl.BlockSpec((tq, NSA_WIDTH), lambda b, i: (b * nt + i, 0)),
                  pl.BlockSpec((1, NSA_KV_HEADS, nb, HEAD_DIM), lambda b, i: (b, 0, 0, 0)),
                  pl.BlockSpec((1, NSA_KV_HEADS, nb, HEAD_DIM), lambda b, i: (b, 0, 0, 0)),
                  pl.BlockSpec((nb, nsp), lambda b, i: (0, 0))],
        out_specs=[pl.BlockSpec((tq, NSA_WIDTH), lambda b, i: (b * nt + i, 0)),
                   pl.BlockSpec((NSA_KV_HEADS, tq, nsp), lambda b, i: (0, b * nt + i, 0))],
        out_shape=[jax.ShapeDtypeStruct((m, NSA_WIDTH), F32),
                   jax.ShapeDtypeStruct((NSA_KV_HEADS, m, nsp), F32)],
        compiler_params=_cparams(("parallel", "arbitrary")),
        name="cmp_attn_topk",
    )(qn, kc, vc, overlap)


def _overlap_matrix(nb, nsp, n_cmp, n_slc):
    cs = np.arange(nb) * CMP_STRIDE
    ss = np.arange(nsp) * SLC_BLOCK
    lo = np.maximum(cs[:, None], ss[None, :])
    hi = np.minimum(cs[:, None] + CMP_BLOCK, ss[None, :] + SLC_BLOCK)
    ov = (np.maximum(hi - lo, 0) / CMP_BLOCK).astype(np.float32)
    ov[n_cmp:, :] = 0.0
    ov[:, n_slc:] = 0.0
    return jnp.asarray(ov)


def _flash_init(m_ref, l_ref, acc_ref):
    m_ref[...] = jnp.full(m_ref.shape, NEG, F32)
    l_ref[...] = jnp.zeros(l_ref.shape, F32)
    acc_ref[...] = jnp.zeros(acc_ref.shape, F32)


def _flash_update(s, mask, v, m_ref, l_ref, acc_ref, prec=None):
    s = jnp.where(mask, s, NEG)
    m_prev = m_ref[...]
    m_new = jnp.maximum(m_prev, jnp.max(s, axis=-1, keepdims=True))
    alpha = jnp.exp(m_prev - m_new)
    p = jnp.exp(s - m_new)
    l_ref[...] = alpha * l_ref[...] + jnp.sum(p, axis=-1, keepdims=True)
    acc_ref[...] = alpha * acc_ref[...] + _dot(p.astype(v.dtype), v, prec)
    m_ref[...] = m_new


def _pattn_kernel(q_ref, ks_ref, vs_ref, kw_ref, vw_ref, ocmp_ref, sel_ref, gate_ref, exp_ref, o_ref,
                  sexp_ref, m_ref, l_ref, acc_ref, *, tq):
    i = pl.program_id(2)
    tk = tq
    rows = NSA_GROUP * tq
    q4 = jnp.concatenate([q_ref[:, r * HEAD_DIM:(r + 1) * HEAD_DIM].astype(BF16) for r in range(NSA_GROUP)],
                         axis=0)
    sexp_ref[...] = _dot(sel_ref[0].astype(BF16), exp_ref[...])
    qpos = i * tq + lax.broadcasted_iota(jnp.int32, (tq, tk), 0)
    lane = lax.broadcasted_iota(jnp.int32, (tq, tk), 1)

    def tile4(x):
        return jnp.concatenate([x] * NSA_GROUP, axis=0)

    def slc_step(j, carry):
        start = pl.multiple_of(j * tk, tk)
        k = ks_ref[pl.ds(start, tk), :]
        v = vs_ref[pl.ds(start, tk), :]
        s = _dot_nt(q4, k) * SCALE
        mask = (sexp_ref[:, pl.ds(start, tk)] > 0.5) & (j * tk + lane <= qpos)
        _flash_update(s, tile4(mask), v, m_ref, l_ref, acc_ref)
        return carry

    _flash_init(m_ref, l_ref, acc_ref)
    lax.fori_loop(0, i + 1, slc_step, 0)
    o_slc = acc_ref[...] / l_ref[...]

    def win_step(j, carry):
        start = pl.multiple_of(j * tk, tk)
        k = kw_ref[pl.ds(start, tk), :]
        v = vw_ref[pl.ds(start, tk), :]
        s = _dot_nt(q4, k) * SCALE
        diff = qpos - (j * tk + lane)
        mask = (diff >= 0) & (diff < WINDOW)
        _flash_update(s, tile4(mask), v, m_ref, l_ref, acc_ref)
        return carry

    _flash_init(m_ref, l_ref, acc_ref)
    lax.fori_loop(jnp.maximum(i - WINDOW // tk, 0), i + 1, win_step, 0)
    o_win = acc_ref[...] / l_ref[...]

    gates = gate_ref[0]
    for r in range(NSA_GROUP):
        sl = slice(r * tq, (r + 1) * tq)
        col = slice(r * HEAD_DIM, (r + 1) * HEAD_DIM)
        o = (gates[:, 3 * r:3 * r + 1] * ocmp_ref[:, col]
             + gates[:, 3 * r + 1:3 * r + 2] * o_slc[sl]
             + gates[:, 3 * r + 2:3 * r + 3] * o_win[sl])
        o_ref[:, col] = o.astype(o_ref.dtype)


def _prompt_attn(qn, slc_bf, win_bf, ocmp, sel, gates, *, bsz, t, tq):
    m = qn.shape[0]
    nt = t // tq
    gw = NSA_GROUP * HEAD_DIM
    half = NSA_KV_HEADS
    nsp = sel.shape[2]
    n_slc = -(-t // SLC_BLOCK)
    expand = np.zeros((nsp, t), np.float32)
    expand[np.arange(t) // SLC_BLOCK, np.arange(t)] = 1.0
    assert n_slc <= nsp
    kv_spec = lambda which: pl.BlockSpec((t, HEAD_DIM), lambda b, g, i: (b, which * half + g))
    return pl.pallas_call(
        functools.partial(_pattn_kernel, tq=tq),
        grid=(bsz, NSA_KV_HEADS, nt),
        in_specs=[pl.BlockSpec((tq, gw), lambda b, g, i: (b * nt + i, g)),
                  kv_spec(0), kv_spec(1), kv_spec(0), kv_spec(1),
                  pl.BlockSpec((tq, gw), lambda b, g, i: (b * nt + i, g)),
                  pl.BlockSpec((1, tq, nsp), lambda b, g, i: (g, b * nt + i, 0)),
                  pl.BlockSpec((1, tq, LANES), lambda b, g, i: (g, b * nt + i, 0)),
                  pl.BlockSpec((nsp, t), lambda b, g, i: (0, 0))],
        out_specs=pl.BlockSpec((tq, gw), lambda b, g, i: (b * nt + i, g)),
        out_shape=jax.ShapeDtypeStruct((m, NSA_WIDTH), BF16),
        scratch_shapes=[pltpu.VMEM((tq, t), F32),
                        pltpu.VMEM((NSA_GROUP * tq, 1), F32),
                        pltpu.VMEM((NSA_GROUP * tq, 1), F32),
                        pltpu.VMEM((NSA_GROUP * tq, HEAD_DIM), F32)],
        compiler_params=_cparams(("parallel", "parallel", "arbitrary")),
        name="nsa_prompt_attn",
    )(qn, slc_bf, slc_bf, win_bf, win_bf, ocmp, sel, gates, jnp.asarray(expand, BF16))


def _sattn_kernel(tbl_ref, q_ref, page_ref, snew_ref, wcache_ref, wnew_ref, ocmp_ref, sel_ref, gate_ref, o_ref,
                  m_ref, l_ref, acc_ref, *, tq, n_pages, past):
    del tbl_ref
    j = pl.program_id(1)
    rows = NSA_GROUP * tq
    half = NSA_KV_HEADS * HEAD_DIM
    per_page = PAGE_SIZE // SLC_BLOCK
    nsp = sel_ref.shape[2]

    def tile4(x):
        return jnp.concatenate([x] * NSA_GROUP, axis=0)

    @pl.when(j == 0)
    def _():
        _flash_init(m_ref, l_ref, acc_ref)

    lane = lax.broadcasted_iota(jnp.int32, (tq, PAGE_SIZE), 1)
    blk_lane = lax.broadcasted_iota(jnp.int32, (tq, nsp), 1)
    for g in range(NSA_KV_HEADS):
        q4 = _stack_heads(q_ref, g, tq, BF16)
        k = page_ref[:, g * HEAD_DIM:(g + 1) * HEAD_DIM].astype(BF16)
        v = page_ref[:, half + g * HEAD_DIM:half + (g + 1) * HEAD_DIM].astype(BF16)
        s = _dot_nt(q4, k) * SCALE
        selg = sel_ref[g]
        mask = jnp.zeros((tq, PAGE_SIZE), jnp.bool_)
        for u in range(per_page):
            flag = jnp.sum(jnp.where(blk_lane == j * per_page + u, selg, 0.0), axis=1, keepdims=True) > 0.5
            mask = mask | (flag & (lane // SLC_BLOCK == u))
        _flash_update(s, tile4(mask), v, m_ref.at[g], l_ref.at[g], acc_ref.at[g])

    @pl.when(j == n_pages - 1)
    def _():
        trow = lax.broadcasted_iota(jnp.int32, (tq, tq), 0)
        tcol = lax.broadcasted_iota(jnp.int32, (tq, tq), 1)
        new_blk = (past + tcol) // SLC_BLOCK
        causal = tcol <= trow
        wrow = lax.broadcasted_iota(jnp.int32, (tq, wcache_ref.shape[1]), 0)
        wcol = lax.broadcasted_iota(jnp.int32, (tq, wcache_ref.shape[1]), 1)
        n_prev = wcache_ref.shape[1]
        wdiff = wrow + n_prev - wcol
        wmask = (wdiff >= 0) & (wdiff < WINDOW)
        ndiff = trow - tcol
        nmask = (ndiff >= 0) & (ndiff < WINDOW)
        gates_all = gate_ref[...]
        for g in range(NSA_KV_HEADS):
            q4 = _stack_heads(q_ref, g, tq, F32)
            kn = snew_ref[:, g * HEAD_DIM:(g + 1) * HEAD_DIM]
            vn = snew_ref[:, half + g * HEAD_DIM:half + (g + 1) * HEAD_DIM]
            s = _dot_nt(q4, kn, HI) * SCALE
            selg = sel_ref[g]
            flag = jnp.zeros((tq, tq), jnp.bool_)
            for u in range(tq):
                col = jnp.sum(jnp.where(blk_lane == (past + u) // SLC_BLOCK, selg, 0.0), axis=1, keepdims=True) > 0.5
                flag = flag | (col & (tcol == u))
            _flash_update(s, tile4(flag & causal), vn, m_ref.at[g], l_ref.at[g], acc_ref.at[g], HI)
            o_slc = acc_ref[g] / l_ref[g]

            kc = wcache_ref[0, :, g * HEAD_DIM:(g + 1) * HEAD_DIM].astype(BF16)
            vc = wcache_ref[0, :, half + g * HEAD_DIM:half + (g + 1) * HEAD_DIM].astype(BF16)
            kw = wnew_ref[:, g * HEAD_DIM:(g + 1) * HEAD_DIM]
            vw = wnew_ref[:, half + g * HEAD_DIM:half + (g + 1) * HEAD_DIM]
            sc = jnp.where(tile4(wmask), _dot_nt(q4.astype(BF16), kc) * SCALE, NEG)
            sn = jnp.where(tile4(nmask), _dot_nt(q4, kw, HI) * SCALE, NEG)
            mx = jnp.maximum(jnp.max(sc, axis=-1, keepdims=True), jnp.max(sn, axis=-1, keepdims=True))
            pc = jnp.exp(sc - mx)
            pn = jnp.exp(sn - mx)
            den = jnp.sum(pc, axis=-1, keepdims=True) + jnp.sum(pn, axis=-1, keepdims=True)
            o_win = (_dot(pc.astype(BF16), vc) + _dot(pn, vw, HI)) / den

            gates = gates_all[g]
            for r in range(NSA_GROUP):
                sl = slice(r * tq, (r + 1) * tq)
                col = slice((g * NSA_GROUP + r) * HEAD_DIM, (g * NSA_GROUP + r + 1) * HEAD_DIM)
                o = (gates[:, 3 * r:3 * r + 1] * ocmp_ref[:, col]
                     + gates[:, 3 * r + 1:3 * r + 2] * o_slc[sl]
                     + gates[:, 3 * r + 2:3 * r + 3] * o_win[sl])
                o_ref[:, col] = o.astype(o_ref.dtype)


def _sample_attn(qn, cache2d, page_table, slc_new, win_cache, win_new, ocmp, sel, gates, *, bsz, tq, past):
    n_pages = page_table.shape[1]
    n_prev = win_cache.shape[1]
    nsp = sel.shape[2]
    rows = NSA_GROUP * tq
    grid_spec = pltpu.PrefetchScalarGridSpec(
        num_scalar_prefetch=1,
        grid=(bsz, n_pages),
        in_specs=[pl.BlockSpec((tq, NSA_WIDTH), lambda b, j, tbl: (b, 0)),
                  pl.BlockSpec((PAGE_SIZE, KV_WIDTH), lambda b, j, tbl: (tbl[b, j], 0)),
                  pl.BlockSpec((tq, KV_WIDTH), lambda b, j, tbl: (b, 0)),
                  pl.BlockSpec((1, n_prev, KV_WIDTH), lambda b, j, tbl: (b, 0, 0)),
                  pl.BlockSpec((tq, KV_WIDTH), lambda b, j, tbl: (b, 0)),
                  pl.BlockSpec((tq, NSA_WIDTH), lambda b, j, tbl: (b, 0)),
                  pl.BlockSpec((NSA_KV_HEADS, tq, nsp), lambda b, j, tbl: (0, b, 0)),
                  pl.BlockSpec((NSA_KV_HEADS, tq, LANES), lambda b, j, tbl: (0, b, 0))],
        out_specs=pl.BlockSpec((tq, NSA_WIDTH), lambda b, j, tbl: (b, 0)),
        scratch_shapes=[pltpu.VMEM((NSA_KV_HEADS, rows, 1), F32),
                        pltpu.VMEM((NSA_KV_HEADS, rows, 1), F32),
                        pltpu.VMEM((NSA_KV_HEADS, rows, HEAD_DIM), F32)],
    )
    return pl.pallas_call(
        functools.partial(_sattn_kernel, tq=tq, n_pages=n_pages, past=past),
        grid_spec=grid_spec,
        out_shape=jax.ShapeDtypeStruct((bsz * tq, NSA_WIDTH), BF16),
        compiler_params=_cparams(("parallel", "arbitrary")),
        name="nsa_sample_attn",
    )(page_table, qn, cache2d, slc_new, win_cache, win_new, ocmp, sel, gates)


def _pack_w_in(w_in):
    gw = GDN_WIDTH
    b0 = 4 * gw
    n0 = b0 + 2 * GDN_HEADS
    g0 = n0 + NSA_WIDTH + 3 * KV_WIDTH
    k = w_in.shape[0]
    pad = jnp.zeros((k, LANES - 2 * GDN_HEADS - 3 * NSA_HEADS), w_in.dtype)
    return jnp.concatenate([w_in[:, :b0], w_in[:, n0:g0], w_in[:, b0:n0], w_in[:, g0:], pad], axis=1).astype(BF16)


def _pos_weights(cmp_pos_w):
    w = cmp_pos_w.reshape(2, CMP_BLOCK // CMP_STRIDE, CMP_STRIDE, NSA_KV_HEADS)
    w = jnp.transpose(w, (1, 2, 0, 3))
    return jnp.broadcast_to(w[..., None], w.shape + (HEAD_DIM,)).reshape(2, CMP_STRIDE, KV_WIDTH)


def _mixers_out(x2d, o_gdn, o_nsa, w_o, mlp_norm_g, w_up, w_down, *, tm):
    attn = jnp.concatenate([o_gdn, o_nsa], axis=1)
    m = x2d.shape[0]
    x1 = _mm_res(attn, w_o, x2d, tm=tm, tn=min(1024, w_o.shape[1]), tk=min(2048, w_o.shape[0]))
    hid = _rms_mm(x1, mlp_norm_g, w_up, tm=min(tm, 512), tn=1024, relu2=True, out_dtype=BF16)
    return _mm_res(hid, w_down, x1, tm=tm, tn=min(1024, w_down.shape[1]), tk=min(2048, w_down.shape[0]))


def _round_up(x, n):
    return -(-x // n) * n


def _prompt_layer(x, p):
    bsz, t, d = x.shape
    m = bsz * t
    x2d = x.reshape(m, d)
    proj = _rms_mm(x2d, p["attn_norm_g"], p["w_in"], tm=512, tn=PROJ_WIDTH // 15)
    conv0 = jnp.zeros((bsz, CONV_W - 1, 3 * GDN_WIDTH), F32)
    s_zero = jnp.zeros((bsz, GDN_HEADS, HEAD_DIM, HEAD_DIM), F32)
    o_gdn, s_new = _gdn(proj, conv0, s_zero, p["gdn_conv_w"], p["gdn_a_log"], p["gdn_dt_bias"], p["gdn_norm_g"],
                        bsz=bsz, rows_per_seq=t, t_valid=t)
    qn, slc_n, win_n, gates = _nsa_prep(proj, p["q_norm_g"], p["k_norm_g"], tm=512)

    n_cmp = (t - CMP_BLOCK) // CMP_STRIDE + 1
    n_slc = -(-t // SLC_BLOCK)
    n_pages = t // PAGE_SIZE
    table = jnp.arange(bsz * n_pages, dtype=jnp.int32).reshape(bsz, n_pages)
    s0, s1 = _pool(proj, table, COL_CMP // KV_WIDTH, p["cmp_w"])
    kc, vc = _cmp_finish(s0, s1, p["cmp_phi"], p["k_norm_g"][0], n_cmp=n_cmp)
    nb = kc.shape[2]
    nsp = _round_up(n_slc, LANES)
    tq = 128
    ocmp, sel = _cmp_topk(qn, kc, vc, _overlap_matrix(nb, nsp, n_cmp, n_slc), bsz=bsz, rows_per_seq=t, tq=tq,
                          n_cmp=n_cmp, n_slc=n_slc, offset=0)
    o_nsa = _prompt_attn(qn, slc_n.astype(BF16), win_n.astype(BF16), ocmp, sel, gates, bsz=bsz, t=t, tq=tq)
    y = _mixers_out(x2d, o_gdn, o_nsa, p["w_o"], p["mlp_norm_g"], p["w_up"], p["w_down"], tm=1024)

    kv_tail = (2, NSA_KV_HEADS, HEAD_DIM)
    proj3 = proj.reshape(bsz, t, PROJ_WIDTH)
    keep = min(WINDOW, t)
    return (y.reshape(bsz, t, d),
            proj3[:, :, COL_CMP:COL_CMP + KV_WIDTH].reshape((bsz, t) + kv_tail),
            slc_n.reshape((bsz, t) + kv_tail),
            win_n.reshape((bsz, t) + kv_tail)[:, t - keep:],
            proj3[:, t - (CONV_W - 1):, 0:3 * GDN_WIDTH],
            s_new)


def _sample_layer(x, cmp_cache, slc_cache, win_cache, conv_buf, s0_state, page_table, p):
    bsz, t, d = x.shape
    tp = _round_up(t, SUBLANES)
    n_pages = page_table.shape[1]
    past = n_pages * PAGE_SIZE
    assert t <= SLC_BLOCK and past % SLC_BLOCK == 0 and t >= CONV_W - 1
    x2d = jnp.pad(x, ((0, 0), (0, tp - t), (0, 0))).reshape(bsz * tp, d)
    proj = _rms_mm(x2d, p["attn_norm_g"], p["w_in"], tm=bsz * tp, tn=PROJ_WIDTH // 15)
    o_gdn, s_new = _gdn(proj, conv_buf, s0_state, p["gdn_conv_w"], p["gdn_a_log"], p["gdn_dt_bias"],
                        p["gdn_norm_g"], bsz=bsz, rows_per_seq=tp, t_valid=t)
    qn, slc_n, win_n, gates = _nsa_prep(proj, p["q_norm_g"], p["k_norm_g"], tm=bsz * tp)

    total = past + t
    n_cmp = (total - CMP_BLOCK) // CMP_STRIDE + 1
    n_slc = -(-total // SLC_BLOCK)
    assert (n_cmp - 1) * CMP_STRIDE + CMP_BLOCK <= past
    n_phys = cmp_cache.shape[0]
    s0, s1 = _pool(cmp_cache.reshape(n_phys * PAGE_SIZE, KV_WIDTH), page_table, 0, p["cmp_w"])
    kc, vc = _cmp_finish(s0, s1, p["cmp_phi"], p["k_norm_g"][0], n_cmp=n_cmp)
    nb = kc.shape[2]
    nsp = _round_up(n_slc, LANES)
    ocmp, sel = _cmp_topk(qn, kc, vc, _overlap_matrix(nb, nsp, n_cmp, n_slc), bsz=bsz, rows_per_seq=tp, tq=tp,
                          n_cmp=n_cmp, n_slc=n_slc, offset=past)
    n_prev = win_cache.shape[1]
    o_nsa = _sample_attn(qn, slc_cache.reshape(n_phys * PAGE_SIZE, KV_WIDTH), page_table, slc_n,
                         win_cache.reshape(bsz, n_prev, KV_WIDTH), win_n, ocmp, sel, gates,
                         bsz=bsz, tq=tp, past=past)
    y = _mixers_out(x2d, o_gdn, o_nsa, p["w_o"], p["mlp_norm_g"], p["w_up"], p["w_down"], tm=bsz * tp)

    kv_tail = (2, NSA_KV_HEADS, HEAD_DIM)
    proj3 = proj.reshape(bsz, tp, PROJ_WIDTH)
    win_all = jnp.concatenate([win_cache, win_n.reshape((bsz, tp) + kv_tail)[:, :t]], axis=1)
    keep = min(WINDOW, n_prev + t)
    return (y.reshape(bsz, tp, d)[:, :t],
            proj3[:, :t, COL_CMP:COL_CMP + KV_WIDTH].reshape((bsz, t) + kv_tail),
            slc_n.reshape((bsz, tp) + kv_tail)[:, :t],
            win_all[:, n_prev + t - keep:],
            proj3[:, t - (CONV_W - 1):t, 0:3 * GDN_WIDTH],
            s_new)


def kernel(x_prompt, x_sample, cache_cmp_kv, cache_slc_kv, cache_win_kv, cache_gdn_conv, state_gdn, page_table, attn_norm_g, w_in, gdn_conv_w, gdn_a_log, gdn_dt_bias, gdn_norm_g, q_norm_g, k_norm_g, cmp_pos_w, cmp_phi, w_o, mlp_norm_g, w_up, w_down):
    depth = w_in.shape[0]
    yp, ys = x_prompt, x_sample
    per_layer = []
    for layer in range(depth):
        p = {
            "attn_norm_g": attn_norm_g[layer], "w_in": _pack_w_in(w_in[layer]),
            "gdn_conv_w": gdn_conv_w[layer], "gdn_a_log": gdn_a_log[layer], "gdn_dt_bias": gdn_dt_bias[layer],
            "gdn_norm_g": gdn_norm_g[layer], "q_norm_g": q_norm_g[layer], "k_norm_g": k_norm_g[layer],
            "cmp_w": _pos_weights(cmp_pos_w[layer]), "cmp_phi": cmp_phi[layer],
            "w_o": w_o[layer].astype(BF16), "mlp_norm_g": mlp_norm_g[layer],
            "w_up": w_up[layer].astype(BF16), "w_down": w_down[layer].astype(BF16),
        }
        yp, cmp_p, slc_p, win_p, conv_p, s_p = _prompt_layer(yp, p)
        ys, cmp_s, slc_s, win_s, conv_s, s_s = _sample_layer(
            ys, cache_cmp_kv[layer], cache_slc_kv[layer], cache_win_kv[layer], cache_gdn_conv[layer],
            state_gdn[layer], page_table, p)
        per_layer.append((cmp_p, cmp_s, slc_p, slc_s, win_p, win_s, conv_p, conv_s, s_p, s_s))
    st = [jnp.stack(z, axis=0) for z in zip(*per_layer)]
    return (yp, ys) + tuple(st)
```

```python
import functools
import math

import numpy as np
import jax
import jax.numpy as jnp
from jax import lax
from jax.experimental import pallas as pl
from jax.experimental.pallas import tpu as pltpu

F32 = jnp.float32
BF16 = jnp.bfloat16
HI = lax.Precision.HIGHEST

HEAD_DIM = 128
GDN_HEADS = 16
NSA_HEADS = 16
NSA_KV_HEADS = 4
NSA_GROUP = NSA_HEADS // NSA_KV_HEADS
GDN_WIDTH = GDN_HEADS * HEAD_DIM
NSA_WIDTH = NSA_HEADS * HEAD_DIM
KV_WIDTH = 2 * NSA_KV_HEADS * HEAD_DIM
KV_SLABS = 2 * NSA_KV_HEADS
CONV_W = 4
GDN_CHUNK = 64
GDN_HEAD_GROUP = 8
CMP_BLOCK = 32
CMP_STRIDE = 16
SLC_BLOCK = 64
SLC_TOPK = 16
SLC_LOCAL = 2
WINDOW = 512
PAGE_SIZE = 128
EPS = 1e-6
FORCE_SCORE = 1e9
NEG = -1e30
SCALE = HEAD_DIM ** -0.5

LANES = 128
SUBLANES = 8
VMEM_LIMIT = 52 * 1024 * 1024

COL_Q, COL_K, COL_V, COL_Z = 0, GDN_WIDTH, 2 * GDN_WIDTH, 3 * GDN_WIDTH
COL_NQ = 4 * GDN_WIDTH
COL_CMP = COL_NQ + NSA_WIDTH
COL_SLC = COL_CMP + KV_WIDTH
COL_WIN = COL_SLC + KV_WIDTH
COL_SMALL = COL_WIN + KV_WIDTH
PROJ_WIDTH = COL_SMALL + LANES
SMALL_B, SMALL_A, SMALL_GATE = 0, GDN_HEADS, 2 * GDN_HEADS


def _cparams(sem):
    return pltpu.CompilerParams(dimension_semantics=sem, vmem_limit_bytes=VMEM_LIMIT)


def _dot(a, b, prec=None):
    return jnp.dot(a, b, preferred_element_type=F32, precision=prec)


def _dot_nt(a, b, prec=None):
    return lax.dot_general(a, b, (((1,), (1,)), ((), ())), preferred_element_type=F32, precision=prec)


def _dot_tn(a, b, prec=None):
    return lax.dot_general(a, b, (((0,), (0,)), ((), ())), preferred_element_type=F32, precision=prec)


def _rms(x, g):
    return x * lax.rsqrt(jnp.mean(x * x, axis=-1, keepdims=True) + EPS) * g


def _silu(x):
    return x * jax.nn.sigmoid(x)


def _softplus(x):
    return jnp.maximum(x, 0.0) + jnp.log1p(jnp.exp(-jnp.abs(x)))


def _rms_mm_kernel(x_ref, g_ref, w_ref, o_ref, xn_ref, *, relu2):
    @pl.when(pl.program_id(1) == 0)
    def _():
        xn_ref[...] = _rms(x_ref[...], g_ref[...]).astype(BF16)

    y = _dot(xn_ref[...], w_ref[...])
    if relu2:
        y = jnp.square(jnp.maximum(y, 0.0))
    o_ref[...] = y.astype(o_ref.dtype)


def _rms_mm(x, g, w, *, tm, tn, relu2=False, out_dtype=F32):
    m, k = x.shape
    n = w.shape[1]
    assert m % tm == 0 and n % tn == 0
    return pl.pallas_call(
        functools.partial(_rms_mm_kernel, relu2=relu2),
        grid=(m // tm, n // tn),
        in_specs=[pl.BlockSpec((tm, k), lambda i, j: (i, 0)),
                  pl.BlockSpec((1, k), lambda i, j: (0, 0)),
                  pl.BlockSpec((k, tn), lambda i, j: (0, j))],
        out_specs=pl.BlockSpec((tm, tn), lambda i, j: (i, j)),
        out_shape=jax.ShapeDtypeStruct((m, n), out_dtype),
        scratch_shapes=[pltpu.VMEM((tm, k), BF16)],
        compiler_params=_cparams(("parallel", "arbitrary")),
        name="rms_matmul",
    )(x, g.reshape(1, k), w)


def _mm_res_kernel(a_ref, w_ref, r_ref, o_ref, acc_ref, *, nk):
    kk = pl.program_id(2)

    @pl.when(kk == 0)
    def _():
        acc_ref[...] = jnp.zeros_like(acc_ref)

    acc_ref[...] += _dot(a_ref[...], w_ref[...])

    @pl.when(kk == nk - 1)
    def _():
        o_ref[...] = r_ref[...] + acc_ref[...]


def _mm_res(a, w, res, *, tm, tn, tk):
    m, k = a.shape
    n = w.shape[1]
    assert m % tm == 0 and n % tn == 0 and k % tk == 0
    nk = k // tk
    return pl.pallas_call(
        functools.partial(_mm_res_kernel, nk=nk),
        grid=(m // tm, n // tn, nk),
        in_specs=[pl.BlockSpec((tm, tk), lambda i, j, kk: (i, kk)),
                  pl.BlockSpec((tk, tn), lambda i, j, kk: (kk, j)),
                  pl.BlockSpec((tm, tn), lambda i, j, kk: (i, j))],
        out_specs=pl.BlockSpec((tm, tn), lambda i, j, kk: (i, j)),
        out_shape=jax.ShapeDtypeStruct((m, n), F32),
        scratch_shapes=[pltpu.VMEM((tm, tn), F32)],
        compiler_params=_cparams(("parallel", "parallel", "arbitrary")),
        name="matmul_residual",
    )(a, w, res)


def _gdn_kernel(q_ref, k_ref, v_ref, z_ref, sm_ref, cbuf_ref, s0_ref, cw_ref, alog_ref, dt_ref, ng_ref,
                o_ref, sout_ref, xp_ref, st_ref, *, rows, nc, t_valid):
    C = GDN_CHUNK
    W = GDN_WIDTH
    c = pl.program_id(1)

    @pl.when(c == 0)
    def _():
        xp_ref[...] = jnp.zeros(xp_ref.shape, F32)
        xp_ref[SUBLANES - (CONV_W - 1):SUBLANES, :] = cbuf_ref[0]
        st_ref[...] = s0_ref[0]

    @pl.when(c > 0)
    def _():
        xp_ref[0:SUBLANES, :] = xp_ref[C:C + SUBLANES, :]

    xp_ref[SUBLANES:SUBLANES + rows, 0:W] = q_ref[...]
    xp_ref[SUBLANES:SUBLANES + rows, W:2 * W] = k_ref[...]
    xp_ref[SUBLANES:SUBLANES + rows, 2 * W:3 * W] = v_ref[...]

    row = lax.broadcasted_iota(jnp.int32, (C, LANES), 0)
    lane = lax.broadcasted_iota(jnp.int32, (C, LANES), 1)
    valid = (c * C + row) < t_valid
    if rows < C:
        sm = jnp.concatenate([sm_ref[...], jnp.zeros((C - rows, LANES), F32)], axis=0)
    else:
        sm = sm_ref[...]
    beta_all = jnp.where(valid, jax.nn.sigmoid(sm), 0.0)
    g_all = jnp.where(valid, -jnp.exp(alog_ref[...]) * _softplus(sm + dt_ref[...]), 0.0)
    ri = lax.broadcasted_iota(jnp.int32, (C, C), 0)
    ci = lax.broadcasted_iota(jnp.int32, (C, C), 1)
    tri = ri >= ci
    strict = ri > ci
    gcum = _dot(tri.astype(F32), g_all, HI)
    li = lax.broadcasted_iota(jnp.int32, (LANES, LANES), 0)
    lj = lax.broadcasted_iota(jnp.int32, (LANES, LANES), 1)
    gcum_t = _dot_nt((li == lj).astype(F32), gcum, HI)
    valid_col = valid[:, 0:1]
    eye = (ri == ci).astype(F32)
    n_sq = int(math.log2(C)) - 1

    def mm(a, b):
        return _dot(a.astype(BF16), b.astype(BF16))

    def mm_nt(a, b):
        return _dot_nt(a.astype(BF16), b.astype(BF16))

    def conv(col):
        acc = None
        for j in range(CONV_W):
            term = (xp_ref[pl.ds(SUBLANES - (CONV_W - 1) + j, C), col:col + LANES]
                    * cw_ref[j:j + 1, col:col + LANES])
            acc = term if acc is None else acc + term
        return _silu(acc)

    def head_group(hs):
        n = len(hs)
        qs = [conv(h * HEAD_DIM) for h in hs]
        ks = [conv(W + h * HEAD_DIM) for h in hs]
        vs = [conv(2 * W + h * HEAD_DIM) for h in hs]
        qs = [q * lax.rsqrt(jnp.sum(q * q, axis=-1, keepdims=True) + EPS) * SCALE for q in qs]
        ks = [jnp.where(valid_col, k * lax.rsqrt(jnp.sum(k * k, axis=-1, keepdims=True) + EPS), 0.0) for k in ks]
        gcs = [gcum[:, SMALL_A + h:SMALL_A + h + 1] for h in hs]
        bhs = [beta_all[:, SMALL_B + h:SMALL_B + h + 1] for h in hs]
        grs = [gcum_t[SMALL_A + h:SMALL_A + h + 1, :] for h in hs]
        decays = [jnp.where(tri, jnp.exp(jnp.where(tri, gc - gr, 0.0)), 0.0) for gc, gr in zip(gcs, grs)]
        egs = [jnp.exp(gc) for gc in gcs]

        kbs = [k * b for k, b in zip(ks, bhs)]
        kqs = [mm_nt(jnp.concatenate([kb, q], axis=0), k) for kb, q, k in zip(kbs, qs, ks)]
        nmats = [jnp.where(strict, -(kq[0:C] * d), 0.0) for kq, d in zip(kqs, decays)]
        aqks = [jnp.where(tri, kq[C:2 * C] * d, 0.0) for kq, d in zip(kqs, decays)]
        pinvs = [eye + nm for nm in nmats]
        npows = nmats
        for _ in range(n_sq):
            npows = [mm(np_, np_) for np_ in npows]
            pinvs = [p + mm(p, np_) for p, np_ in zip(pinvs, npows)]
        uws = [mm(p, jnp.concatenate([v * b, kb * eg], axis=1))
               for p, v, b, kb, eg in zip(pinvs, vs, bhs, kbs, egs)]

        ss = [st_ref[h] for h in hs]
        wss = [mm(jnp.concatenate([uw[:, HEAD_DIM:2 * HEAD_DIM], q * eg], axis=0), s)
               for uw, q, eg, s in zip(uws, qs, egs, ss)]
        v_news = [uw[:, 0:HEAD_DIM] - ws[0:C] for uw, ws in zip(uws, wss)]
        os_ = [ws[C:2 * C] + mm(aqk, vn) for ws, aqk, vn in zip(wss, aqks, v_news)]
        for i in range(n):
            g_last = gcs[i][C - 1:C, :]
            k_dec = ks[i] * jnp.exp(g_last - gcs[i])
            st_ref[hs[i]] = ss[i] * jnp.exp(g_last) + _dot_tn(k_dec.astype(BF16), v_news[i].astype(BF16))
        for i in range(n):
            col = hs[i] * HEAD_DIM
            o = _rms(os_[i], ng_ref[...])
            o_ref[:, col:col + HEAD_DIM] = (o[0:rows] * _silu(z_ref[:, col:col + HEAD_DIM])).astype(o_ref.dtype)

    for h0 in range(0, GDN_HEADS, GDN_HEAD_GROUP):
        head_group(list(range(h0, h0 + GDN_HEAD_GROUP)))

    @pl.when(c == nc - 1)
    def _():
        sout_ref[0] = st_ref[...]


def _gdn(proj, conv_buf, s0, conv_w, a_log, dt_bias, norm_g, *, bsz, rows_per_seq, t_valid):
    C = GDN_CHUNK
    if rows_per_seq >= C:
        assert rows_per_seq % C == 0
        rows, nc = C, rows_per_seq // C
    else:
        assert rows_per_seq % SUBLANES == 0
        rows, nc = rows_per_seq, 1
    wblk = GDN_WIDTH
    zeros = jnp.zeros((LANES - 2 * GDN_HEADS,), F32)
    alog_row = jnp.concatenate([jnp.zeros((GDN_HEADS,), F32), a_log, zeros]).reshape(1, LANES)
    dt_row = jnp.concatenate([jnp.zeros((GDN_HEADS,), F32), dt_bias, zeros]).reshape(1, LANES)
    row_map = lambda b, c: b * nc + c
    in_specs = [
        pl.BlockSpec((rows, wblk), lambda b, c: (row_map(b, c), COL_Q // wblk)),
        pl.BlockSpec((rows, wblk), lambda b, c: (row_map(b, c), COL_K // wblk)),
        pl.BlockSpec((rows, wblk), lambda b, c: (row_map(b, c), COL_V // wblk)),
        pl.BlockSpec((rows, wblk), lambda b, c: (row_map(b, c), COL_Z // wblk)),
        pl.BlockSpec((rows, LANES), lambda b, c: (row_map(b, c), COL_SMALL // LANES)),
        pl.BlockSpec((1, CONV_W - 1, 3 * GDN_WIDTH), lambda b, c: (b, 0, 0)),
        pl.BlockSpec((1, GDN_HEADS, HEAD_DIM, HEAD_DIM), lambda b, c: (b, 0, 0, 0)),
        pl.BlockSpec((CONV_W, 3 * GDN_WIDTH), lambda b, c: (0, 0)),
        pl.BlockSpec((1, LANES), lambda b, c: (0, 0)),
        pl.BlockSpec((1, LANES), lambda b, c: (0, 0)),
        pl.BlockSpec((1, HEAD_DIM), lambda b, c: (0, 0)),
    ]
    out_specs = [
        pl.BlockSpec((rows, wblk), lambda b, c: (row_map(b, c), 0)),
        pl.BlockSpec((1, GDN_HEADS, HEAD_DIM, HEAD_DIM), lambda b, c: (b, 0, 0, 0)),
    ]
    return pl.pallas_call(
        functools.partial(_gdn_kernel, rows=rows, nc=nc, t_valid=t_valid),
        grid=(bsz, nc),
        in_specs=in_specs,
        out_specs=out_specs,
        out_shape=[jax.ShapeDtypeStruct((bsz * rows_per_seq, GDN_WIDTH), BF16),
                   jax.ShapeDtypeStruct((bsz, GDN_HEADS, HEAD_DIM, HEAD_DIM), F32)],
        scratch_shapes=[pltpu.VMEM((C + SUBLANES, 3 * GDN_WIDTH), F32),
                        pltpu.VMEM((GDN_HEADS, HEAD_DIM, HEAD_DIM), F32)],
        compiler_params=_cparams(("parallel", "arbitrary")),
        name="gdn_chunk_scan",
    )(proj, proj, proj, proj, proj, conv_buf, s0, conv_w, alog_row, dt_row, norm_g.reshape(1, HEAD_DIM))


def _prep_kernel(nq_ref, slc_ref, win_ref, sm_ref, qg_ref, kg_ref, q_out, slc_out, win_out, gate_out):
    qg = qg_ref[...]
    for h in range(NSA_HEADS):
        sl = slice(h * HEAD_DIM, (h + 1) * HEAD_DIM)
        q_out[:, sl] = _rms(nq_ref[:, sl], qg)
    half = NSA_KV_HEADS * HEAD_DIM
    for src, dst, gi in ((slc_ref, slc_out, 1), (win_ref, win_out, 2)):
        kg = kg_ref[gi:gi + 1, :]
        for g in range(NSA_KV_HEADS):
            sl = slice(g * HEAD_DIM, (g + 1) * HEAD_DIM)
            dst[:, sl] = _rms(src[:, sl], kg)
        dst[:, half:2 * half] = src[:, half:2 * half]
    sig = jax.nn.sigmoid(sm_ref[...])
    per = NSA_GROUP * 3
    for g in range(NSA_KV_HEADS):
        gate_out[g] = pltpu.roll(sig, LANES - (SMALL_GATE + per * g), axis=1)


def _nsa_prep(proj, q_norm_g, k_norm_g, *, tm):
    m = proj.shape[0]
    assert m % tm == 0
    return pl.pallas_call(
        _prep_kernel,
        grid=(m // tm,),
        in_specs=[pl.BlockSpec((tm, NSA_WIDTH), lambda i: (i, COL_NQ // NSA_WIDTH)),
                  pl.BlockSpec((tm, KV_WIDTH), lambda i: (i, COL_SLC // KV_WIDTH)),
                  pl.BlockSpec((tm, KV_WIDTH), lambda i: (i, COL_WIN // KV_WIDTH)),
                  pl.BlockSpec((tm, LANES), lambda i: (i, COL_SMALL // LANES)),
                  pl.BlockSpec((1, HEAD_DIM), lambda i: (0, 0)),
                  pl.BlockSpec((3, HEAD_DIM), lambda i: (0, 0))],
        out_specs=[pl.BlockSpec((tm, NSA_WIDTH), lambda i: (i, 0)),
                   pl.BlockSpec((tm, KV_WIDTH), lambda i: (i, 0)),
                   pl.BlockSpec((tm, KV_WIDTH), lambda i: (i, 0)),
                   pl.BlockSpec((NSA_KV_HEADS, tm, LANES), lambda i: (0, i, 0))],
        out_shape=[jax.ShapeDtypeStruct((m, NSA_WIDTH), F32),
                   jax.ShapeDtypeStruct((m, KV_WIDTH), F32),
                   jax.ShapeDtypeStruct((m, KV_WIDTH), F32),
                   jax.ShapeDtypeStruct((NSA_KV_HEADS, m, LANES), F32)],
        compiler_params=_cparams(("parallel",)),
        name="nsa_prep",
    )(proj, proj, proj, proj, q_norm_g.reshape(1, HEAD_DIM), k_norm_g)


def _pool_kernel(tbl_ref, *refs, pg):
    del tbl_ref
    page_refs = refs[:pg]
    w_ref, s0_ref, s1_ref = refs[pg:]
    sub = PAGE_SIZE // CMP_STRIDE
    out_rows = sub * KV_SLABS
    for u in range(pg):
        x = page_refs[u][...].reshape(sub, CMP_STRIDE, KV_SLABS, HEAD_DIM)
        s0_ref[0, u * out_rows:(u + 1) * out_rows, :] = jnp.sum(x * w_ref[0][None], axis=1).reshape(out_rows, HEAD_DIM)
        s1_ref[0, u * out_rows:(u + 1) * out_rows, :] = jnp.sum(x * w_ref[1][None], axis=1).reshape(out_rows, HEAD_DIM)


def _pool(rows2d, table, wexp):
    bsz, n_pages = table.shape
    pg = math.gcd(n_pages, 8)
    sub = PAGE_SIZE // CMP_STRIDE
    page_rows = PAGE_SIZE * KV_SLABS
    out_rows = pg * sub * KV_SLABS

    def page_spec(u):
        return pl.BlockSpec((page_rows, HEAD_DIM), lambda b, p, tbl: (tbl[b, p * pg + u], 0))

    grid_spec = pltpu.PrefetchScalarGridSpec(
        num_scalar_prefetch=1,
        grid=(bsz, n_pages // pg),
        in_specs=[page_spec(u) for u in range(pg)]
        + [pl.BlockSpec((2, CMP_STRIDE, KV_SLABS, HEAD_DIM), lambda b, p, tbl: (0, 0, 0, 0))],
        out_specs=[pl.BlockSpec((1, out_rows, HEAD_DIM), lambda b, p, tbl: (b, p, 0)),
                   pl.BlockSpec((1, out_rows, HEAD_DIM), lambda b, p, tbl: (b, p, 0))],
    )
    shape = jax.ShapeDtypeStruct((bsz, n_pages * sub * KV_SLABS, HEAD_DIM), F32)
    return pl.pallas_call(
        functools.partial(_pool_kernel, pg=pg), grid_spec=grid_spec, out_shape=[shape, shape],
        compiler_params=_cparams(("parallel", "arbitrary")), name="cmp_pool",
    )(table, *([rows2d] * pg), wexp)


def _cmp_fin_kernel(s0_ref, s1_ref, phi_ref, kg_ref, kc_ref, vc_ref, *, n_cmp):
    nb = s0_ref.shape[1] // KV_SLABS
    live = lax.broadcasted_iota(jnp.int32, (nb, HEAD_DIM), 0) < n_cmp
    zero_row = jnp.zeros((1, HEAD_DIM), F32)

    def pooled(slab):
        first = s0_ref[0, pl.ds(slab, nb, stride=KV_SLABS), :]
        second = s1_ref[0, pl.ds(KV_SLABS + slab, nb - 1, stride=KV_SLABS), :]
        return first + jnp.concatenate([second, zero_row], axis=0)

    for g in range(NSA_KV_HEADS):
        kc_ref[0, g] = jnp.where(live, _rms(_dot(pooled(g), phi_ref[0, g], HI), kg_ref[...]), 0.0)
        vc_ref[0, g] = jnp.where(live, _dot(pooled(NSA_KV_HEADS + g), phi_ref[1, g], HI), 0.0)


def _cmp_finish(s0, s1, phi, kg, *, n_cmp):
    bsz = s0.shape[0]
    nb = s0.shape[1] // KV_SLABS
    assert n_cmp <= nb - 1
    shape = jax.ShapeDtypeStruct((bsz, NSA_KV_HEADS, nb, HEAD_DIM), F32)
    return pl.pallas_call(
        functools.partial(_cmp_fin_kernel, n_cmp=n_cmp),
        grid=(bsz,),
        in_specs=[pl.BlockSpec((1, nb * KV_SLABS, HEAD_DIM), lambda b: (b, 0, 0)),
                  pl.BlockSpec((1, nb * KV_SLABS, HEAD_DIM), lambda b: (b, 0, 0)),
                  pl.BlockSpec((2, NSA_KV_HEADS, HEAD_DIM, HEAD_DIM), lambda b: (0, 0, 0, 0)),
                  pl.BlockSpec((1, HEAD_DIM), lambda b: (0, 0))],
        out_specs=[pl.BlockSpec((1, NSA_KV_HEADS, nb, HEAD_DIM), lambda b: (b, 0, 0, 0)),
                   pl.BlockSpec((1, NSA_KV_HEADS, nb, HEAD_DIM), lambda b: (b, 0, 0, 0))],
        out_shape=[shape, shape],
        compiler_params=_cparams(("parallel",)),
        name="cmp_finish",
    )(s0, s1, phi, kg.reshape(1, HEAD_DIM))


def _stack_heads(ref, g, tq, dtype):
    parts = [ref[:, (g * NSA_GROUP + r) * HEAD_DIM:(g * NSA_GROUP + r + 1) * HEAD_DIM].astype(dtype)
             for r in range(NSA_GROUP)]
    return jnp.concatenate(parts, axis=0)


def _cmp_topk_kernel(q_ref, kc_ref, vc_ref, ov_ref, ocmp_ref, sel_ref, *, tq, n_cmp, n_slc, offset):
    i = pl.program_id(1)
    nb = kc_ref.shape[2]
    nsp = ov_ref.shape[1]
    rows = NSA_GROUP * tq
    r4 = lax.broadcasted_iota(jnp.int32, (rows, nb), 0)
    n4 = lax.broadcasted_iota(jnp.int32, (rows, nb), 1)
    tpos = offset + i * tq + (r4 & (tq - 1))
    valid = (n4 * CMP_STRIDE + (CMP_BLOCK - 1) <= tpos) & (n4 < n_cmp)

    blk = lax.broadcasted_iota(jnp.int32, (tq, nsp), 1)
    tq_pos = offset + i * tq + lax.broadcasted_iota(jnp.int32, (tq, nsp), 0)
    cur = tq_pos // SLC_BLOCK
    forced = (blk == 0) | ((blk <= cur) & (blk > cur - SLC_LOCAL))
    future = blk > cur
    in_range = blk < n_slc
    topk = min(SLC_TOPK, n_slc)

    for g in range(NSA_KV_HEADS):
        q4 = _stack_heads(q_ref, g, tq, F32)
        s = _dot_nt(q4, kc_ref[0, g], HI) * SCALE
        s = jnp.where(valid, s, -jnp.inf)
        m = jnp.max(s, axis=-1, keepdims=True)
        m = jnp.where(m == -jnp.inf, 0.0, m)
        p = jnp.exp(s - m)
        p = p / jnp.maximum(jnp.sum(p, axis=-1, keepdims=True), jnp.finfo(jnp.float32).tiny)
        o = _dot(p, vc_ref[0, g], HI)
        psum = p[0:tq]
        for r in range(NSA_GROUP):
            col = (g * NSA_GROUP + r) * HEAD_DIM
            ocmp_ref[:, col:col + HEAD_DIM] = o[r * tq:(r + 1) * tq]
            if r:
                psum = psum + p[r * tq:(r + 1) * tq]
        imp = _dot(psum, ov_ref[...], HI)
        score = jnp.where(forced, FORCE_SCORE, jnp.where(future, -FORCE_SCORE, imp))
        score = jnp.where(in_range, score, -jnp.inf)
        rank = jnp.zeros((tq, nsp), jnp.int32)
        for j in range(n_slc):
            cj = score[:, j:j + 1]
            ahead = (cj > score) | ((cj == score) & (blk > j))
            rank = rank + ahead.astype(jnp.int32)
        sel_ref[g] = ((rank < topk) & in_range).astype(F32)


def _cmp_topk(qn, kc, vc, overlap, *, bsz, rows_per_seq, tq, n_cmp, n_slc, offset):
    m = qn.shape[0]
    nt = rows_per_seq // tq
    nb = kc.shape[2]
    nsp = overlap.shape[1]
    return pl.pallas_call(
        functools.partial(_cmp_topk_kernel, tq=tq, n_cmp=n_cmp, n_slc=n_slc, offset=offset),
        grid=(bsz, nt),
        in_specs=[pl.BlockSpec((tq, NSA_WIDTH), lambda b, i: (b * nt + i, 0)),
                  pl.BlockSpec((1, NSA_KV_HEADS, nb, HEAD_DIM), lambda b, i: (b, 0, 0, 0)),
                  pl.BlockSpec((1, NSA_KV_HEADS, nb, HEAD_DIM), lambda b, i: (b, 0, 0, 0)),
                  pl.BlockSpec((nb, nsp), lambda b, i: (0, 0))],
        out_specs=[pl.BlockSpec((tq, NSA_WIDTH), lambda b, i: (b * nt + i, 0)),
                   pl.BlockSpec((NSA_KV_HEADS, tq, nsp), lambda b, i: (0, b * nt + i, 0))],
        out_shape=[jax.ShapeDtypeStruct((m, NSA_WIDTH), F32),
                   jax.ShapeDtypeStruct((NSA_KV_HEADS, m, nsp), F32)],
        compiler_params=_cparams(("parallel", "arbitrary")),
        name="cmp_attn_topk",
    )(qn, kc, vc, overlap)


def _overlap_matrix(nb, nsp, n_cmp, n_slc):
    cs = np.arange(nb) * CMP_STRIDE
    ss = np.arange(nsp) * SLC_BLOCK
    lo = np.maximum(cs[:, None], ss[None, :])
    hi = np.minimum(cs[:, None] + CMP_BLOCK, ss[None, :] + SLC_BLOCK)
    ov = (np.maximum(hi - lo, 0) / CMP_BLOCK).astype(np.float32)
    ov[n_cmp:, :] = 0.0
    ov[:, n_slc:] = 0.0
    return jnp.asarray(ov)


def _flash_init(m_ref, l_ref, acc_ref):
    m_ref[...] = jnp.full(m_ref.shape, NEG, F32)
    l_ref[...] = jnp.zeros(l_ref.shape, F32)
    acc_ref[...] = jnp.zeros(acc_ref.shape, F32)


def _flash_update(s, mask, v, m_ref, l_ref, acc_ref, prec=None):
    s = jnp.where(mask, s, NEG)
    m_prev = m_ref[...]
    m_new = jnp.maximum(m_prev, jnp.max(s, axis=-1, keepdims=True))
    alpha = jnp.exp(m_prev - m_new)
    p = jnp.exp(s - m_new)
    l_ref[...] = alpha * l_ref[...] + jnp.sum(p, axis=-1, keepdims=True)
    acc_ref[...] = alpha * acc_ref[...] + _dot(p.astype(v.dtype), v, prec)
    m_ref[...] = m_new


def _pattn_kernel(q_ref, ks_ref, vs_ref, kw_ref, vw_ref, ocmp_ref, sel_ref, gate_ref, exp_ref, o_ref,
                  sexp_ref, m_ref, l_ref, acc_ref, *, tq):
    i = pl.program_id(2)
    tk = tq
    rows = NSA_GROUP * tq
    q4 = jnp.concatenate([q_ref[:, r * HEAD_DIM:(r + 1) * HEAD_DIM].astype(BF16) for r in range(NSA_GROUP)],
                         axis=0)
    sexp_ref[...] = _dot(sel_ref[0].astype(BF16), exp_ref[...])
    qpos = i * tq + lax.broadcasted_iota(jnp.int32, (tq, tk), 0)
    lane = lax.broadcasted_iota(jnp.int32, (tq, tk), 1)

    def tile4(x):
        return jnp.concatenate([x] * NSA_GROUP, axis=0)

    def slc_step(j, carry):
        start = pl.multiple_of(j * tk, tk)
        k = ks_ref[pl.ds(start, tk), :]
        v = vs_ref[pl.ds(start, tk), :]
        s = _dot_nt(q4, k) * SCALE
        mask = (sexp_ref[:, pl.ds(start, tk)] > 0.5) & (j * tk + lane <= qpos)
        _flash_update(s, tile4(mask), v, m_ref, l_ref, acc_ref)
        return carry

    _flash_init(m_ref, l_ref, acc_ref)
    lax.fori_loop(0, i + 1, slc_step, 0)
    o_slc = acc_ref[...] / l_ref[...]

    def win_step(j, carry):
        start = pl.multiple_of(j * tk, tk)
        k = kw_ref[pl.ds(start, tk), :]
        v = vw_ref[pl.ds(start, tk), :]
        s = _dot_nt(q4, k) * SCALE
        diff = qpos - (j * tk + lane)
        mask = (diff >= 0) & (diff < WINDOW)
        _flash_update(s, tile4(mask), v, m_ref, l_ref, acc_ref)
        return carry

    _flash_init(m_ref, l_ref, acc_ref)
    lax.fori_loop(jnp.maximum(i - WINDOW // tk, 0), i + 1, win_step, 0)
    o_win = acc_ref[...] / l_ref[...]

    gates = gate_ref[0]
    for r in range(NSA_GROUP):
        sl = slice(r * tq, (r + 1) * tq)
        col = slice(r * HEAD_DIM, (r + 1) * HEAD_DIM)
        o = (gates[:, 3 * r:3 * r + 1] * ocmp_ref[:, col]
             + gates[:, 3 * r + 1:3 * r + 2] * o_slc[sl]
             + gates[:, 3 * r + 2:3 * r + 3] * o_win[sl])
        o_ref[:, col] = o.astype(o_ref.dtype)


def _prompt_attn(qn, slc_bf, win_bf, ocmp, sel, gates, *, bsz, t, tq):
    m = qn.shape[0]
    nt = t // tq
    gw = NSA_GROUP * HEAD_DIM
    half = NSA_KV_HEADS
    nsp = sel.shape[2]
    n_slc = -(-t // SLC_BLOCK)
    expand = np.zeros((nsp, t), np.float32)
    expand[np.arange(t) // SLC_BLOCK, np.arange(t)] = 1.0
    assert n_slc <= nsp
    kv_spec = lambda which: pl.BlockSpec((t, HEAD_DIM), lambda b, g, i: (b, which * half + g))
    return pl.pallas_call(
        functools.partial(_pattn_kernel, tq=tq),
        grid=(bsz, NSA_KV_HEADS, nt),
        in_specs=[pl.BlockSpec((tq, gw), lambda b, g, i: (b * nt + i, g)),
                  kv_spec(0), kv_spec(1), kv_spec(0), kv_spec(1),
                  pl.BlockSpec((tq, gw), lambda b, g, i: (b * nt + i, g)),
                  pl.BlockSpec((1, tq, nsp), lambda b, g, i: (g, b * nt + i, 0)),
                  pl.BlockSpec((1, tq, LANES), lambda b, g, i: (g, b * nt + i, 0)),
                  pl.BlockSpec((nsp, t), lambda b, g, i: (0, 0))],
        out_specs=pl.BlockSpec((tq, gw), lambda b, g, i: (b * nt + i, g)),
        out_shape=jax.ShapeDtypeStruct((m, NSA_WIDTH), BF16),
        scratch_shapes=[pltpu.VMEM((tq, t), F32),
                        pltpu.VMEM((NSA_GROUP * tq, 1), F32),
                        pltpu.VMEM((NSA_GROUP * tq, 1), F32),
                        pltpu.VMEM((NSA_GROUP * tq, HEAD_DIM), F32)],
        compiler_params=_cparams(("parallel", "parallel", "arbitrary")),
        name="nsa_prompt_attn",
    )(qn, slc_bf, slc_bf, win_bf, win_bf, ocmp, sel, gates, jnp.asarray(expand, BF16))


def _sattn_kernel(tbl_ref, q_ref, *refs, tq, n_steps, pg, past):
    del tbl_ref
    page_refs = refs[:pg]
    (snew_ref, wcache_ref, wnew_ref, ocmp_ref, sel_ref, gate_ref, exp_ref, o_ref, s_buf, v_buf) = refs[pg:]
    j = pl.program_id(1)
    rows = NSA_GROUP * tq
    half = NSA_KV_HEADS * HEAD_DIM
    nsp = sel_ref.shape[2]
    n_prev = wcache_ref.shape[0] // KV_SLABS

    def tile4(x):
        return jnp.concatenate([x] * NSA_GROUP, axis=0)

    q4_bf = [_stack_heads(q_ref, g, tq, BF16) for g in range(NSA_KV_HEADS)]
    for u in range(pg):
        key0 = pl.multiple_of((j * pg + u) * PAGE_SIZE, PAGE_SIZE)
        for g in range(NSA_KV_HEADS):
            k = page_refs[u][pl.ds(g, PAGE_SIZE, stride=KV_SLABS), :].astype(BF16)
            s_buf[g * rows:(g + 1) * rows, pl.ds(key0, PAGE_SIZE)] = _dot_nt(q4_bf[g], k)
            v_buf[g, pl.ds(key0, PAGE_SIZE), :] = (
                page_refs[u][pl.ds(NSA_KV_HEADS + g, PAGE_SIZE, stride=KV_SLABS), :].astype(BF16))

    @pl.when(j == n_steps - 1)
    def _():
        blk_lane = lax.broadcasted_iota(jnp.int32, (tq, nsp), 1)
        trow = lax.broadcasted_iota(jnp.int32, (tq, tq), 0)
        tcol = lax.broadcasted_iota(jnp.int32, (tq, tq), 1)
        causal = tcol <= trow
        wrow = lax.broadcasted_iota(jnp.int32, (tq, n_prev), 0)
        wcol = lax.broadcasted_iota(jnp.int32, (tq, n_prev), 1)
        wdiff = wrow + n_prev - wcol
        wmask = (wdiff >= 0) & (wdiff < WINDOW)
        ndiff = trow - tcol
        nmask = (ndiff >= 0) & (ndiff < WINDOW)
        gates_all = gate_ref[...]
        for g in range(NSA_KV_HEADS):
            q4 = _stack_heads(q_ref, g, tq, F32)
            selg = sel_ref[g]
            sexp = _dot(selg.astype(BF16), exp_ref[...])
            sc = jnp.where(tile4(sexp > 0.5), s_buf[g * rows:(g + 1) * rows, :] * SCALE, NEG)
            kn = snew_ref[:, g * HEAD_DIM:(g + 1) * HEAD_DIM]
            vn = snew_ref[:, half + g * HEAD_DIM:half + (g + 1) * HEAD_DIM]
            flag = jnp.zeros((tq, tq), jnp.bool_)
            for u in range(tq):
                col = jnp.sum(jnp.where(blk_lane == (past + u) // SLC_BLOCK, selg, 0.0), axis=1, keepdims=True) > 0.5
                flag = flag | (col & (tcol == u))
            sn = jnp.where(tile4(flag & causal), _dot_nt(q4, kn, HI) * SCALE, NEG)
            mx = jnp.maximum(jnp.max(sc, axis=-1, keepdims=True), jnp.max(sn, axis=-1, keepdims=True))
            pc = jnp.exp(sc - mx)
            pn = jnp.exp(sn - mx)
            den = jnp.sum(pc, axis=-1, keepdims=True) + jnp.sum(pn, axis=-1, keepdims=True)
            o_slc = (_dot(pc.astype(BF16), v_buf[g]) + _dot(pn, vn, HI)) / den

            kc = wcache_ref[pl.ds(g, n_prev, stride=KV_SLABS), :].astype(BF16)
            vc = wcache_ref[pl.ds(NSA_KV_HEADS + g, n_prev, stride=KV_SLABS), :].astype(BF16)
            kw = wnew_ref[:, g * HEAD_DIM:(g + 1) * HEAD_DIM]
            vw = wnew_ref[:, half + g * HEAD_DIM:half + (g + 1) * HEAD_DIM]
            sc = jnp.where(tile4(wmask), _dot_nt(q4.astype(BF16), kc) * SCALE, NEG)
            sn = jnp.where(tile4(nmask), _dot_nt(q4, kw, HI) * SCALE, NEG)
            mx = jnp.maximum(jnp.max(sc, axis=-1, keepdims=True), jnp.max(sn, axis=-1, keepdims=True))
            pc = jnp.exp(sc - mx)
            pn = jnp.exp(sn - mx)
            den = jnp.sum(pc, axis=-1, keepdims=True) + jnp.sum(pn, axis=-1, keepdims=True)
            o_win = (_dot(pc.astype(BF16), vc) + _dot(pn, vw, HI)) / den

            gates = gates_all[g]
            for r in range(NSA_GROUP):
                sl = slice(r * tq, (r + 1) * tq)
                col = slice((g * NSA_GROUP + r) * HEAD_DIM, (g * NSA_GROUP + r + 1) * HEAD_DIM)
                o = (gates[:, 3 * r:3 * r + 1] * ocmp_ref[:, col]
                     + gates[:, 3 * r + 1:3 * r + 2] * o_slc[sl]
                     + gates[:, 3 * r + 2:3 * r + 3] * o_win[sl])
                o_ref[:, col] = o.astype(o_ref.dtype)


def _sample_attn(qn, cache2d, page_table, slc_new, win_cache2d, win_row0, win_new, ocmp, sel, gates,
                 *, bsz, tq, past, n_prev):
    n_pages = page_table.shape[1]
    pg = math.gcd(n_pages, 4)
    n_steps = n_pages // pg
    nsp = sel.shape[2]
    rows = NSA_GROUP * tq
    page_rows = PAGE_SIZE * KV_SLABS
    expand = np.zeros((nsp, past), np.float32)
    expand[np.arange(past) // SLC_BLOCK, np.arange(past)] = 1.0

    def page_spec(u):
        return pl.BlockSpec((page_rows, HEAD_DIM), lambda b, j, tbl: (tbl[b, j * pg + u], 0))

    grid_spec = pltpu.PrefetchScalarGridSpec(
        num_scalar_prefetch=1,
        grid=(bsz, n_steps),
        in_specs=[pl.BlockSpec((tq, NSA_WIDTH), lambda b, j, tbl: (b, 0))]
        + [page_spec(u) for u in range(pg)]
        + [pl.BlockSpec((tq, KV_WIDTH), lambda b, j, tbl: (b, 0)),
           pl.BlockSpec((n_prev * KV_SLABS, HEAD_DIM), lambda b, j, tbl: (win_row0 + b, 0)),
           pl.BlockSpec((tq, KV_WIDTH), lambda b, j, tbl: (b, 0)),
           pl.BlockSpec((tq, NSA_WIDTH), lambda b, j, tbl: (b, 0)),
           pl.BlockSpec((NSA_KV_HEADS, tq, nsp), lambda b, j, tbl: (0, b, 0)),
           pl.BlockSpec((NSA_KV_HEADS, tq, LANES), lambda b, j, tbl: (0, b, 0)),
           pl.BlockSpec((nsp, past), lambda b, j, tbl: (0, 0))],
        out_specs=pl.BlockSpec((tq, NSA_WIDTH), lambda b, j, tbl: (b, 0)),
        scratch_shapes=[pltpu.VMEM((NSA_KV_HEADS * rows, past), F32),
                        pltpu.VMEM((NSA_KV_HEADS, past, HEAD_DIM), BF16)],
    )
    return pl.pallas_call(
        functools.partial(_sattn_kernel, tq=tq, n_steps=n_steps, pg=pg, past=past),
        grid_spec=grid_spec,
        out_shape=jax.ShapeDtypeStruct((bsz * tq, NSA_WIDTH), BF16),
        compiler_params=_cparams(("parallel", "arbitrary")),
        name="nsa_sample_attn",
    )(page_table, qn, *([cache2d] * pg), slc_new, win_cache2d, win_new, ocmp, sel, gates,
      jnp.asarray(expand, BF16))


def _pack_w_in(w_in):
    gw = GDN_WIDTH
    b0 = 4 * gw
    n0 = b0 + 2 * GDN_HEADS
    g0 = n0 + NSA_WIDTH + 3 * KV_WIDTH
    k = w_in.shape[0]
    pad = jnp.zeros((k, LANES - 2 * GDN_HEADS - 3 * NSA_HEADS), w_in.dtype)
    return jnp.concatenate([w_in[:, :b0], w_in[:, n0:g0], w_in[:, b0:n0], w_in[:, g0:], pad], axis=1).astype(BF16)


def _pos_weights(cmp_pos_w):
    w = cmp_pos_w.reshape(2, CMP_BLOCK // CMP_STRIDE, CMP_STRIDE, NSA_KV_HEADS)
    w = jnp.transpose(w, (1, 2, 0, 3))
    return jnp.broadcast_to(w[..., None], w.shape + (HEAD_DIM,)).reshape(2, CMP_STRIDE, KV_SLABS, HEAD_DIM)


def _mixers_out(x2d, o_gdn, o_nsa, w_o, mlp_norm_g, w_up, w_down, *, tm):
    attn = jnp.concatenate([o_gdn, o_nsa], axis=1)
    m = x2d.shape[0]
    x1 = _mm_res(attn, w_o, x2d, tm=tm, tn=min(1024, w_o.shape[1]), tk=min(2048, w_o.shape[0]))
    hid = _rms_mm(x1, mlp_norm_g, w_up, tm=min(tm, 512), tn=1024, relu2=True, out_dtype=BF16)
    return _mm_res(hid, w_down, x1, tm=tm, tn=min(1024, w_down.shape[1]), tk=min(2048, w_down.shape[0]))


def _round_up(x, n):
    return -(-x // n) * n


def _prompt_layer(x, p):
    bsz, t, d = x.shape
    m = bsz * t
    x2d = x.reshape(m, d)
    proj = _rms_mm(x2d, p["attn_norm_g"], p["w_in"], tm=512, tn=PROJ_WIDTH // 15)
    conv0 = jnp.zeros((bsz, CONV_W - 1, 3 * GDN_WIDTH), F32)
    s_zero = jnp.zeros((bsz, GDN_HEADS, HEAD_DIM, HEAD_DIM), F32)
    o_gdn, s_new = _gdn(proj, conv0, s_zero, p["gdn_conv_w"], p["gdn_a_log"], p["gdn_dt_bias"], p["gdn_norm_g"],
                        bsz=bsz, rows_per_seq=t, t_valid=t)
    qn, slc_n, win_n, gates = _nsa_prep(proj, p["q_norm_g"], p["k_norm_g"], tm=512)

    n_cmp = (t - CMP_BLOCK) // CMP_STRIDE + 1
    n_slc = -(-t // SLC_BLOCK)
    n_pages = t // PAGE_SIZE
    table = jnp.arange(bsz * n_pages, dtype=jnp.int32).reshape(bsz, n_pages)
    kv_tail = (2, NSA_KV_HEADS, HEAD_DIM)
    proj3 = proj.reshape(bsz, t, PROJ_WIDTH)
    cmp_new = proj3[:, :, COL_CMP:COL_CMP + KV_WIDTH].reshape((bsz, t) + kv_tail)
    s0, s1 = _pool(cmp_new.reshape(m * KV_SLABS, HEAD_DIM), table, p["cmp_w"])
    kc, vc = _cmp_finish(s0, s1, p["cmp_phi"], p["k_norm_g"][0], n_cmp=n_cmp)
    nb = kc.shape[2]
    nsp = _round_up(n_slc, LANES)
    tq = 128
    ocmp, sel = _cmp_topk(qn, kc, vc, _overlap_matrix(nb, nsp, n_cmp, n_slc), bsz=bsz, rows_per_seq=t, tq=tq,
                          n_cmp=n_cmp, n_slc=n_slc, offset=0)
    o_nsa = _prompt_attn(qn, slc_n.astype(BF16), win_n.astype(BF16), ocmp, sel, gates, bsz=bsz, t=t, tq=tq)
    y = _mixers_out(x2d, o_gdn, o_nsa, p["w_o"], p["mlp_norm_g"], p["w_up"], p["w_down"], tm=1024)

    keep = min(WINDOW, t)
    return (y.reshape(bsz, t, d),
            cmp_new,
            slc_n.reshape((bsz, t) + kv_tail),
            win_n.reshape((bsz, t) + kv_tail)[:, t - keep:],
            proj3[:, t - (CONV_W - 1):, 0:3 * GDN_WIDTH],
            s_new)


def _sample_layer(x, layer, cache_cmp_kv, cache_slc_kv, cache_win_kv, conv_buf, s0_state, page_table, p):
    bsz, t, d = x.shape
    tp = _round_up(t, SUBLANES)
    n_pages = page_table.shape[1]
    past = n_pages * PAGE_SIZE
    assert t <= SLC_BLOCK and past % SLC_BLOCK == 0 and t >= CONV_W - 1
    x2d = jnp.pad(x, ((0, 0), (0, tp - t), (0, 0))).reshape(bsz * tp, d)
    proj = _rms_mm(x2d, p["attn_norm_g"], p["w_in"], tm=bsz * tp, tn=PROJ_WIDTH // 15)
    o_gdn, s_new = _gdn(proj, conv_buf, s0_state, p["gdn_conv_w"], p["gdn_a_log"], p["gdn_dt_bias"],
                        p["gdn_norm_g"], bsz=bsz, rows_per_seq=tp, t_valid=t)
    qn, slc_n, win_n, gates = _nsa_prep(proj, p["q_norm_g"], p["k_norm_g"], tm=bsz * tp)

    total = past + t
    n_cmp = (total - CMP_BLOCK) // CMP_STRIDE + 1
    n_slc = -(-total // SLC_BLOCK)
    assert (n_cmp - 1) * CMP_STRIDE + CMP_BLOCK <= past
    depth, n_phys = cache_cmp_kv.shape[:2]
    table = page_table + layer * n_phys
    cache_rows = depth * n_phys * PAGE_SIZE * KV_SLABS
    s0, s1 = _pool(cache_cmp_kv.reshape(cache_rows, HEAD_DIM), table, p["cmp_w"])
    kc, vc = _cmp_finish(s0, s1, p["cmp_phi"], p["k_norm_g"][0], n_cmp=n_cmp)
    nb = kc.shape[2]
    nsp = _round_up(n_slc, LANES)
    ocmp, sel = _cmp_topk(qn, kc, vc, _overlap_matrix(nb, nsp, n_cmp, n_slc), bsz=bsz, rows_per_seq=tp, tq=tp,
                          n_cmp=n_cmp, n_slc=n_slc, offset=past)
    n_prev = cache_win_kv.shape[2]
    o_nsa = _sample_attn(qn, cache_slc_kv.reshape(cache_rows, HEAD_DIM), table, slc_n,
                         cache_win_kv.reshape(depth * bsz * n_prev * KV_SLABS, HEAD_DIM), layer * bsz,
                         win_n, ocmp, sel, gates, bsz=bsz, tq=tp, past=past, n_prev=n_prev)
    y = _mixers_out(x2d, o_gdn, o_nsa, p["w_o"], p["mlp_norm_g"], p["w_up"], p["w_down"], tm=bsz * tp)

    kv_tail = (2, NSA_KV_HEADS, HEAD_DIM)
    proj3 = proj.reshape(bsz, tp, PROJ_WIDTH)
    win_all = jnp.concatenate([cache_win_kv[layer], win_n.reshape((bsz, tp) + kv_tail)[:, :t]], axis=1)
    keep = min(WINDOW, n_prev + t)
    return (y.reshape(bsz, tp, d)[:, :t],
            proj3[:, :t, COL_CMP:COL_CMP + KV_WIDTH].reshape((bsz, t) + kv_tail),
            slc_n.reshape((bsz, tp) + kv_tail)[:, :t],
            win_all[:, n_prev + t - keep:],
            proj3[:, t - (CONV_W - 1):t, 0:3 * GDN_WIDTH],
            s_new)


def kernel(x_prompt, x_sample, cache_cmp_kv, cache_slc_kv, cache_win_kv, cache_gdn_conv, state_gdn, page_table, attn_norm_g, w_in, gdn_conv_w, gdn_a_log, gdn_dt_bias, gdn_norm_g, q_norm_g, k_norm_g, cmp_pos_w, cmp_phi, w_o, mlp_norm_g, w_up, w_down):
    depth = w_in.shape[0]
    yp, ys = x_prompt, x_sample
    per_layer = []
    for layer in range(depth):
        p = {
            "attn_norm_g": attn_norm_g[layer], "w_in": _pack_w_in(w_in[layer]),
            "gdn_conv_w": gdn_conv_w[layer], "gdn_a_log": gdn_a_log[layer], "gdn_dt_bias": gdn_dt_bias[layer],
            "gdn_norm_g": gdn_norm_g[layer], "q_norm_g": q_norm_g[layer], "k_norm_g": k_norm_g[layer],
            "cmp_w": _pos_weights(cmp_pos_w[layer]), "cmp_phi": cmp_phi[layer],
            "w_o": w_o[layer].astype(BF16), "mlp_norm_g": mlp_norm_g[layer],
            "w_up": w_up[layer].astype(BF16), "w_down": w_down[layer].astype(BF16),
        }
        yp, cmp_p, slc_p, win_p, conv_p, s_p = _prompt_layer(yp, p)
        ys, cmp_s, slc_s, win_s, conv_s, s_s = _sample_layer(
            ys, layer, cache_cmp_kv, cache_slc_kv, cache_win_kv, cache_gdn_conv[layer],
            state_gdn[layer], page_table, p)
        per_layer.append((cmp_p, cmp_s, slc_p, slc_s, win_p, win_s, conv_p, conv_s, s_p, s_s))
    st = [jnp.stack(z, axis=0) for z in zip(*per_layer)]
    return (yp, ys) + tuple(st)
```

```python
import functools
import math

import numpy as np
import jax
import jax.numpy as jnp
from jax import lax
from jax.experimental import pallas as pl
from jax.experimental.pallas import tpu as pltpu

F32 = jnp.float32
BF16 = jnp.bfloat16
HI = lax.Precision.HIGHEST

HEAD_DIM = 128
GDN_HEADS = 16
NSA_HEADS = 16
NSA_KV_HEADS = 4
NSA_GROUP = NSA_HEADS // NSA_KV_HEADS
GDN_WIDTH = GDN_HEADS * HEAD_DIM
NSA_WIDTH = NSA_HEADS * HEAD_DIM
KV_WIDTH = 2 * NSA_KV_HEADS * HEAD_DIM
KV_SLABS = 2 * NSA_KV_HEADS
CONV_W = 4
GDN_CHUNK = 64
GDN_HEAD_GROUP = 8
CMP_BLOCK = 32
CMP_STRIDE = 16
SLC_BLOCK = 64
SLC_TOPK = 16
SLC_LOCAL = 2
WINDOW = 512
PAGE_SIZE = 128
EPS = 1e-6
FORCE_SCORE = 1e9
NEG = -1e30
SCALE = HEAD_DIM ** -0.5

LANES = 128
SUBLANES = 8
VMEM_LIMIT = 52 * 1024 * 1024

COL_Q, COL_K, COL_V, COL_Z = 0, GDN_WIDTH, 2 * GDN_WIDTH, 3 * GDN_WIDTH
PROJ_A_WIDTH = 4 * GDN_WIDTH
COL_NQ = 0
COL_CMP = COL_NQ + NSA_WIDTH
COL_SLC = COL_CMP + KV_WIDTH
COL_WIN = COL_SLC + KV_WIDTH
PROJ_B_WIDTH = COL_WIN + KV_WIDTH
SMALL_B, SMALL_A, SMALL_GATE = 0, GDN_HEADS, 2 * GDN_HEADS
W_IN_B0 = PROJ_A_WIDTH
W_IN_N0 = W_IN_B0 + 2 * GDN_HEADS
W_IN_G0 = W_IN_N0 + PROJ_B_WIDTH


def _cparams(sem):
    return pltpu.CompilerParams(dimension_semantics=sem, vmem_limit_bytes=VMEM_LIMIT)


def _dot(a, b, prec=None):
    return jnp.dot(a, b, preferred_element_type=F32, precision=prec)


def _dot_nt(a, b, prec=None):
    return lax.dot_general(a, b, (((1,), (1,)), ((), ())), preferred_element_type=F32, precision=prec)


def _dot_tn(a, b, prec=None):
    return lax.dot_general(a, b, (((0,), (0,)), ((), ())), preferred_element_type=F32, precision=prec)


def _rms(x, g):
    return x * lax.rsqrt(jnp.mean(x * x, axis=-1, keepdims=True) + EPS) * g


def _silu(x):
    return x * jax.nn.sigmoid(x)


def _softplus(x):
    return jnp.maximum(x, 0.0) + jnp.log1p(jnp.exp(-jnp.abs(x)))


def _rms_cast_kernel(x_ref, g_ref, o_ref):
    o_ref[...] = _rms(x_ref[...], g_ref[...]).astype(o_ref.dtype)


def _rms_cast(x, g, *, tm):
    m, k = x.shape
    assert m % tm == 0
    return pl.pallas_call(
        _rms_cast_kernel,
        grid=(m // tm,),
        in_specs=[pl.BlockSpec((tm, k), lambda i: (i, 0)), pl.BlockSpec((1, k), lambda i: (0, 0))],
        out_specs=pl.BlockSpec((tm, k), lambda i: (i, 0)),
        out_shape=jax.ShapeDtypeStruct((m, k), BF16),
        compiler_params=_cparams(("parallel",)),
        name="rms_cast",
    )(x, g.reshape(1, k))


def _mm_nw_kernel(*refs, n_a, relu2, has_res):
    a_refs = refs[:n_a]
    w_ref = refs[n_a]
    r_ref = refs[n_a + 1] if has_res else None
    o_ref, wbf_ref = refs[-2:]

    @pl.when(pl.program_id(1) == 0)
    def _():
        wbf_ref[...] = w_ref[...].astype(BF16)

    y, k0 = None, 0
    for a_ref in a_refs:
        part = _dot(a_ref[...], wbf_ref[k0:k0 + a_ref.shape[1], :])
        y = part if y is None else y + part
        k0 += a_ref.shape[1]
    if relu2:
        y = jnp.square(jnp.maximum(y, 0.0))
    if has_res:
        y = y + r_ref[...]
    o_ref[...] = y.astype(o_ref.dtype)


def _mm_nw(acts, w, *, n, tm, tn, res=None, relu2=False, out_dtype=F32):
    m = acts[0].shape[0]
    k = sum(a.shape[1] for a in acts)
    assert m % tm == 0 and n % tn == 0 and w.shape[0] == k
    in_specs = [pl.BlockSpec((tm, a.shape[1]), lambda j, i: (i, 0)) for a in acts]
    in_specs.append(pl.BlockSpec((k, tn), lambda j, i: (0, j)))
    args = list(acts) + [w]
    if res is not None:
        in_specs.append(pl.BlockSpec((tm, tn), lambda j, i: (i, j)))
        args.append(res)
    return pl.pallas_call(
        functools.partial(_mm_nw_kernel, n_a=len(acts), relu2=relu2, has_res=res is not None),
        grid=(n // tn, m // tm),
        in_specs=in_specs,
        out_specs=pl.BlockSpec((tm, tn), lambda j, i: (i, j)),
        out_shape=jax.ShapeDtypeStruct((m, n), out_dtype),
        scratch_shapes=[pltpu.VMEM((k, tn), BF16)],
        compiler_params=_cparams(("parallel", "arbitrary")),
        name="matmul_wcast",
    )(*args)


def _mm_res_kernel(a_ref, w_ref, r_ref, o_ref, acc_ref, *, nk):
    kk = pl.program_id(2)

    @pl.when(kk == 0)
    def _():
        acc_ref[...] = jnp.zeros_like(acc_ref)

    acc_ref[...] += _dot(a_ref[...], w_ref[...])

    @pl.when(kk == nk - 1)
    def _():
        o_ref[...] = r_ref[...] + acc_ref[...]


def _mm_res(a, w, res, *, tm, tn, tk):
    m, k = a.shape
    n = w.shape[1]
    assert m % tm == 0 and n % tn == 0 and k % tk == 0
    nk = k // tk
    return pl.pallas_call(
        functools.partial(_mm_res_kernel, nk=nk),
        grid=(m // tm, n // tn, nk),
        in_specs=[pl.BlockSpec((tm, tk), lambda i, j, kk: (i, kk)),
                  pl.BlockSpec((tk, tn), lambda i, j, kk: (kk, j)),
                  pl.BlockSpec((tm, tn), lambda i, j, kk: (i, j))],
        out_specs=pl.BlockSpec((tm, tn), lambda i, j, kk: (i, j)),
        out_shape=jax.ShapeDtypeStruct((m, n), F32),
        scratch_shapes=[pltpu.VMEM((tm, tn), F32)],
        compiler_params=_cparams(("parallel", "parallel", "arbitrary")),
        name="matmul_residual",
    )(a, w, res)


def _gdn_kernel(q_ref, k_ref, v_ref, z_ref, sm_ref, cbuf_ref, s0_ref, cw_ref, alog_ref, dt_ref, ng_ref,
                o_ref, sout_ref, xp_ref, st_ref, *, rows, nc, t_valid):
    C = GDN_CHUNK
    W = GDN_WIDTH
    c = pl.program_id(1)

    @pl.when(c == 0)
    def _():
        xp_ref[...] = jnp.zeros(xp_ref.shape, F32)
        xp_ref[SUBLANES - (CONV_W - 1):SUBLANES, :] = cbuf_ref[0]
        st_ref[...] = s0_ref[0]

    @pl.when(c > 0)
    def _():
        xp_ref[0:SUBLANES, :] = xp_ref[C:C + SUBLANES, :]

    xp_ref[SUBLANES:SUBLANES + rows, 0:W] = q_ref[...]
    xp_ref[SUBLANES:SUBLANES + rows, W:2 * W] = k_ref[...]
    xp_ref[SUBLANES:SUBLANES + rows, 2 * W:3 * W] = v_ref[...]

    row = lax.broadcasted_iota(jnp.int32, (C, LANES), 0)
    lane = lax.broadcasted_iota(jnp.int32, (C, LANES), 1)
    valid = (c * C + row) < t_valid
    if rows < C:
        sm = jnp.concatenate([sm_ref[...], jnp.zeros((C - rows, LANES), F32)], axis=0)
    else:
        sm = sm_ref[...]
    beta_all = jnp.where(valid, jax.nn.sigmoid(sm), 0.0)
    g_all = jnp.where(valid, -jnp.exp(alog_ref[...]) * _softplus(sm + dt_ref[...]), 0.0)
    ri = lax.broadcasted_iota(jnp.int32, (C, C), 0)
    ci = lax.broadcasted_iota(jnp.int32, (C, C), 1)
    tri = ri >= ci
    strict = ri > ci
    gcum = _dot(tri.astype(F32), g_all, HI)
    li = lax.broadcasted_iota(jnp.int32, (LANES, LANES), 0)
    lj = lax.broadcasted_iota(jnp.int32, (LANES, LANES), 1)
    gcum_t = _dot_nt((li == lj).astype(F32), gcum, HI)
    valid_col = valid[:, 0:1]
    eye = (ri == ci).astype(F32)
    n_sq = int(math.log2(C)) - 1

    def mm(a, b):
        return _dot(a.astype(BF16), b.astype(BF16))

    def mm_nt(a, b):
        return _dot_nt(a.astype(BF16), b.astype(BF16))

    def conv(col):
        acc = None
        for j in range(CONV_W):
            term = (xp_ref[pl.ds(SUBLANES - (CONV_W - 1) + j, C), col:col + LANES]
                    * cw_ref[j:j + 1, col:col + LANES])
            acc = term if acc is None else acc + term
        return _silu(acc)

    def head_group(hs):
        n = len(hs)
        qs = [conv(h * HEAD_DIM) for h in hs]
        ks = [conv(W + h * HEAD_DIM) for h in hs]
        vs = [conv(2 * W + h * HEAD_DIM) for h in hs]
        qs = [q * lax.rsqrt(jnp.sum(q * q, axis=-1, keepdims=True) + EPS) * SCALE for q in qs]
        ks = [jnp.where(valid_col, k * lax.rsqrt(jnp.sum(k * k, axis=-1, keepdims=True) + EPS), 0.0) for k in ks]
        gcs = [gcum[:, SMALL_A + h:SMALL_A + h + 1] for h in hs]
        bhs = [beta_all[:, SMALL_B + h:SMALL_B + h + 1] for h in hs]
        grs = [gcum_t[SMALL_A + h:SMALL_A + h + 1, :] for h in hs]
        decays = [jnp.where(tri, jnp.exp(jnp.where(tri, gc - gr, 0.0)), 0.0) for gc, gr in zip(gcs, grs)]
        egs = [jnp.exp(gc) for gc in gcs]

        kbs = [k * b for k, b in zip(ks, bhs)]
        kqs = [mm_nt(jnp.concatenate([kb, q], axis=0), k) for kb, q, k in zip(kbs, qs, ks)]
        nmats = [jnp.where(strict, -(kq[0:C] * d), 0.0) for kq, d in zip(kqs, decays)]
        aqks = [jnp.where(tri, kq[C:2 * C] * d, 0.0) for kq, d in zip(kqs, decays)]
        pinvs = [eye + nm for nm in nmats]
        npows = nmats
        for _ in range(n_sq):
            npows = [mm(np_, np_) for np_ in npows]
            pinvs = [p + mm(p, np_) for p, np_ in zip(pinvs, npows)]
        uws = [mm(p, jnp.concatenate([v * b, kb * eg], axis=1))
               for p, v, b, kb, eg in zip(pinvs, vs, bhs, kbs, egs)]

        ss = [st_ref[h] for h in hs]
        wss = [mm(jnp.concatenate([uw[:, HEAD_DIM:2 * HEAD_DIM], q * eg], axis=0), s)
               for uw, q, eg, s in zip(uws, qs, egs, ss)]
        v_news = [uw[:, 0:HEAD_DIM] - ws[0:C] for uw, ws in zip(uws, wss)]
        os_ = [ws[C:2 * C] + mm(aqk, vn) for ws, aqk, vn in zip(wss, aqks, v_news)]
        for i in range(n):
            g_last = gcs[i][C - 1:C, :]
            k_dec = ks[i] * jnp.exp(g_last - gcs[i])
            st_ref[hs[i]] = ss[i] * jnp.exp(g_last) + _dot_tn(k_dec.astype(BF16), v_news[i].astype(BF16))
        for i in range(n):
            col = hs[i] * HEAD_DIM
            o = _rms(os_[i], ng_ref[...])
            o_ref[:, col:col + HEAD_DIM] = (o[0:rows] * _silu(z_ref[:, col:col + HEAD_DIM])).astype(o_ref.dtype)

    for h0 in range(0, GDN_HEADS, GDN_HEAD_GROUP):
        head_group(list(range(h0, h0 + GDN_HEAD_GROUP)))

    @pl.when(c == nc - 1)
    def _():
        sout_ref[0] = st_ref[...]


def _gdn(proj, small, conv_buf, s0, conv_w, a_log, dt_bias, norm_g, *, bsz, rows_per_seq, t_valid):
    C = GDN_CHUNK
    if rows_per_seq >= C:
        assert rows_per_seq % C == 0
        rows, nc = C, rows_per_seq // C
    else:
        assert rows_per_seq % SUBLANES == 0
        rows, nc = rows_per_seq, 1
    wblk = GDN_WIDTH
    zeros = jnp.zeros((LANES - 2 * GDN_HEADS,), F32)
    alog_row = jnp.concatenate([jnp.zeros((GDN_HEADS,), F32), a_log, zeros]).reshape(1, LANES)
    dt_row = jnp.concatenate([jnp.zeros((GDN_HEADS,), F32), dt_bias, zeros]).reshape(1, LANES)
    row_map = lambda b, c: b * nc + c
    in_specs = [
        pl.BlockSpec((rows, wblk), lambda b, c: (row_map(b, c), COL_Q // wblk)),
        pl.BlockSpec((rows, wblk), lambda b, c: (row_map(b, c), COL_K // wblk)),
        pl.BlockSpec((rows, wblk), lambda b, c: (row_map(b, c), COL_V // wblk)),
        pl.BlockSpec((rows, wblk), lambda b, c: (row_map(b, c), COL_Z // wblk)),
        pl.BlockSpec((rows, LANES), lambda b, c: (row_map(b, c), 0)),
        pl.BlockSpec((1, CONV_W - 1, 3 * GDN_WIDTH), lambda b, c: (b, 0, 0)),
        pl.BlockSpec((1, GDN_HEADS, HEAD_DIM, HEAD_DIM), lambda b, c: (b, 0, 0, 0)),
        pl.BlockSpec((CONV_W, 3 * GDN_WIDTH), lambda b, c: (0, 0)),
        pl.BlockSpec((1, LANES), lambda b, c: (0, 0)),
        pl.BlockSpec((1, LANES), lambda b, c: (0, 0)),
        pl.BlockSpec((1, HEAD_DIM), lambda b, c: (0, 0)),
    ]
    out_specs = [
        pl.BlockSpec((rows, wblk), lambda b, c: (row_map(b, c), 0)),
        pl.BlockSpec((1, GDN_HEADS, HEAD_DIM, HEAD_DIM), lambda b, c: (b, 0, 0, 0)),
    ]
    return pl.pallas_call(
        functools.partial(_gdn_kernel, rows=rows, nc=nc, t_valid=t_valid),
        grid=(bsz, nc),
        in_specs=in_specs,
        out_specs=out_specs,
        out_shape=[jax.ShapeDtypeStruct((bsz * rows_per_seq, GDN_WIDTH), BF16),
                   jax.ShapeDtypeStruct((bsz, GDN_HEADS, HEAD_DIM, HEAD_DIM), F32)],
        scratch_shapes=[pltpu.VMEM((C + SUBLANES, 3 * GDN_WIDTH), F32),
                        pltpu.VMEM((GDN_HEADS, HEAD_DIM, HEAD_DIM), F32)],
        compiler_params=_cparams(("parallel", "arbitrary")),
        name="gdn_chunk_scan",
    )(proj, proj, proj, proj, small, conv_buf, s0, conv_w, alog_row, dt_row, norm_g.reshape(1, HEAD_DIM))


def _prep_kernel(nq_ref, slc_ref, win_ref, sm_ref, qg_ref, kg_ref,
                 q_out, slc_out, win_out, gate_out, slc_bf_out, win_bf_out):
    qg = qg_ref[...]
    for h in range(NSA_HEADS):
        sl = slice(h * HEAD_DIM, (h + 1) * HEAD_DIM)
        q_out[:, sl] = _rms(nq_ref[:, sl], qg)
    half = NSA_KV_HEADS * HEAD_DIM
    for src, dst, dst_bf, gi in ((slc_ref, slc_out, slc_bf_out, 1), (win_ref, win_out, win_bf_out, 2)):
        kg = kg_ref[gi:gi + 1, :]
        for g in range(NSA_KV_HEADS):
            sl = slice(g * HEAD_DIM, (g + 1) * HEAD_DIM)
            kn = _rms(src[:, sl], kg)
            dst[:, sl] = kn
            dst_bf[:, sl] = kn.astype(BF16)
        v = src[:, half:2 * half]
        dst[:, half:2 * half] = v
        dst_bf[:, half:2 * half] = v.astype(BF16)
    sig = jax.nn.sigmoid(sm_ref[...])
    per = NSA_GROUP * 3
    for g in range(NSA_KV_HEADS):
        gate_out[g] = pltpu.roll(sig, LANES - (SMALL_GATE + per * g), axis=1)


def _nsa_prep(proj, small, q_norm_g, k_norm_g, *, tm):
    m = proj.shape[0]
    assert m % tm == 0
    return pl.pallas_call(
        _prep_kernel,
        grid=(m // tm,),
        in_specs=[pl.BlockSpec((tm, NSA_WIDTH), lambda i: (i, COL_NQ // NSA_WIDTH)),
                  pl.BlockSpec((tm, KV_WIDTH), lambda i: (i, COL_SLC // KV_WIDTH)),
                  pl.BlockSpec((tm, KV_WIDTH), lambda i: (i, COL_WIN // KV_WIDTH)),
                  pl.BlockSpec((tm, LANES), lambda i: (i, 0)),
                  pl.BlockSpec((1, HEAD_DIM), lambda i: (0, 0)),
                  pl.BlockSpec((3, HEAD_DIM), lambda i: (0, 0))],
        out_specs=[pl.BlockSpec((tm, NSA_WIDTH), lambda i: (i, 0)),
                   pl.BlockSpec((tm, KV_WIDTH), lambda i: (i, 0)),
                   pl.BlockSpec((tm, KV_WIDTH), lambda i: (i, 0)),
                   pl.BlockSpec((NSA_KV_HEADS, tm, LANES), lambda i: (0, i, 0)),
                   pl.BlockSpec((tm, KV_WIDTH), lambda i: (i, 0)),
                   pl.BlockSpec((tm, KV_WIDTH), lambda i: (i, 0))],
        out_shape=[jax.ShapeDtypeStruct((m, NSA_WIDTH), F32),
                   jax.ShapeDtypeStruct((m, KV_WIDTH), F32),
                   jax.ShapeDtypeStruct((m, KV_WIDTH), F32),
                   jax.ShapeDtypeStruct((NSA_KV_HEADS, m, LANES), F32),
                   jax.ShapeDtypeStruct((m, KV_WIDTH), BF16),
                   jax.ShapeDtypeStruct((m, KV_WIDTH), BF16)],
        compiler_params=_cparams(("parallel",)),
        name="nsa_prep",
    )(proj, proj, proj, small, q_norm_g.reshape(1, HEAD_DIM), k_norm_g)


def _pool_kernel(tbl_ref, *refs, pg):
    del tbl_ref
    page_refs = refs[:pg]
    w_ref, s0_ref, s1_ref = refs[pg:]
    sub = PAGE_SIZE // CMP_STRIDE
    out_rows = sub * KV_SLABS
    for u in range(pg):
        x = page_refs[u][...].reshape(sub, CMP_STRIDE, KV_SLABS, HEAD_DIM)
        s0_ref[0, u * out_rows:(u + 1) * out_rows, :] = jnp.sum(x * w_ref[0][None], axis=1).reshape(out_rows, HEAD_DIM)
        s1_ref[0, u * out_rows:(u + 1) * out_rows, :] = jnp.sum(x * w_ref[1][None], axis=1).reshape(out_rows, HEAD_DIM)


def _pool(rows2d, table, wexp):
    bsz, n_pages = table.shape
    pg = math.gcd(n_pages, 8)
    sub = PAGE_SIZE // CMP_STRIDE
    page_rows = PAGE_SIZE * KV_SLABS
    out_rows = pg * sub * KV_SLABS

    def page_spec(u):
        return pl.BlockSpec((page_rows, HEAD_DIM), lambda b, p, tbl: (tbl[b, p * pg + u], 0))

    grid_spec = pltpu.PrefetchScalarGridSpec(
        num_scalar_prefetch=1,
        grid=(bsz, n_pages // pg),
        in_specs=[page_spec(u) for u in range(pg)]
        + [pl.BlockSpec((2, CMP_STRIDE, KV_SLABS, HEAD_DIM), lambda b, p, tbl: (0, 0, 0, 0))],
        out_specs=[pl.BlockSpec((1, out_rows, HEAD_DIM), lambda b, p, tbl: (b, p, 0)),
                   pl.BlockSpec((1, out_rows, HEAD_DIM), lambda b, p, tbl: (b, p, 0))],
    )
    shape = jax.ShapeDtypeStruct((bsz, n_pages * sub * KV_SLABS, HEAD_DIM), F32)
    return pl.pallas_call(
        functools.partial(_pool_kernel, pg=pg), grid_spec=grid_spec, out_shape=[shape, shape],
        compiler_params=_cparams(("parallel", "arbitrary")), name="cmp_pool",
    )(table, *([rows2d] * pg), wexp)


def _cmp_fin_kernel(s0_ref, s1_ref, phi_ref, kg_ref, kc_ref, vc_ref, *, n_cmp):
    nb = s0_ref.shape[1] // KV_SLABS
    live = lax.broadcasted_iota(jnp.int32, (nb, HEAD_DIM), 0) < n_cmp
    zero_row = jnp.zeros((1, HEAD_DIM), F32)

    def pooled(slab):
        first = s0_ref[0, pl.ds(slab, nb, stride=KV_SLABS), :]
        second = s1_ref[0, pl.ds(KV_SLABS + slab, nb - 1, stride=KV_SLABS), :]
        return first + jnp.concatenate([second, zero_row], axis=0)

    for g in range(NSA_KV_HEADS):
        kc_ref[0, g] = jnp.where(live, _rms(_dot(pooled(g), phi_ref[0, g], HI), kg_ref[...]), 0.0)
        vc_ref[0, g] = jnp.where(live, _dot(pooled(NSA_KV_HEADS + g), phi_ref[1, g], HI), 0.0)


def _cmp_finish(s0, s1, phi, kg, *, n_cmp):
    bsz = s0.shape[0]
    nb = s0.shape[1] // KV_SLABS
    assert n_cmp <= nb - 1
    shape = jax.ShapeDtypeStruct((bsz, NSA_KV_HEADS, nb, HEAD_DIM), F32)
    return pl.pallas_call(
        functools.partial(_cmp_fin_kernel, n_cmp=n_cmp),
        grid=(bsz,),
        in_specs=[pl.BlockSpec((1, nb * KV_SLABS, HEAD_DIM), lambda b: (b, 0, 0)),
                  pl.BlockSpec((1, nb * KV_SLABS, HEAD_DIM), lambda b: (b, 0, 0)),
                  pl.BlockSpec((2, NSA_KV_HEADS, HEAD_DIM, HEAD_DIM), lambda b: (0, 0, 0, 0)),
                  pl.BlockSpec((1, HEAD_DIM), lambda b: (0, 0))],
        out_specs=[pl.BlockSpec((1, NSA_KV_HEADS, nb, HEAD_DIM), lambda b: (b, 0, 0, 0)),
                   pl.BlockSpec((1, NSA_KV_HEADS, nb, HEAD_DIM), lambda b: (b, 0, 0, 0))],
        out_shape=[shape, shape],
        compiler_params=_cparams(("parallel",)),
        name="cmp_finish",
    )(s0, s1, phi, kg.reshape(1, HEAD_DIM))


def _stack_heads(ref, g, tq, dtype):
    parts = [ref[:, (g * NSA_GROUP + r) * HEAD_DIM:(g * NSA_GROUP + r + 1) * HEAD_DIM].astype(dtype)
             for r in range(NSA_GROUP)]
    return jnp.concatenate(parts, axis=0)


def _cmp_topk_kernel(q_ref, kc_ref, vc_ref, ov_ref, ocmp_ref, sel_ref, *, tq, n_cmp, n_slc, offset):
    i = pl.program_id(1)
    nb = kc_ref.shape[2]
    nsp = ov_ref.shape[1]
    rows = NSA_GROUP * tq
    r4 = lax.broadcasted_iota(jnp.int32, (rows, nb), 0)
    n4 = lax.broadcasted_iota(jnp.int32, (rows, nb), 1)
    tpos = offset + i * tq + (r4 & (tq - 1))
    valid = (n4 * CMP_STRIDE + (CMP_BLOCK - 1) <= tpos) & (n4 < n_cmp)

    blk = lax.broadcasted_iota(jnp.int32, (tq, nsp), 1)
    tq_pos = offset + i * tq + lax.broadcasted_iota(jnp.int32, (tq, nsp), 0)
    cur = tq_pos // SLC_BLOCK
    forced = (blk == 0) | ((blk <= cur) & (blk > cur - SLC_LOCAL))
    future = blk > cur
    in_range = blk < n_slc
    topk = min(SLC_TOPK, n_slc)

    for g in range(NSA_KV_HEADS):
        q4 = _stack_heads(q_ref, g, tq, F32)
        s = _dot_nt(q4, kc_ref[0, g], HI) * SCALE
        s = jnp.where(valid, s, -jnp.inf)
        m = jnp.max(s, axis=-1, keepdims=True)
        m = jnp.where(m == -jnp.inf, 0.0, m)
        p = jnp.exp(s - m)
        p = p / jnp.maximum(jnp.sum(p, axis=-1, keepdims=True), jnp.finfo(jnp.float32).tiny)
        o = _dot(p, vc_ref[0, g], HI)
        psum = p[0:tq]
        for r in range(NSA_GROUP):
            col = (g * NSA_GROUP + r) * HEAD_DIM
            ocmp_ref[:, col:col + HEAD_DIM] = o[r * tq:(r + 1) * tq]
            if r:
                psum = psum + p[r * tq:(r + 1) * tq]
        imp = _dot(psum, ov_ref[...], HI)
        score = jnp.where(forced, FORCE_SCORE, jnp.where(future, -FORCE_SCORE, imp))
        score = jnp.where(in_range, score, -jnp.inf)
        rank = jnp.zeros((tq, nsp), jnp.int32)
        for j in range(n_slc):
            cj = score[:, j:j + 1]
            ahead = (cj > score) | ((cj == score) & (blk > j))
            rank = rank + ahead.astype(jnp.int32)
        sel_ref[g] = ((rank < topk) & in_range).astype(F32)


def _cmp_topk(qn, kc, vc, overlap, *, bsz, rows_per_seq, tq, n_cmp, n_slc, offset):
    m = qn.shape[0]
    nt = rows_per_seq // tq
    nb = kc.shape[2]
    nsp = overlap.shape[1]
    return pl.pallas_call(
        functools.partial(_cmp_topk_kernel, tq=tq, n_cmp=n_cmp, n_slc=n_slc, offset=offset),
        grid=(bsz, nt),
        in_specs=[pl.BlockSpec((tq, NSA_WIDTH), lambda b, i: (b * nt + i, 0)),
                  pl.BlockSpec((1, NSA_KV_HEADS, nb, HEAD_DIM), lambda b, i: (b, 0, 0, 0)),
                  pl.BlockSpec((1, NSA_KV_HEADS, nb, HEAD_DIM), lambda b, i: (b, 0, 0, 0)),
                  pl.BlockSpec((nb, nsp), lambda b, i: (0, 0))],
        out_specs=[pl.BlockSpec((tq, NSA_WIDTH), lambda b, i: (b * nt + i, 0)),
                   pl.BlockSpec((NSA_KV_HEADS, tq, nsp), lambda b, i: (0, b * nt + i, 0))],
        out_shape=[jax.ShapeDtypeStruct((m, NSA_WIDTH), F32),
                   jax.ShapeDtypeStruct((NSA_KV_HEADS, m, nsp), F32)],
        compiler_params=_cparams(("parallel", "arbitrary")),
        name="cmp_attn_topk",
    )(qn, kc, vc, overlap)


def _overlap_matrix(nb, nsp, n_cmp, n_slc):
    cs = np.arange(nb) * CMP_STRIDE
    ss = np.arange(nsp) * SLC_BLOCK
    lo = np.maximum(cs[:, None], ss[None, :])
    hi = np.minimum(cs[:, None] + CMP_BLOCK, ss[None, :] + SLC_BLOCK)
    ov = (np.maximum(hi - lo, 0) / CMP_BLOCK).astype(np.float32)
    ov[n_cmp:, :] = 0.0
    ov[:, n_slc:] = 0.0
    return jnp.asarray(ov)


def _flash_init(m_ref, l_ref, acc_ref):
    m_ref[...] = jnp.full(m_ref.shape, NEG, F32)
    l_ref[...] = jnp.zeros(l_ref.shape, F32)
    acc_ref[...] = jnp.zeros(acc_ref.shape, F32)


def _flash_update(s, mask, v, m_ref, l_ref, acc_ref, prec=None):
    s = jnp.where(mask, s, NEG)
    m_prev = m_ref[...]
    m_new = jnp.maximum(m_prev, jnp.max(s, axis=-1, keepdims=True))
    alpha = jnp.exp(m_prev - m_new)
    p = jnp.exp(s - m_new)
    l_ref[...] = alpha * l_ref[...] + jnp.sum(p, axis=-1, keepdims=True)
    acc_ref[...] = alpha * acc_ref[...] + _dot(p.astype(v.dtype), v, prec)
    m_ref[...] = m_new


def _pattn_kernel(q_ref, ks_ref, vs_ref, kw_ref, vw_ref, ocmp_ref, sel_ref, gate_ref, exp_ref, o_ref,
                  sexp_ref, m_ref, l_ref, acc_ref, *, tq, tk, wlen):
    i = pl.program_id(2)
    t = ks_ref.shape[0]
    q4 = jnp.concatenate([(q_ref[:, r * HEAD_DIM:(r + 1) * HEAD_DIM] * SCALE).astype(BF16)
                          for r in range(NSA_GROUP)], axis=0)
    sexp_ref[...] = _dot(sel_ref[0].astype(BF16), exp_ref[...])
    qpos = i * tq + lax.broadcasted_iota(jnp.int32, (tq, tk), 0)
    lane = lax.broadcasted_iota(jnp.int32, (tq, tk), 1)

    def tile4(x):
        return jnp.concatenate([x] * NSA_GROUP, axis=0)

    def slc_step(j, carry):
        start = pl.multiple_of(j * tk, tk)
        k = ks_ref[pl.ds(start, tk), :]
        mask = (sexp_ref[:, pl.ds(start, tk)] > 0.5) & (start + lane <= qpos)
        sls = [slice(r * tq, (r + 1) * tq) for r in range(NSA_GROUP)]
        m_prev = m_ref[...]
        l_prev = l_ref[...]
        ss = [jnp.where(mask, _dot_nt(q4[sl], k), NEG) for sl in sls]
        m_news = [jnp.maximum(m_prev[sl], jnp.max(s, axis=-1, keepdims=True)) for sl, s in zip(sls, ss)]
        ps = [jnp.exp(s - mn) for s, mn in zip(ss, m_news)]
        sums = [jnp.sum(p, axis=-1, keepdims=True) for p in ps]
        m_new = jnp.concatenate(m_news, axis=0)
        alpha = jnp.exp(m_prev - m_new)
        l_ref[...] = alpha * l_prev + jnp.concatenate(sums, axis=0)
        m_ref[...] = m_new
        pv = _dot(jnp.concatenate([p.astype(BF16) for p in ps], axis=0), vs_ref[pl.ds(start, tk), :])
        acc_ref[...] = alpha * acc_ref[...] + pv
        return carry

    _flash_init(m_ref, l_ref, acc_ref)
    lax.fori_loop(0, ((i + 1) * tq + tk - 1) // tk, slc_step, 0)
    o_slc = acc_ref[...] / l_ref[...]

    wstart = pl.multiple_of(jnp.clip((i + 1) * tq - wlen, 0, t - wlen), tq)
    s = _dot_nt(q4, kw_ref[pl.ds(wstart, wlen), :])
    wdiff = (i * tq + lax.broadcasted_iota(jnp.int32, (tq, wlen), 0)
             - (wstart + lax.broadcasted_iota(jnp.int32, (tq, wlen), 1)))
    s = jnp.where(tile4((wdiff >= 0) & (wdiff < WINDOW)), s, NEG)
    p = jnp.exp(s - jnp.max(s, axis=-1, keepdims=True))
    o_win = _dot(p.astype(BF16), vw_ref[pl.ds(wstart, wlen), :]) / jnp.sum(p, axis=-1, keepdims=True)

    gates = gate_ref[0]
    for r in range(NSA_GROUP):
        sl = slice(r * tq, (r + 1) * tq)
        col = slice(r * HEAD_DIM, (r + 1) * HEAD_DIM)
        o = (gates[:, 3 * r:3 * r + 1] * ocmp_ref[:, col]
             + gates[:, 3 * r + 1:3 * r + 2] * o_slc[sl]
             + gates[:, 3 * r + 2:3 * r + 3] * o_win[sl])
        o_ref[:, col] = o.astype(o_ref.dtype)


def _prompt_attn(qn, slc_bf, win_bf, ocmp, sel, gates, *, bsz, t, tq):
    m = qn.shape[0]
    nt = t // tq
    gw = NSA_GROUP * HEAD_DIM
    half = NSA_KV_HEADS
    nsp = sel.shape[2]
    n_slc = -(-t // SLC_BLOCK)
    expand = np.zeros((nsp, t), np.float32)
    expand[np.arange(t) // SLC_BLOCK, np.arange(t)] = 1.0
    assert n_slc <= nsp
    tk = next(c for c in (512, 256, 128) if t % c == 0 and c >= tq)
    wlen = min(t, WINDOW + tq)
    assert wlen % tq == 0 and t % tq == 0
    kv_spec = lambda which: pl.BlockSpec((t, HEAD_DIM), lambda b, g, i: (b, which * half + g))
    return pl.pallas_call(
        functools.partial(_pattn_kernel, tq=tq, tk=tk, wlen=wlen),
        grid=(bsz, NSA_KV_HEADS, nt),
        in_specs=[pl.BlockSpec((tq, gw), lambda b, g, i: (b * nt + i, g)),
                  kv_spec(0), kv_spec(1), kv_spec(0), kv_spec(1),
                  pl.BlockSpec((tq, gw), lambda b, g, i: (b * nt + i, g)),
                  pl.BlockSpec((1, tq, nsp), lambda b, g, i: (g, b * nt + i, 0)),
                  pl.BlockSpec((1, tq, LANES), lambda b, g, i: (g, b * nt + i, 0)),
                  pl.BlockSpec((nsp, t), lambda b, g, i: (0, 0))],
        out_specs=pl.BlockSpec((tq, gw), lambda b, g, i: (b * nt + i, g)),
        out_shape=jax.ShapeDtypeStruct((m, NSA_WIDTH), BF16),
        scratch_shapes=[pltpu.VMEM((tq, t), F32),
                        pltpu.VMEM((NSA_GROUP * tq, 1), F32),
                        pltpu.VMEM((NSA_GROUP * tq, 1), F32),
                        pltpu.VMEM((NSA_GROUP * tq, HEAD_DIM), F32)],
        compiler_params=_cparams(("parallel", "parallel", "arbitrary")),
        name="nsa_prompt_attn",
    )(qn, slc_bf, slc_bf, win_bf, win_bf, ocmp, sel, gates, jnp.asarray(expand, BF16))


def _sattn_kernel(tbl_ref, q_ref, *refs, tq, n_steps, pg, past):
    del tbl_ref
    page_refs = refs[:pg]
    (snew_ref, wcache_ref, wnew_ref, ocmp_ref, sel_ref, gate_ref, exp_ref, o_ref, s_buf, v_buf) = refs[pg:]
    j = pl.program_id(1)
    rows = NSA_GROUP * tq
    half = NSA_KV_HEADS * HEAD_DIM
    nsp = sel_ref.shape[2]
    n_prev = wcache_ref.shape[0] // KV_SLABS

    def tile4(x):
        return jnp.concatenate([x] * NSA_GROUP, axis=0)

    q4_bf = [_stack_heads(q_ref, g, tq, BF16) for g in range(NSA_KV_HEADS)]
    for u in range(pg):
        key0 = pl.multiple_of((j * pg + u) * PAGE_SIZE, PAGE_SIZE)
        for g in range(NSA_KV_HEADS):
            k = page_refs[u][pl.ds(g, PAGE_SIZE, stride=KV_SLABS), :].astype(BF16)
            s_buf[g * rows:(g + 1) * rows, pl.ds(key0, PAGE_SIZE)] = _dot_nt(q4_bf[g], k)
            v_buf[g, pl.ds(key0, PAGE_SIZE), :] = (
                page_refs[u][pl.ds(NSA_KV_HEADS + g, PAGE_SIZE, stride=KV_SLABS), :].astype(BF16))

    @pl.when(j == n_steps - 1)
    def _():
        blk_lane = lax.broadcasted_iota(jnp.int32, (tq, nsp), 1)
        trow = lax.broadcasted_iota(jnp.int32, (tq, tq), 0)
        tcol = lax.broadcasted_iota(jnp.int32, (tq, tq), 1)
        causal = tcol <= trow
        wrow = lax.broadcasted_iota(jnp.int32, (tq, n_prev), 0)
        wcol = lax.broadcasted_iota(jnp.int32, (tq, n_prev), 1)
        wdiff = wrow + n_prev - wcol
        wmask = (wdiff >= 0) & (wdiff < WINDOW)
        ndiff = trow - tcol
        nmask = (ndiff >= 0) & (ndiff < WINDOW)
        gates_all = gate_ref[...]
        for g in range(NSA_KV_HEADS):
            q4 = _stack_heads(q_ref, g, tq, F32)
            selg = sel_ref[g]
            sexp = _dot(selg.astype(BF16), exp_ref[...])
            sc = jnp.where(tile4(sexp > 0.5), s_buf[g * rows:(g + 1) * rows, :] * SCALE, NEG)
            kn = snew_ref[:, g * HEAD_DIM:(g + 1) * HEAD_DIM]
            vn = snew_ref[:, half + g * HEAD_DIM:half + (g + 1) * HEAD_DIM]
            flag = jnp.zeros((tq, tq), jnp.bool_)
            for u in range(tq):
                col = jnp.sum(jnp.where(blk_lane == (past + u) // SLC_BLOCK, selg, 0.0), axis=1, keepdims=True) > 0.5
                flag = flag | (col & (tcol == u))
            sn = jnp.where(tile4(flag & causal), _dot_nt(q4, kn, HI) * SCALE, NEG)
            mx = jnp.maximum(jnp.max(sc, axis=-1, keepdims=True), jnp.max(sn, axis=-1, keepdims=True))
            pc = jnp.exp(sc - mx)
            pn = jnp.exp(sn - mx)
            den = jnp.sum(pc, axis=-1, keepdims=True) + jnp.sum(pn, axis=-1, keepdims=True)
            o_slc = (_dot(pc.astype(BF16), v_buf[g]) + _dot(pn, vn, HI)) / den

            kc = wcache_ref[pl.ds(g, n_prev, stride=KV_SLABS), :].astype(BF16)
            vc = wcache_ref[pl.ds(NSA_KV_HEADS + g, n_prev, stride=KV_SLABS), :].astype(BF16)
            kw = wnew_ref[:, g * HEAD_DIM:(g + 1) * HEAD_DIM]
            vw = wnew_ref[:, half + g * HEAD_DIM:half + (g + 1) * HEAD_DIM]
            sc = jnp.where(tile4(wmask), _dot_nt(q4.astype(BF16), kc) * SCALE, NEG)
            sn = jnp.where(tile4(nmask), _dot_nt(q4, kw, HI) * SCALE, NEG)
            mx = jnp.maximum(jnp.max(sc, axis=-1, keepdims=True), jnp.max(sn, axis=-1, keepdims=True))
            pc = jnp.exp(sc - mx)
            pn = jnp.exp(sn - mx)
            den = jnp.sum(pc, axis=-1, keepdims=True) + jnp.sum(pn, axis=-1, keepdims=True)
            o_win = (_dot(pc.astype(BF16), vc) + _dot(pn, vw, HI)) / den

            gates = gates_all[g]
            for r in range(NSA_GROUP):
                sl = slice(r * tq, (r + 1) * tq)
                col = slice((g * NSA_GROUP + r) * HEAD_DIM, (g * NSA_GROUP + r + 1) * HEAD_DIM)
                o = (gates[:, 3 * r:3 * r + 1] * ocmp_ref[:, col]
                     + gates[:, 3 * r + 1:3 * r + 2] * o_slc[sl]
                     + gates[:, 3 * r + 2:3 * r + 3] * o_win[sl])
                o_ref[:, col] = o.astype(o_ref.dtype)


def _sample_attn(qn, cache2d, page_table, slc_new, win_cache2d, win_row0, win_new, ocmp, sel, gates,
                 *, bsz, tq, past, n_prev):
    n_pages = page_table.shape[1]
    pg = math.gcd(n_pages, 4)
    n_steps = n_pages // pg
    nsp = sel.shape[2]
    rows = NSA_GROUP * tq
    page_rows = PAGE_SIZE * KV_SLABS
    expand = np.zeros((nsp, past), np.float32)
    expand[np.arange(past) // SLC_BLOCK, np.arange(past)] = 1.0

    def page_spec(u):
        return pl.BlockSpec((page_rows, HEAD_DIM), lambda b, j, tbl: (tbl[b, j * pg + u], 0))

    grid_spec = pltpu.PrefetchScalarGridSpec(
        num_scalar_prefetch=1,
        grid=(bsz, n_steps),
        in_specs=[pl.BlockSpec((tq, NSA_WIDTH), lambda b, j, tbl: (b, 0))]
        + [page_spec(u) for u in range(pg)]
        + [pl.BlockSpec((tq, KV_WIDTH), lambda b, j, tbl: (b, 0)),
           pl.BlockSpec((n_prev * KV_SLABS, HEAD_DIM), lambda b, j, tbl: (win_row0 + b, 0)),
           pl.BlockSpec((tq, KV_WIDTH), lambda b, j, tbl: (b, 0)),
           pl.BlockSpec((tq, NSA_WIDTH), lambda b, j, tbl: (b, 0)),
           pl.BlockSpec((NSA_KV_HEADS, tq, nsp), lambda b, j, tbl: (0, b, 0)),
           pl.BlockSpec((NSA_KV_HEADS, tq, LANES), lambda b, j, tbl: (0, b, 0)),
           pl.BlockSpec((nsp, past), lambda b, j, tbl: (0, 0))],
        out_specs=pl.BlockSpec((tq, NSA_WIDTH), lambda b, j, tbl: (b, 0)),
        scratch_shapes=[pltpu.VMEM((NSA_KV_HEADS * rows, past), F32),
                        pltpu.VMEM((NSA_KV_HEADS, past, HEAD_DIM), BF16)],
    )
    return pl.pallas_call(
        functools.partial(_sattn_kernel, tq=tq, n_steps=n_steps, pg=pg, past=past),
        grid_spec=grid_spec,
        out_shape=jax.ShapeDtypeStruct((bsz * tq, NSA_WIDTH), BF16),
        compiler_params=_cparams(("parallel", "arbitrary")),
        name="nsa_sample_attn",
    )(page_table, qn, *([cache2d] * pg), slc_new, win_cache2d, win_new, ocmp, sel, gates,
      jnp.asarray(expand, BF16))


def _split_w_in(w_in):
    k = w_in.shape[0]
    pad = jnp.zeros((k, LANES - 2 * GDN_HEADS - 3 * NSA_HEADS), w_in.dtype)
    w_small = jnp.concatenate([w_in[:, W_IN_B0:W_IN_N0], w_in[:, W_IN_G0:], pad], axis=1)
    return w_in, w_in[:, W_IN_N0:W_IN_G0], w_small


def _in_proj(x2d, p, *, tm):
    xn = _rms_cast(x2d, p["attn_norm_g"], tm=min(tm, 512))
    w_full, w_b, w_small = p["w_in"]
    proj_a = _mm_nw([xn], w_full, n=PROJ_A_WIDTH, tm=tm, tn=512)
    proj_b = _mm_nw([xn], w_b, n=PROJ_B_WIDTH, tm=tm, tn=512)
    small = _mm_nw([xn], w_small, n=LANES, tm=tm, tn=LANES)
    return proj_a, proj_b, small


def _pos_weights(cmp_pos_w):
    w = cmp_pos_w.reshape(2, CMP_BLOCK // CMP_STRIDE, CMP_STRIDE, NSA_KV_HEADS)
    w = jnp.transpose(w, (1, 2, 0, 3))
    return jnp.broadcast_to(w[..., None], w.shape + (HEAD_DIM,)).reshape(2, CMP_STRIDE, KV_SLABS, HEAD_DIM)


def _mixers_out(x2d, o_gdn, o_nsa, w_o, mlp_norm_g, w_up, w_down, *, tm):
    x1 = _mm_nw([o_gdn, o_nsa], w_o, n=w_o.shape[1], tm=tm, tn=512, res=x2d)
    xn = _rms_cast(x1, mlp_norm_g, tm=min(tm, 512))
    hid = _mm_nw([xn], w_up, n=w_up.shape[1], tm=tm, tn=512, relu2=True, out_dtype=BF16)
    return _mm_res(hid, w_down, x1, tm=tm, tn=min(1024, w_down.shape[1]), tk=min(2048, w_down.shape[0]))


def _round_up(x, n):
    return -(-x // n) * n


def _prompt_layer(x, p):
    bsz, t, d = x.shape
    m = bsz * t
    x2d = x.reshape(m, d)
    proj_a, proj_b, small = _in_proj(x2d, p, tm=1024)
    conv0 = jnp.zeros((bsz, CONV_W - 1, 3 * GDN_WIDTH), F32)
    s_zero = jnp.zeros((bsz, GDN_HEADS, HEAD_DIM, HEAD_DIM), F32)
    o_gdn, s_new = _gdn(proj_a, small, conv0, s_zero, p["gdn_conv_w"], p["gdn_a_log"], p["gdn_dt_bias"],
                        p["gdn_norm_g"], bsz=bsz, rows_per_seq=t, t_valid=t)
    qn, slc_n, win_n, gates, slc_bf, win_bf = _nsa_prep(proj_b, small, p["q_norm_g"], p["k_norm_g"], tm=512)

    n_cmp = (t - CMP_BLOCK) // CMP_STRIDE + 1
    n_slc = -(-t // SLC_BLOCK)
    n_pages = t // PAGE_SIZE
    table = jnp.arange(bsz * n_pages, dtype=jnp.int32).reshape(bsz, n_pages)
    kv_tail = (2, NSA_KV_HEADS, HEAD_DIM)
    cmp_new = proj_b[:, COL_CMP:COL_CMP + KV_WIDTH].reshape((bsz, t) + kv_tail)
    s0, s1 = _pool(cmp_new.reshape(m * KV_SLABS, HEAD_DIM), table, p["cmp_w"])
    kc, vc = _cmp_finish(s0, s1, p["cmp_phi"], p["k_norm_g"][0], n_cmp=n_cmp)
    nb = kc.shape[2]
    nsp = _round_up(n_slc, LANES)
    tq = 128
    ocmp, sel = _cmp_topk(qn, kc, vc, _overlap_matrix(nb, nsp, n_cmp, n_slc), bsz=bsz, rows_per_seq=t, tq=tq,
                          n_cmp=n_cmp, n_slc=n_slc, offset=0)
    o_nsa = _prompt_attn(qn, slc_bf, win_bf, ocmp, sel, gates, bsz=bsz, t=t, tq=tq)
    y = _mixers_out(x2d, o_gdn, o_nsa, p["w_o"], p["mlp_norm_g"], p["w_up"], p["w_down"], tm=1024)

    keep = min(WINDOW, t)
    return (y.reshape(bsz, t, d),
            cmp_new,
            slc_n.reshape((bsz, t) + kv_tail),
            win_n.reshape((bsz, t) + kv_tail)[:, t - keep:],
            proj_a.reshape(bsz, t, PROJ_A_WIDTH)[:, t - (CONV_W - 1):, 0:3 * GDN_WIDTH],
            s_new)


def _sample_layer(x, layer, cache_cmp_kv, cache_slc_kv, cache_win_kv, conv_buf, s0_state, page_table, p):
    bsz, t, d = x.shape
    tp = _round_up(t, SUBLANES)
    n_pages = page_table.shape[1]
    past = n_pages * PAGE_SIZE
    assert t <= SLC_BLOCK and past % SLC_BLOCK == 0 and t >= CONV_W - 1
    x2d = jnp.pad(x, ((0, 0), (0, tp - t), (0, 0))).reshape(bsz * tp, d)
    proj_a, proj_b, small = _in_proj(x2d, p, tm=bsz * tp)
    o_gdn, s_new = _gdn(proj_a, small, conv_buf, s0_state, p["gdn_conv_w"], p["gdn_a_log"], p["gdn_dt_bias"],
                        p["gdn_norm_g"], bsz=bsz, rows_per_seq=tp, t_valid=t)
    qn, slc_n, win_n, gates, _, _ = _nsa_prep(proj_b, small, p["q_norm_g"], p["k_norm_g"], tm=bsz * tp)

    total = past + t
    n_cmp = (total - CMP_BLOCK) // CMP_STRIDE + 1
    n_slc = -(-total // SLC_BLOCK)
    assert (n_cmp - 1) * CMP_STRIDE + CMP_BLOCK <= past
    depth, n_phys = cache_cmp_kv.shape[:2]
    table = page_table + layer * n_phys
    cache_rows = depth * n_phys * PAGE_SIZE * KV_SLABS
    s0, s1 = _pool(cache_cmp_kv.reshape(cache_rows, HEAD_DIM), table, p["cmp_w"])
    kc, vc = _cmp_finish(s0, s1, p["cmp_phi"], p["k_norm_g"][0], n_cmp=n_cmp)
    nb = kc.shape[2]
    nsp = _round_up(n_slc, LANES)
    ocmp, sel = _cmp_topk(qn, kc, vc, _overlap_matrix(nb, nsp, n_cmp, n_slc), bsz=bsz, rows_per_seq=tp, tq=tp,
                          n_cmp=n_cmp, n_slc=n_slc, offset=past)
    n_prev = cache_win_kv.shape[2]
    o_nsa = _sample_attn(qn, cache_slc_kv.reshape(cache_rows, HEAD_DIM), table, slc_n,
                         cache_win_kv.reshape(depth * bsz * n_prev * KV_SLABS, HEAD_DIM), layer * bsz,
                         win_n, ocmp, sel, gates, bsz=bsz, tq=tp, past=past, n_prev=n_prev)
    y = _mixers_out(x2d, o_gdn, o_nsa, p["w_o"], p["mlp_norm_g"], p["w_up"], p["w_down"], tm=bsz * tp)

    kv_tail = (2, NSA_KV_HEADS, HEAD_DIM)
    win_all = jnp.concatenate([cache_win_kv[layer], win_n.reshape((bsz, tp) + kv_tail)[:, :t]], axis=1)
    keep = min(WINDOW, n_prev + t)
    return (y.reshape(bsz, tp, d)[:, :t],
            proj_b.reshape(bsz, tp, PROJ_B_WIDTH)[:, :t, COL_CMP:COL_CMP + KV_WIDTH].reshape((bsz, t) + kv_tail),
            slc_n.reshape((bsz, tp) + kv_tail)[:, :t],
            win_all[:, n_prev + t - keep:],
            proj_a.reshape(bsz, tp, PROJ_A_WIDTH)[:, t - (CONV_W - 1):t, 0:3 * GDN_WIDTH],
            s_new)


def kernel(x_prompt, x_sample, cache_cmp_kv, cache_slc_kv, cache_win_kv, cache_gdn_conv, state_gdn, page_table, attn_norm_g, w_in, gdn_conv_w, gdn_a_log, gdn_dt_bias, gdn_norm_g, q_norm_g, k_norm_g, cmp_pos_w, cmp_phi, w_o, mlp_norm_g, w_up, w_down):
    depth = w_in.shape[0]
    yp, ys = x_prompt, x_sample
    per_layer = []
    for layer in range(depth):
        p = {
            "attn_norm_g": attn_norm_g[layer], "w_in": _split_w_in(w_in[layer]),
            "gdn_conv_w": gdn_conv_w[layer], "gdn_a_log": gdn_a_log[layer], "gdn_dt_bias": gdn_dt_bias[layer],
            "gdn_norm_g": gdn_norm_g[layer], "q_norm_g": q_norm_g[layer], "k_norm_g": k_norm_g[layer],
            "cmp_w": _pos_weights(cmp_pos_w[layer]), "cmp_phi": cmp_phi[layer],
            "w_o": w_o[layer], "mlp_norm_g": mlp_norm_g[layer],
            "w_up": w_up[layer], "w_down": w_down[layer].astype(BF16),
        }
        yp, cmp_p, slc_p, win_p, conv_p, s_p = _prompt_layer(yp, p)
        ys, cmp_s, slc_s, win_s, conv_s, s_s = _sample_layer(
            ys, layer, cache_cmp_kv, cache_slc_kv, cache_win_kv, cache_gdn_conv[layer],
            state_gdn[layer], page_table, p)
        per_layer.append((cmp_p, cmp_s, slc_p, slc_s, win_p, win_s, conv_p, conv_s, s_p, s_s))
    st = [jnp.stack(z, axis=0) for z in zip(*per_layer)]
    return (yp, ys) + tuple(st)
```

```python
import functools
import math

import numpy as np
import jax
import jax.numpy as jnp
from jax import lax
from jax.experimental import pallas as pl
from jax.experimental.pallas import tpu as pltpu

F32 = jnp.float32
BF16 = jnp.bfloat16
HI = lax.Precision.HIGHEST

HEAD_DIM = 128
GDN_HEADS = 16
NSA_HEADS = 16
NSA_KV_HEADS = 4
NSA_GROUP = NSA_HEADS // NSA_KV_HEADS
GDN_WIDTH = GDN_HEADS * HEAD_DIM
NSA_WIDTH = NSA_HEADS * HEAD_DIM
KV_WIDTH = 2 * NSA_KV_HEADS * HEAD_DIM
KV_SLABS = 2 * NSA_KV_HEADS
CONV_W = 4
GDN_CHUNK = 64
GDN_HEAD_GROUP = 8
CMP_BLOCK = 32
CMP_STRIDE = 16
SLC_BLOCK = 64
SLC_TOPK = 16
SLC_LOCAL = 2
WINDOW = 512
PAGE_SIZE = 128
EPS = 1e-6
FORCE_SCORE = 1e9
NEG = -1e30
SCALE = HEAD_DIM ** -0.5

LANES = 128
SUBLANES = 8
VMEM_LIMIT = 52 * 1024 * 1024

COL_Q, COL_K, COL_V, COL_Z = 0, GDN_WIDTH, 2 * GDN_WIDTH, 3 * GDN_WIDTH
PROJ_A_WIDTH = 4 * GDN_WIDTH
COL_NQ = 0
COL_CMP = COL_NQ + NSA_WIDTH
COL_SLC = COL_CMP + KV_WIDTH
COL_WIN = COL_SLC + KV_WIDTH
PROJ_B_WIDTH = COL_WIN + KV_WIDTH
SMALL_B, SMALL_A, SMALL_GATE = 0, GDN_HEADS, 2 * GDN_HEADS
W_IN_B0 = PROJ_A_WIDTH
W_IN_N0 = W_IN_B0 + 2 * GDN_HEADS
W_IN_G0 = W_IN_N0 + PROJ_B_WIDTH


def _cparams(sem):
    return pltpu.CompilerParams(dimension_semantics=sem, vmem_limit_bytes=VMEM_LIMIT)


def _dot(a, b, prec=None):
    return jnp.dot(a, b, preferred_element_type=F32, precision=prec)


def _dot_nt(a, b, prec=None):
    return lax.dot_general(a, b, (((1,), (1,)), ((), ())), preferred_element_type=F32, precision=prec)


def _dot_tn(a, b, prec=None):
    return lax.dot_general(a, b, (((0,), (0,)), ((), ())), preferred_element_type=F32, precision=prec)


def _rms(x, g):
    return x * lax.rsqrt(jnp.mean(x * x, axis=-1, keepdims=True) + EPS) * g


def _silu(x):
    return x * jax.nn.sigmoid(x)


def _softplus(x):
    return jnp.maximum(x, 0.0) + jnp.log1p(jnp.exp(-jnp.abs(x)))


def _rms_cast_kernel(x_ref, g_ref, o_ref):
    o_ref[...] = _rms(x_ref[...], g_ref[...]).astype(o_ref.dtype)


def _rms_cast(x, g, *, tm):
    m, k = x.shape
    assert m % tm == 0
    return pl.pallas_call(
        _rms_cast_kernel,
        grid=(m // tm,),
        in_specs=[pl.BlockSpec((tm, k), lambda i: (i, 0)), pl.BlockSpec((1, k), lambda i: (0, 0))],
        out_specs=pl.BlockSpec((tm, k), lambda i: (i, 0)),
        out_shape=jax.ShapeDtypeStruct((m, k), BF16),
        compiler_params=_cparams(("parallel",)),
        name="rms_cast",
    )(x, g.reshape(1, k))


def _mm_nw_kernel(*refs, n_a, relu2, has_res, w_t):
    a_refs = refs[:n_a]
    w_ref = refs[n_a]
    r_ref = refs[n_a + 1] if has_res else None
    o_ref, wbf_ref = refs[-2:]

    @pl.when(pl.program_id(1) == 0)
    def _():
        w = w_ref[...]
        wbf_ref[...] = (w.T if w_t else w).astype(BF16)

    y, k0 = None, 0
    for a_ref in a_refs:
        part = _dot(a_ref[...], wbf_ref[k0:k0 + a_ref.shape[1], :])
        y = part if y is None else y + part
        k0 += a_ref.shape[1]
    if relu2:
        y = jnp.square(jnp.maximum(y, 0.0))
    if has_res:
        y = y + r_ref[...]
    o_ref[...] = y.astype(o_ref.dtype)


def _mm_nw(acts, w, *, n, tm, tn, res=None, relu2=False, out_dtype=F32, w_t=False):
    m = acts[0].shape[0]
    k = sum(a.shape[1] for a in acts)
    assert m % tm == 0 and n % tn == 0 and w.shape[1 if w_t else 0] == k
    in_specs = [pl.BlockSpec((tm, a.shape[1]), lambda j, i: (i, 0)) for a in acts]
    in_specs.append(pl.BlockSpec((tn, k), lambda j, i: (j, 0)) if w_t else pl.BlockSpec((k, tn), lambda j, i: (0, j)))
    args = list(acts) + [w]
    if res is not None:
        in_specs.append(pl.BlockSpec((tm, tn), lambda j, i: (i, j)))
        args.append(res)
    return pl.pallas_call(
        functools.partial(_mm_nw_kernel, n_a=len(acts), relu2=relu2, has_res=res is not None, w_t=w_t),
        grid=(n // tn, m // tm),
        in_specs=in_specs,
        out_specs=pl.BlockSpec((tm, tn), lambda j, i: (i, j)),
        out_shape=jax.ShapeDtypeStruct((m, n), out_dtype),
        scratch_shapes=[pltpu.VMEM((k, tn), BF16)],
        compiler_params=_cparams(("parallel", "arbitrary")),
        name="matmul_wcast",
    )(*args)


def _mm_res_kernel(a_ref, w_ref, r_ref, o_ref, acc_ref, *, nk):
    kk = pl.program_id(2)

    @pl.when(kk == 0)
    def _():
        acc_ref[...] = jnp.zeros_like(acc_ref)

    acc_ref[...] += _dot(a_ref[...], w_ref[...])

    @pl.when(kk == nk - 1)
    def _():
        o_ref[...] = r_ref[...] + acc_ref[...]


def _mm_res(a, w, res, *, tm, tn, tk):
    m, k = a.shape
    n = w.shape[1]
    assert m % tm == 0 and n % tn == 0 and k % tk == 0
    nk = k // tk
    return pl.pallas_call(
        functools.partial(_mm_res_kernel, nk=nk),
        grid=(m // tm, n // tn, nk),
        in_specs=[pl.BlockSpec((tm, tk), lambda i, j, kk: (i, kk)),
                  pl.BlockSpec((tk, tn), lambda i, j, kk: (kk, j)),
                  pl.BlockSpec((tm, tn), lambda i, j, kk: (i, j))],
        out_specs=pl.BlockSpec((tm, tn), lambda i, j, kk: (i, j)),
        out_shape=jax.ShapeDtypeStruct((m, n), F32),
        scratch_shapes=[pltpu.VMEM((tm, tn), F32)],
        compiler_params=_cparams(("parallel", "parallel", "arbitrary")),
        name="matmul_residual",
    )(a, w, res)


def _gdn_kernel(q_ref, k_ref, v_ref, z_ref, sm_ref, cbuf_ref, s0_ref, cw_ref, alog_ref, dt_ref, ng_ref,
                o_ref, sout_ref, xp_ref, st_ref, *, rows, nc, t_valid):
    C = GDN_CHUNK
    W = GDN_WIDTH
    c = pl.program_id(1)

    @pl.when(c == 0)
    def _():
        xp_ref[...] = jnp.zeros(xp_ref.shape, F32)
        xp_ref[SUBLANES - (CONV_W - 1):SUBLANES, :] = cbuf_ref[0]
        st_ref[...] = s0_ref[0]

    @pl.when(c > 0)
    def _():
        xp_ref[0:SUBLANES, :] = xp_ref[C:C + SUBLANES, :]

    xp_ref[SUBLANES:SUBLANES + rows, 0:W] = q_ref[...]
    xp_ref[SUBLANES:SUBLANES + rows, W:2 * W] = k_ref[...]
    xp_ref[SUBLANES:SUBLANES + rows, 2 * W:3 * W] = v_ref[...]

    row = lax.broadcasted_iota(jnp.int32, (C, LANES), 0)
    lane = lax.broadcasted_iota(jnp.int32, (C, LANES), 1)
    valid = (c * C + row) < t_valid
    if rows < C:
        sm = jnp.concatenate([sm_ref[...], jnp.zeros((C - rows, LANES), F32)], axis=0)
    else:
        sm = sm_ref[...]
    beta_all = jnp.where(valid, jax.nn.sigmoid(sm), 0.0)
    g_all = jnp.where(valid, -jnp.exp(alog_ref[...]) * _softplus(sm + dt_ref[...]), 0.0)
    ri = lax.broadcasted_iota(jnp.int32, (C, C), 0)
    ci = lax.broadcasted_iota(jnp.int32, (C, C), 1)
    tri = ri >= ci
    strict = ri > ci
    gcum = _dot(tri.astype(F32), g_all, HI)
    li = lax.broadcasted_iota(jnp.int32, (LANES, LANES), 0)
    lj = lax.broadcasted_iota(jnp.int32, (LANES, LANES), 1)
    gcum_t = _dot_nt((li == lj).astype(F32), gcum, HI)
    valid_col = valid[:, 0:1]
    eye = (ri == ci).astype(F32)
    n_sq = int(math.log2(C)) - 1

    def mm(a, b):
        return _dot(a.astype(BF16), b.astype(BF16))

    def mm_nt(a, b):
        return _dot_nt(a.astype(BF16), b.astype(BF16))

    def conv(col):
        acc = None
        for j in range(CONV_W):
            term = (xp_ref[pl.ds(SUBLANES - (CONV_W - 1) + j, C), col:col + LANES]
                    * cw_ref[j:j + 1, col:col + LANES])
            acc = term if acc is None else acc + term
        return _silu(acc)

    def head_group(hs):
        n = len(hs)
        qs = [conv(h * HEAD_DIM) for h in hs]
        ks = [conv(W + h * HEAD_DIM) for h in hs]
        vs = [conv(2 * W + h * HEAD_DIM) for h in hs]
        qs = [q * lax.rsqrt(jnp.sum(q * q, axis=-1, keepdims=True) + EPS) * SCALE for q in qs]
        ks = [jnp.where(valid_col, k * lax.rsqrt(jnp.sum(k * k, axis=-1, keepdims=True) + EPS), 0.0) for k in ks]
        gcs = [gcum[:, SMALL_A + h:SMALL_A + h + 1] for h in hs]
        bhs = [beta_all[:, SMALL_B + h:SMALL_B + h + 1] for h in hs]
        grs = [gcum_t[SMALL_A + h:SMALL_A + h + 1, :] for h in hs]
        decays = [jnp.where(tri, jnp.exp(jnp.where(tri, gc - gr, 0.0)), 0.0) for gc, gr in zip(gcs, grs)]
        egs = [jnp.exp(gc) for gc in gcs]

        kbs = [k * b for k, b in zip(ks, bhs)]
        kqs = [mm_nt(jnp.concatenate([kb, q], axis=0), k) for kb, q, k in zip(kbs, qs, ks)]
        nmats = [jnp.where(strict, -(kq[0:C] * d), 0.0) for kq, d in zip(kqs, decays)]
        aqks = [jnp.where(tri, kq[C:2 * C] * d, 0.0) for kq, d in zip(kqs, decays)]
        pinvs = [eye + nm for nm in nmats]
        npows = nmats
        for _ in range(n_sq):
            npows = [mm(np_, np_) for np_ in npows]
            pinvs = [p + mm(p, np_) for p, np_ in zip(pinvs, npows)]
        uws = [mm(p, jnp.concatenate([v * b, kb * eg], axis=1))
               for p, v, b, kb, eg in zip(pinvs, vs, bhs, kbs, egs)]

        ss = [st_ref[h] for h in hs]
        wss = [mm(jnp.concatenate([uw[:, HEAD_DIM:2 * HEAD_DIM], q * eg], axis=0), s)
               for uw, q, eg, s in zip(uws, qs, egs, ss)]
        v_news = [uw[:, 0:HEAD_DIM] - ws[0:C] for uw, ws in zip(uws, wss)]
        os_ = [ws[C:2 * C] + mm(aqk, vn) for ws, aqk, vn in zip(wss, aqks, v_news)]
        for i in range(n):
            g_last = gcs[i][C - 1:C, :]
            k_dec = ks[i] * jnp.exp(g_last - gcs[i])
            st_ref[hs[i]] = ss[i] * jnp.exp(g_last) + _dot_tn(k_dec.astype(BF16), v_news[i].astype(BF16))
        for i in range(n):
            col = hs[i] * HEAD_DIM
            o = _rms(os_[i], ng_ref[...])
            o_ref[:, col:col + HEAD_DIM] = (o[0:rows] * _silu(z_ref[:, col:col + HEAD_DIM])).astype(o_ref.dtype)

    for h0 in range(0, GDN_HEADS, GDN_HEAD_GROUP):
        head_group(list(range(h0, h0 + GDN_HEAD_GROUP)))

    @pl.when(c == nc - 1)
    def _():
        sout_ref[0] = st_ref[...]


def _gdn(proj, small, conv_buf, s0, conv_w, a_log, dt_bias, norm_g, *, bsz, rows_per_seq, t_valid):
    C = GDN_CHUNK
    if rows_per_seq >= C:
        assert rows_per_seq % C == 0
        rows, nc = C, rows_per_seq // C
    else:
        assert rows_per_seq % SUBLANES == 0
        rows, nc = rows_per_seq, 1
    wblk = GDN_WIDTH
    zeros = jnp.zeros((LANES - 2 * GDN_HEADS,), F32)
    alog_row = jnp.concatenate([jnp.zeros((GDN_HEADS,), F32), a_log, zeros]).reshape(1, LANES)
    dt_row = jnp.concatenate([jnp.zeros((GDN_HEADS,), F32), dt_bias, zeros]).reshape(1, LANES)
    row_map = lambda b, c: b * nc + c
    in_specs = [
        pl.BlockSpec((rows, wblk), lambda b, c: (row_map(b, c), COL_Q // wblk)),
        pl.BlockSpec((rows, wblk), lambda b, c: (row_map(b, c), COL_K // wblk)),
        pl.BlockSpec((rows, wblk), lambda b, c: (row_map(b, c), COL_V // wblk)),
        pl.BlockSpec((rows, wblk), lambda b, c: (row_map(b, c), COL_Z // wblk)),
        pl.BlockSpec((rows, LANES), lambda b, c: (row_map(b, c), 0)),
        pl.BlockSpec((1, CONV_W - 1, 3 * GDN_WIDTH), lambda b, c: (b, 0, 0)),
        pl.BlockSpec((1, GDN_HEADS, HEAD_DIM, HEAD_DIM), lambda b, c: (b, 0, 0, 0)),
        pl.BlockSpec((CONV_W, 3 * GDN_WIDTH), lambda b, c: (0, 0)),
        pl.BlockSpec((1, LANES), lambda b, c: (0, 0)),
        pl.BlockSpec((1, LANES), lambda b, c: (0, 0)),
        pl.BlockSpec((1, HEAD_DIM), lambda b, c: (0, 0)),
    ]
    out_specs = [
        pl.BlockSpec((rows, wblk), lambda b, c: (row_map(b, c), 0)),
        pl.BlockSpec((1, GDN_HEADS, HEAD_DIM, HEAD_DIM), lambda b, c: (b, 0, 0, 0)),
    ]
    return pl.pallas_call(
        functools.partial(_gdn_kernel, rows=rows, nc=nc, t_valid=t_valid),
        grid=(bsz, nc),
        in_specs=in_specs,
        out_specs=out_specs,
        out_shape=[jax.ShapeDtypeStruct((bsz * rows_per_seq, GDN_WIDTH), BF16),
                   jax.ShapeDtypeStruct((bsz, GDN_HEADS, HEAD_DIM, HEAD_DIM), F32)],
        scratch_shapes=[pltpu.VMEM((C + SUBLANES, 3 * GDN_WIDTH), F32),
                        pltpu.VMEM((GDN_HEADS, HEAD_DIM, HEAD_DIM), F32)],
        compiler_params=_cparams(("parallel", "arbitrary")),
        name="gdn_chunk_scan",
    )(proj, proj, proj, proj, small, conv_buf, s0, conv_w, alog_row, dt_row, norm_g.reshape(1, HEAD_DIM))


def _prep_kernel(nq_ref, slc_ref, win_ref, sm_ref, qg_ref, kg_ref,
                 q_out, slc_out, win_out, gate_out, slc_bf_out, win_bf_out):
    qg = qg_ref[...]
    for h in range(NSA_HEADS):
        sl = slice(h * HEAD_DIM, (h + 1) * HEAD_DIM)
        q_out[:, sl] = _rms(nq_ref[:, sl], qg)
    half = NSA_KV_HEADS * HEAD_DIM
    for src, dst, dst_bf, gi in ((slc_ref, slc_out, slc_bf_out, 1), (win_ref, win_out, win_bf_out, 2)):
        kg = kg_ref[gi:gi + 1, :]
        for g in range(NSA_KV_HEADS):
            sl = slice(g * HEAD_DIM, (g + 1) * HEAD_DIM)
            kn = _rms(src[:, sl], kg)
            dst[:, sl] = kn
            dst_bf[:, sl] = kn.astype(BF16)
        v = src[:, half:2 * half]
        dst[:, half:2 * half] = v
        dst_bf[:, half:2 * half] = v.astype(BF16)
    sig = jax.nn.sigmoid(sm_ref[...])
    per = NSA_GROUP * 3
    for g in range(NSA_KV_HEADS):
        gate_out[g] = pltpu.roll(sig, LANES - (SMALL_GATE + per * g), axis=1)


def _nsa_prep(proj, small, q_norm_g, k_norm_g, *, tm):
    m = proj.shape[0]
    assert m % tm == 0
    return pl.pallas_call(
        _prep_kernel,
        grid=(m // tm,),
        in_specs=[pl.BlockSpec((tm, NSA_WIDTH), lambda i: (i, COL_NQ // NSA_WIDTH)),
                  pl.BlockSpec((tm, KV_WIDTH), lambda i: (i, COL_SLC // KV_WIDTH)),
                  pl.BlockSpec((tm, KV_WIDTH), lambda i: (i, COL_WIN // KV_WIDTH)),
                  pl.BlockSpec((tm, LANES), lambda i: (i, 0)),
                  pl.BlockSpec((1, HEAD_DIM), lambda i: (0, 0)),
                  pl.BlockSpec((3, HEAD_DIM), lambda i: (0, 0))],
        out_specs=[pl.BlockSpec((tm, NSA_WIDTH), lambda i: (i, 0)),
                   pl.BlockSpec((tm, KV_WIDTH), lambda i: (i, 0)),
                   pl.BlockSpec((tm, KV_WIDTH), lambda i: (i, 0)),
                   pl.BlockSpec((NSA_KV_HEADS, tm, LANES), lambda i: (0, i, 0)),
                   pl.BlockSpec((tm, KV_WIDTH), lambda i: (i, 0)),
                   pl.BlockSpec((tm, KV_WIDTH), lambda i: (i, 0))],
        out_shape=[jax.ShapeDtypeStruct((m, NSA_WIDTH), F32),
                   jax.ShapeDtypeStruct((m, KV_WIDTH), F32),
                   jax.ShapeDtypeStruct((m, KV_WIDTH), F32),
                   jax.ShapeDtypeStruct((NSA_KV_HEADS, m, LANES), F32),
                   jax.ShapeDtypeStruct((m, KV_WIDTH), BF16),
                   jax.ShapeDtypeStruct((m, KV_WIDTH), BF16)],
        compiler_params=_cparams(("parallel",)),
        name="nsa_prep",
    )(proj, proj, proj, small, q_norm_g.reshape(1, HEAD_DIM), k_norm_g)


def _pool_kernel(tbl_ref, *refs, pg):
    del tbl_ref
    page_refs = refs[:pg]
    w_ref, s0_ref, s1_ref = refs[pg:]
    sub = PAGE_SIZE // CMP_STRIDE
    out_rows = sub * KV_SLABS
    for u in range(pg):
        x = page_refs[u][...].reshape(sub, CMP_STRIDE, KV_SLABS, HEAD_DIM)
        s0_ref[0, u * out_rows:(u + 1) * out_rows, :] = jnp.sum(x * w_ref[0][None], axis=1).reshape(out_rows, HEAD_DIM)
        s1_ref[0, u * out_rows:(u + 1) * out_rows, :] = jnp.sum(x * w_ref[1][None], axis=1).reshape(out_rows, HEAD_DIM)


def _pool(rows2d, table, wexp):
    bsz, n_pages = table.shape
    pg = math.gcd(n_pages, 8)
    sub = PAGE_SIZE // CMP_STRIDE
    page_rows = PAGE_SIZE * KV_SLABS
    out_rows = pg * sub * KV_SLABS

    def page_spec(u):
        return pl.BlockSpec((page_rows, HEAD_DIM), lambda b, p, tbl: (tbl[b, p * pg + u], 0))

    grid_spec = pltpu.PrefetchScalarGridSpec(
        num_scalar_prefetch=1,
        grid=(bsz, n_pages // pg),
        in_specs=[page_spec(u) for u in range(pg)]
        + [pl.BlockSpec((2, CMP_STRIDE, KV_SLABS, HEAD_DIM), lambda b, p, tbl: (0, 0, 0, 0))],
        out_specs=[pl.BlockSpec((1, out_rows, HEAD_DIM), lambda b, p, tbl: (b, p, 0)),
                   pl.BlockSpec((1, out_rows, HEAD_DIM), lambda b, p, tbl: (b, p, 0))],
    )
    shape = jax.ShapeDtypeStruct((bsz, n_pages * sub * KV_SLABS, HEAD_DIM), F32)
    return pl.pallas_call(
        functools.partial(_pool_kernel, pg=pg), grid_spec=grid_spec, out_shape=[shape, shape],
        compiler_params=_cparams(("parallel", "arbitrary")), name="cmp_pool",
    )(table, *([rows2d] * pg), wexp)


def _cmp_fin_kernel(s0_ref, s1_ref, phi_ref, kg_ref, kc_ref, vc_ref, *, n_cmp):
    nb = s0_ref.shape[1] // KV_SLABS
    live = lax.broadcasted_iota(jnp.int32, (nb, HEAD_DIM), 0) < n_cmp
    zero_row = jnp.zeros((1, HEAD_DIM), F32)

    def pooled(slab):
        first = s0_ref[0, pl.ds(slab, nb, stride=KV_SLABS), :]
        second = s1_ref[0, pl.ds(KV_SLABS + slab, nb - 1, stride=KV_SLABS), :]
        return first + jnp.concatenate([second, zero_row], axis=0)

    for g in range(NSA_KV_HEADS):
        kc_ref[0, g] = jnp.where(live, _rms(_dot(pooled(g), phi_ref[0, g], HI), kg_ref[...]), 0.0)
        vc_ref[0, g] = jnp.where(live, _dot(pooled(NSA_KV_HEADS + g), phi_ref[1, g], HI), 0.0)


def _cmp_finish(s0, s1, phi, kg, *, n_cmp):
    bsz = s0.shape[0]
    nb = s0.shape[1] // KV_SLABS
    assert n_cmp <= nb - 1
    shape = jax.ShapeDtypeStruct((bsz, NSA_KV_HEADS, nb, HEAD_DIM), F32)
    return pl.pallas_call(
        functools.partial(_cmp_fin_kernel, n_cmp=n_cmp),
        grid=(bsz,),
        in_specs=[pl.BlockSpec((1, nb * KV_SLABS, HEAD_DIM), lambda b: (b, 0, 0)),
                  pl.BlockSpec((1, nb * KV_SLABS, HEAD_DIM), lambda b: (b, 0, 0)),
                  pl.BlockSpec((2, NSA_KV_HEADS, HEAD_DIM, HEAD_DIM), lambda b: (0, 0, 0, 0)),
                  pl.BlockSpec((1, HEAD_DIM), lambda b: (0, 0))],
        out_specs=[pl.BlockSpec((1, NSA_KV_HEADS, nb, HEAD_DIM), lambda b: (b, 0, 0, 0)),
                   pl.BlockSpec((1, NSA_KV_HEADS, nb, HEAD_DIM), lambda b: (b, 0, 0, 0))],
        out_shape=[shape, shape],
        compiler_params=_cparams(("parallel",)),
        name="cmp_finish",
    )(s0, s1, phi, kg.reshape(1, HEAD_DIM))


def _stack_heads(ref, g, tq, dtype):
    parts = [ref[:, (g * NSA_GROUP + r) * HEAD_DIM:(g * NSA_GROUP + r + 1) * HEAD_DIM].astype(dtype)
             for r in range(NSA_GROUP)]
    return jnp.concatenate(parts, axis=0)


def _cmp_topk_kernel(q_ref, kc_ref, vc_ref, ov_ref, ocmp_ref, sel_ref, *, tq, n_cmp, n_slc, offset, token_lanes):
    i = pl.program_id(1)
    nb = kc_ref.shape[2]
    nsp = sel_ref.shape[2]
    rows = NSA_GROUP * tq
    r4 = lax.broadcasted_iota(jnp.int32, (rows, nb), 0)
    n4 = lax.broadcasted_iota(jnp.int32, (rows, nb), 1)
    tpos = offset + i * tq + (r4 & (tq - 1))
    valid = (n4 * CMP_STRIDE + (CMP_BLOCK - 1) <= tpos) & (n4 < n_cmp)

    if token_lanes:
        nsr = -(-n_slc // SUBLANES) * SUBLANES
        shape, blk_axis, tok_axis = (nsr, tq), 0, 1
    else:
        shape, blk_axis, tok_axis = (tq, nsp), 1, 0
    blk = lax.broadcasted_iota(jnp.int32, shape, blk_axis)
    tq_pos = offset + i * tq + lax.broadcasted_iota(jnp.int32, shape, tok_axis)
    cur = tq_pos // SLC_BLOCK
    forced = (blk == 0) | ((blk <= cur) & (blk > cur - SLC_LOCAL))
    future = blk > cur
    in_range = blk < n_slc
    topk = min(SLC_TOPK, n_slc)

    for g in range(NSA_KV_HEADS):
        q4 = _stack_heads(q_ref, g, tq, F32)
        s = _dot_nt(q4, kc_ref[0, g], HI) * SCALE
        s = jnp.where(valid, s, -jnp.inf)
        m = jnp.max(s, axis=-1, keepdims=True)
        m = jnp.where(m == -jnp.inf, 0.0, m)
        p = jnp.exp(s - m)
        p = p / jnp.maximum(jnp.sum(p, axis=-1, keepdims=True), jnp.finfo(jnp.float32).tiny)
        o = _dot(p, vc_ref[0, g], HI)
        psum = p[0:tq]
        for r in range(NSA_GROUP):
            col = (g * NSA_GROUP + r) * HEAD_DIM
            ocmp_ref[:, col:col + HEAD_DIM] = o[r * tq:(r + 1) * tq]
            if r:
                psum = psum + p[r * tq:(r + 1) * tq]
        if token_lanes:
            imp = _dot_nt(ov_ref[0:shape[0], :], psum, HI)
        else:
            imp = _dot(psum, ov_ref[...], HI)
        score = jnp.where(forced, FORCE_SCORE, jnp.where(future, -FORCE_SCORE, imp))
        score = jnp.where(in_range, score, -jnp.inf)
        rank = jnp.zeros(shape, jnp.int32)
        for j in range(n_slc):
            cj = score[j:j + 1, :] if token_lanes else score[:, j:j + 1]
            ahead = (cj > score) | ((cj == score) & (blk > j))
            rank = rank + ahead.astype(jnp.int32)
        sel = ((rank < topk) & in_range).astype(F32)
        if token_lanes:
            sel = jnp.concatenate([sel, jnp.zeros((nsp - shape[0], tq), F32)], axis=0).T
        sel_ref[g] = sel


def _cmp_topk(qn, kc, vc, overlap, *, bsz, rows_per_seq, tq, n_cmp, n_slc, offset):
    m = qn.shape[0]
    nt = rows_per_seq // tq
    nb = kc.shape[2]
    nsp = overlap.shape[1]
    token_lanes = tq % LANES == 0 and nsp == tq
    if token_lanes:
        overlap = overlap.T
    return pl.pallas_call(
        functools.partial(_cmp_topk_kernel, tq=tq, n_cmp=n_cmp, n_slc=n_slc, offset=offset,
                          token_lanes=token_lanes),
        grid=(bsz, nt),
        in_specs=[pl.BlockSpec((tq, NSA_WIDTH), lambda b, i: (b * nt + i, 0)),
                  pl.BlockSpec((1, NSA_KV_HEADS, nb, HEAD_DIM), lambda b, i: (b, 0, 0, 0)),
                  pl.BlockSpec((1, NSA_KV_HEADS, nb, HEAD_DIM), lambda b, i: (b, 0, 0, 0)),
                  pl.BlockSpec(overlap.shape, lambda b, i: (0, 0))],
        out_specs=[pl.BlockSpec((tq, NSA_WIDTH), lambda b, i: (b * nt + i, 0)),
                   pl.BlockSpec((NSA_KV_HEADS, tq, nsp), lambda b, i: (0, b * nt + i, 0))],
        out_shape=[jax.ShapeDtypeStruct((m, NSA_WIDTH), F32),
                   jax.ShapeDtypeStruct((NSA_KV_HEADS, m, nsp), F32)],
        compiler_params=_cparams(("parallel", "arbitrary")),
        name="cmp_attn_topk",
    )(qn, kc, vc, overlap)


def _overlap_matrix(nb, nsp, n_cmp, n_slc):
    cs = np.arange(nb) * CMP_STRIDE
    ss = np.arange(nsp) * SLC_BLOCK
    lo = np.maximum(cs[:, None], ss[None, :])
    hi = np.minimum(cs[:, None] + CMP_BLOCK, ss[None, :] + SLC_BLOCK)
    ov = (np.maximum(hi - lo, 0) / CMP_BLOCK).astype(np.float32)
    ov[n_cmp:, :] = 0.0
    ov[:, n_slc:] = 0.0
    return jnp.asarray(ov)


def _flash_init(m_ref, l_ref, acc_ref):
    m_ref[...] = jnp.full(m_ref.shape, NEG, F32)
    l_ref[...] = jnp.zeros(l_ref.shape, F32)
    acc_ref[...] = jnp.zeros(acc_ref.shape, F32)


def _flash_update(s, mask, v, m_ref, l_ref, acc_ref, prec=None):
    s = jnp.where(mask, s, NEG)
    m_prev = m_ref[...]
    m_new = jnp.maximum(m_prev, jnp.max(s, axis=-1, keepdims=True))
    alpha = jnp.exp(m_prev - m_new)
    p = jnp.exp(s - m_new)
    l_ref[...] = alpha * l_ref[...] + jnp.sum(p, axis=-1, keepdims=True)
    acc_ref[...] = alpha * acc_ref[...] + _dot(p.astype(v.dtype), v, prec)
    m_ref[...] = m_new


def _pattn_kernel(q_ref, ks_ref, vs_ref, kw_ref, vw_ref, ocmp_ref, sel_ref, gate_ref, exp_ref, o_ref,
                  sexp_ref, m_ref, l_ref, acc_ref, *, tq, tk, wlen):
    i = pl.program_id(2)
    t = ks_ref.shape[0]
    q4 = jnp.concatenate([(q_ref[:, r * HEAD_DIM:(r + 1) * HEAD_DIM] * SCALE).astype(BF16)
                          for r in range(NSA_GROUP)], axis=0)
    sexp_ref[...] = _dot(sel_ref[0].astype(BF16), exp_ref[...])
    qpos = i * tq + lax.broadcasted_iota(jnp.int32, (tq, tk), 0)
    lane = lax.broadcasted_iota(jnp.int32, (tq, tk), 1)

    def tile4(x):
        return jnp.concatenate([x] * NSA_GROUP, axis=0)

    def slc_step(j, carry):
        start = pl.multiple_of(j * tk, tk)
        s = _dot_nt(q4, ks_ref[pl.ds(start, tk), :])
        mask = (sexp_ref[:, pl.ds(start, tk)] > 0.5) & (start + lane <= qpos)
        _flash_update(s, tile4(mask), vs_ref[pl.ds(start, tk), :], m_ref, l_ref, acc_ref)
        return carry

    _flash_init(m_ref, l_ref, acc_ref)
    lax.fori_loop(0, ((i + 1) * tq + tk - 1) // tk, slc_step, 0)
    o_slc = acc_ref[...] / l_ref[...]

    wstart = pl.multiple_of(jnp.clip((i + 1) * tq - wlen, 0, t - wlen), tq)
    s = _dot_nt(q4, kw_ref[pl.ds(wstart, wlen), :])
    wdiff = (i * tq + lax.broadcasted_iota(jnp.int32, (tq, wlen), 0)
             - (wstart + lax.broadcasted_iota(jnp.int32, (tq, wlen), 1)))
    s = jnp.where(tile4((wdiff >= 0) & (wdiff < WINDOW)), s, NEG)
    p = jnp.exp(s - jnp.max(s, axis=-1, keepdims=True))
    o_win = _dot(p.astype(BF16), vw_ref[pl.ds(wstart, wlen), :]) / jnp.sum(p, axis=-1, keepdims=True)

    gates = gate_ref[0]
    for r in range(NSA_GROUP):
        sl = slice(r * tq, (r + 1) * tq)
        col = slice(r * HEAD_DIM, (r + 1) * HEAD_DIM)
        o = (gates[:, 3 * r:3 * r + 1] * ocmp_ref[:, col]
             + gates[:, 3 * r + 1:3 * r + 2] * o_slc[sl]
             + gates[:, 3 * r + 2:3 * r + 3] * o_win[sl])
        o_ref[:, col] = o.astype(o_ref.dtype)


def _prompt_attn(qn, slc_bf, win_bf, ocmp, sel, gates, *, bsz, t, tq):
    m = qn.shape[0]
    nt = t // tq
    gw = NSA_GROUP * HEAD_DIM
    half = NSA_KV_HEADS
    nsp = sel.shape[2]
    n_slc = -(-t // SLC_BLOCK)
    expand = np.zeros((nsp, t), np.float32)
    expand[np.arange(t) // SLC_BLOCK, np.arange(t)] = 1.0
    assert n_slc <= nsp
    tk = next(c for c in (512, 256, 128) if t % c == 0 and c >= tq)
    wlen = min(t, WINDOW + tq)
    assert wlen % tq == 0 and t % tq == 0
    kv_spec = lambda which: pl.BlockSpec((t, HEAD_DIM), lambda b, g, i: (b, which * half + g))
    return pl.pallas_call(
        functools.partial(_pattn_kernel, tq=tq, tk=tk, wlen=wlen),
        grid=(bsz, NSA_KV_HEADS, nt),
        in_specs=[pl.BlockSpec((tq, gw), lambda b, g, i: (b * nt + i, g)),
                  kv_spec(0), kv_spec(1), kv_spec(0), kv_spec(1),
                  pl.BlockSpec((tq, gw), lambda b, g, i: (b * nt + i, g)),
                  pl.BlockSpec((1, tq, nsp), lambda b, g, i: (g, b * nt + i, 0)),
                  pl.BlockSpec((1, tq, LANES), lambda b, g, i: (g, b * nt + i, 0)),
                  pl.BlockSpec((nsp, t), lambda b, g, i: (0, 0))],
        out_specs=pl.BlockSpec((tq, gw), lambda b, g, i: (b * nt + i, g)),
        out_shape=jax.ShapeDtypeStruct((m, NSA_WIDTH), BF16),
        scratch_shapes=[pltpu.VMEM((tq, t), F32),
                        pltpu.VMEM((NSA_GROUP * tq, 1), F32),
                        pltpu.VMEM((NSA_GROUP * tq, 1), F32),
                        pltpu.VMEM((NSA_GROUP * tq, HEAD_DIM), F32)],
        compiler_params=_cparams(("parallel", "parallel", "arbitrary")),
        name="nsa_prompt_attn",
    )(qn, slc_bf, slc_bf, win_bf, win_bf, ocmp, sel, gates, jnp.asarray(expand, BF16))


def _sattn_kernel(tbl_ref, q_ref, *refs, tq, n_steps, pg, past):
    del tbl_ref
    page_refs = refs[:pg]
    (snew_ref, wcache_ref, wnew_ref, ocmp_ref, sel_ref, gate_ref, exp_ref, o_ref, s_buf, v_buf) = refs[pg:]
    j = pl.program_id(1)
    rows = NSA_GROUP * tq
    half = NSA_KV_HEADS * HEAD_DIM
    nsp = sel_ref.shape[2]
    n_prev = wcache_ref.shape[0] // KV_SLABS

    def tile4(x):
        return jnp.concatenate([x] * NSA_GROUP, axis=0)

    q4_bf = [_stack_heads(q_ref, g, tq, BF16) for g in range(NSA_KV_HEADS)]
    for u in range(pg):
        key0 = pl.multiple_of((j * pg + u) * PAGE_SIZE, PAGE_SIZE)
        for g in range(NSA_KV_HEADS):
            k = page_refs[u][pl.ds(g, PAGE_SIZE, stride=KV_SLABS), :].astype(BF16)
            s_buf[g * rows:(g + 1) * rows, pl.ds(key0, PAGE_SIZE)] = _dot_nt(q4_bf[g], k)
            v_buf[g, pl.ds(key0, PAGE_SIZE), :] = (
                page_refs[u][pl.ds(NSA_KV_HEADS + g, PAGE_SIZE, stride=KV_SLABS), :].astype(BF16))

    @pl.when(j == n_steps - 1)
    def _():
        blk_lane = lax.broadcasted_iota(jnp.int32, (tq, nsp), 1)
        trow = lax.broadcasted_iota(jnp.int32, (tq, tq), 0)
        tcol = lax.broadcasted_iota(jnp.int32, (tq, tq), 1)
        causal = tcol <= trow
        wrow = lax.broadcasted_iota(jnp.int32, (tq, n_prev), 0)
        wcol = lax.broadcasted_iota(jnp.int32, (tq, n_prev), 1)
        wdiff = wrow + n_prev - wcol
        wmask = (wdiff >= 0) & (wdiff < WINDOW)
        ndiff = trow - tcol
        nmask = (ndiff >= 0) & (ndiff < WINDOW)
        gates_all = gate_ref[...]
        for g in range(NSA_KV_HEADS):
            q4 = _stack_heads(q_ref, g, tq, F32)
            selg = sel_ref[g]
            sexp = _dot(selg.astype(BF16), exp_ref[...])
            sc = jnp.where(tile4(sexp > 0.5), s_buf[g * rows:(g + 1) * rows, :] * SCALE, NEG)
            kn = snew_ref[:, g * HEAD_DIM:(g + 1) * HEAD_DIM]
            vn = snew_ref[:, half + g * HEAD_DIM:half + (g + 1) * HEAD_DIM]
            flag = jnp.zeros((tq, tq), jnp.bool_)
            for u in range(tq):
                col = jnp.sum(jnp.where(blk_lane == (past + u) // SLC_BLOCK, selg, 0.0), axis=1, keepdims=True) > 0.5
                flag = flag | (col & (tcol == u))
            sn = jnp.where(tile4(flag & causal), _dot_nt(q4, kn, HI) * SCALE, NEG)
            mx = jnp.maximum(jnp.max(sc, axis=-1, keepdims=True), jnp.max(sn, axis=-1, keepdims=True))
            pc = jnp.exp(sc - mx)
            pn = jnp.exp(sn - mx)
            den = jnp.sum(pc, axis=-1, keepdims=True) + jnp.sum(pn, axis=-1, keepdims=True)
            o_slc = (_dot(pc.astype(BF16), v_buf[g]) + _dot(pn, vn, HI)) / den

            kc = wcache_ref[pl.ds(g, n_prev, stride=KV_SLABS), :].astype(BF16)
            vc = wcache_ref[pl.ds(NSA_KV_HEADS + g, n_prev, stride=KV_SLABS), :].astype(BF16)
            kw = wnew_ref[:, g * HEAD_DIM:(g + 1) * HEAD_DIM]
            vw = wnew_ref[:, half + g * HEAD_DIM:half + (g + 1) * HEAD_DIM]
            sc = jnp.where(tile4(wmask), _dot_nt(q4.astype(BF16), kc) * SCALE, NEG)
            sn = jnp.where(tile4(nmask), _dot_nt(q4, kw, HI) * SCALE, NEG)
            mx = jnp.maximum(jnp.max(sc, axis=-1, keepdims=True), jnp.max(sn, axis=-1, keepdims=True))
            pc = jnp.exp(sc - mx)
            pn = jnp.exp(sn - mx)
            den = jnp.sum(pc, axis=-1, keepdims=True) + jnp.sum(pn, axis=-1, keepdims=True)
            o_win = (_dot(pc.astype(BF16), vc) + _dot(pn, vw, HI)) / den

            gates = gates_all[g]
            for r in range(NSA_GROUP):
                sl = slice(r * tq, (r + 1) * tq)
                col = slice((g * NSA_GROUP + r) * HEAD_DIM, (g * NSA_GROUP + r + 1) * HEAD_DIM)
                o = (gates[:, 3 * r:3 * r + 1] * ocmp_ref[:, col]
                     + gates[:, 3 * r + 1:3 * r + 2] * o_slc[sl]
                     + gates[:, 3 * r + 2:3 * r + 3] * o_win[sl])
                o_ref[:, col] = o.astype(o_ref.dtype)


def _sample_attn(qn, cache2d, page_table, slc_new, win_cache2d, win_row0, win_new, ocmp, sel, gates,
                 *, bsz, tq, past, n_prev):
    n_pages = page_table.shape[1]
    pg = math.gcd(n_pages, 8)
    n_steps = n_pages // pg
    nsp = sel.shape[2]
    rows = NSA_GROUP * tq
    page_rows = PAGE_SIZE * KV_SLABS
    expand = np.zeros((nsp, past), np.float32)
    expand[np.arange(past) // SLC_BLOCK, np.arange(past)] = 1.0

    def page_spec(u):
        return pl.BlockSpec((page_rows, HEAD_DIM), lambda b, j, tbl: (tbl[b, j * pg + u], 0))

    grid_spec = pltpu.PrefetchScalarGridSpec(
        num_scalar_prefetch=1,
        grid=(bsz, n_steps),
        in_specs=[pl.BlockSpec((tq, NSA_WIDTH), lambda b, j, tbl: (b, 0))]
        + [page_spec(u) for u in range(pg)]
        + [pl.BlockSpec((tq, KV_WIDTH), lambda b, j, tbl: (b, 0)),
           pl.BlockSpec((n_prev * KV_SLABS, HEAD_DIM), lambda b, j, tbl: (win_row0 + b, 0)),
           pl.BlockSpec((tq, KV_WIDTH), lambda b, j, tbl: (b, 0)),
           pl.BlockSpec((tq, NSA_WIDTH), lambda b, j, tbl: (b, 0)),
           pl.BlockSpec((NSA_KV_HEADS, tq, nsp), lambda b, j, tbl: (0, b, 0)),
           pl.BlockSpec((NSA_KV_HEADS, tq, LANES), lambda b, j, tbl: (0, b, 0)),
           pl.BlockSpec((nsp, past), lambda b, j, tbl: (0, 0))],
        out_specs=pl.BlockSpec((tq, NSA_WIDTH), lambda b, j, tbl: (b, 0)),
        scratch_shapes=[pltpu.VMEM((NSA_KV_HEADS * rows, past), F32),
                        pltpu.VMEM((NSA_KV_HEADS, past, HEAD_DIM), BF16)],
    )
    return pl.pallas_call(
        functools.partial(_sattn_kernel, tq=tq, n_steps=n_steps, pg=pg, past=past),
        grid_spec=grid_spec,
        out_shape=jax.ShapeDtypeStruct((bsz * tq, NSA_WIDTH), BF16),
        compiler_params=_cparams(("parallel", "arbitrary")),
        name="nsa_sample_attn",
    )(page_table, qn, *([cache2d] * pg), slc_new, win_cache2d, win_new, ocmp, sel, gates,
      jnp.asarray(expand, BF16))


def _split_w_in(w_in):
    wt = w_in.T
    pad = jnp.zeros((LANES - 2 * GDN_HEADS - 3 * NSA_HEADS, wt.shape[1]), w_in.dtype)
    wt_small = jnp.concatenate([wt[W_IN_B0:W_IN_N0], wt[W_IN_G0:], pad], axis=0)
    return wt, wt[W_IN_N0:W_IN_G0], wt_small


def _in_proj(x2d, p, *, tm):
    xn = _rms_cast(x2d, p["attn_norm_g"], tm=min(tm, 512))
    w_full, w_b, w_small = p["w_in"]
    proj_a = _mm_nw([xn], w_full, n=PROJ_A_WIDTH, tm=tm, tn=512, w_t=True)
    proj_b = _mm_nw([xn], w_b, n=PROJ_B_WIDTH, tm=tm, tn=512, w_t=True)
    small = _mm_nw([xn], w_small, n=LANES, tm=tm, tn=LANES, w_t=True)
    return proj_a, proj_b, small


def _pos_weights(cmp_pos_w):
    w = cmp_pos_w.reshape(2, CMP_BLOCK // CMP_STRIDE, CMP_STRIDE, NSA_KV_HEADS)
    w = jnp.transpose(w, (1, 2, 0, 3))
    return jnp.broadcast_to(w[..., None], w.shape + (HEAD_DIM,)).reshape(2, CMP_STRIDE, KV_SLABS, HEAD_DIM)


def _mixers_out(x2d, o_gdn, o_nsa, w_o, mlp_norm_g, w_up, w_down, *, tm):
    x1 = _mm_nw([o_gdn, o_nsa], w_o, n=w_o.shape[1], tm=tm, tn=512, res=x2d)
    xn = _rms_cast(x1, mlp_norm_g, tm=min(tm, 512))
    hid = _mm_nw([xn], w_up, n=w_up.shape[1], tm=tm, tn=512, relu2=True, out_dtype=BF16)
    return _mm_res(hid, w_down, x1, tm=tm, tn=min(1024, w_down.shape[1]), tk=min(2048, w_down.shape[0]))


def _round_up(x, n):
    return -(-x // n) * n


def _prompt_layer(x, p):
    bsz, t, d = x.shape
    m = bsz * t
    x2d = x.reshape(m, d)
    proj_a, proj_b, small = _in_proj(x2d, p, tm=1024)
    conv0 = jnp.zeros((bsz, CONV_W - 1, 3 * GDN_WIDTH), F32)
    s_zero = jnp.zeros((bsz, GDN_HEADS, HEAD_DIM, HEAD_DIM), F32)
    o_gdn, s_new = _gdn(proj_a, small, conv0, s_zero, p["gdn_conv_w"], p["gdn_a_log"], p["gdn_dt_bias"],
                        p["gdn_norm_g"], bsz=bsz, rows_per_seq=t, t_valid=t)
    qn, slc_n, win_n, gates, slc_bf, win_bf = _nsa_prep(proj_b, small, p["q_norm_g"], p["k_norm_g"], tm=512)

    n_cmp = (t - CMP_BLOCK) // CMP_STRIDE + 1
    n_slc = -(-t // SLC_BLOCK)
    n_pages = t // PAGE_SIZE
    table = jnp.arange(bsz * n_pages, dtype=jnp.int32).reshape(bsz, n_pages)
    kv_tail = (2, NSA_KV_HEADS, HEAD_DIM)
    cmp_new = proj_b[:, COL_CMP:COL_CMP + KV_WIDTH].reshape((bsz, t) + kv_tail)
    s0, s1 = _pool(cmp_new.reshape(m * KV_SLABS, HEAD_DIM), table, p["cmp_w"])
    kc, vc = _cmp_finish(s0, s1, p["cmp_phi"], p["k_norm_g"][0], n_cmp=n_cmp)
    nb = kc.shape[2]
    nsp = _round_up(n_slc, LANES)
    tq = 128
    ocmp, sel = _cmp_topk(qn, kc, vc, _overlap_matrix(nb, nsp, n_cmp, n_slc), bsz=bsz, rows_per_seq=t, tq=tq,
                          n_cmp=n_cmp, n_slc=n_slc, offset=0)
    o_nsa = _prompt_attn(qn, slc_bf, win_bf, ocmp, sel, gates, bsz=bsz, t=t, tq=tq)
    y = _mixers_out(x2d, o_gdn, o_nsa, p["w_o"], p["mlp_norm_g"], p["w_up"], p["w_down"], tm=1024)

    keep = min(WINDOW, t)
    return (y.reshape(bsz, t, d),
            cmp_new,
            slc_n.reshape((bsz, t) + kv_tail),
            win_n.reshape((bsz, t) + kv_tail)[:, t - keep:],
            proj_a.reshape(bsz, t, PROJ_A_WIDTH)[:, t - (CONV_W - 1):, 0:3 * GDN_WIDTH],
            s_new)


def _sample_layer(x, layer, cache_cmp_kv, cache_slc_kv, cache_win_kv, conv_buf, s0_state, page_table, p):
    bsz, t, d = x.shape
    tp = _round_up(t, SUBLANES)
    n_pages = page_table.shape[1]
    past = n_pages * PAGE_SIZE
    assert t <= SLC_BLOCK and past % SLC_BLOCK == 0 and t >= CONV_W - 1
    x2d = jnp.pad(x, ((0, 0), (0, tp - t), (0, 0))).reshape(bsz * tp, d)
    proj_a, proj_b, small = _in_proj(x2d, p, tm=bsz * tp)
    o_gdn, s_new = _gdn(proj_a, small, conv_buf, s0_state, p["gdn_conv_w"], p["gdn_a_log"], p["gdn_dt_bias"],
                        p["gdn_norm_g"], bsz=bsz, rows_per_seq=tp, t_valid=t)
    qn, slc_n, win_n, gates, _, _ = _nsa_prep(proj_b, small, p["q_norm_g"], p["k_norm_g"], tm=bsz * tp)

    total = past + t
    n_cmp = (total - CMP_BLOCK) // CMP_STRIDE + 1
    n_slc = -(-total // SLC_BLOCK)
    assert (n_cmp - 1) * CMP_STRIDE + CMP_BLOCK <= past
    depth, n_phys = cache_cmp_kv.shape[:2]
    table = page_table + layer * n_phys
    cache_rows = depth * n_phys * PAGE_SIZE * KV_SLABS
    s0, s1 = _pool(cache_cmp_kv.reshape(cache_rows, HEAD_DIM), table, p["cmp_w"])
    kc, vc = _cmp_finish(s0, s1, p["cmp_phi"], p["k_norm_g"][0], n_cmp=n_cmp)
    nb = kc.shape[2]
    nsp = _round_up(n_slc, LANES)
    ocmp, sel = _cmp_topk(qn, kc, vc, _overlap_matrix(nb, nsp, n_cmp, n_slc), bsz=bsz, rows_per_seq=tp, tq=tp,
                          n_cmp=n_cmp, n_slc=n_slc, offset=past)
    n_prev = cache_win_kv.shape[2]
    o_nsa = _sample_attn(qn, cache_slc_kv.reshape(cache_rows, HEAD_DIM), table, slc_n,
                         cache_win_kv.reshape(depth * bsz * n_prev * KV_SLABS, HEAD_DIM), layer * bsz,
                         win_n, ocmp, sel, gates, bsz=bsz, tq=tp, past=past, n_prev=n_prev)
    y = _mixers_out(x2d, o_gdn, o_nsa, p["w_o"], p["mlp_norm_g"], p["w_up"], p["w_down"], tm=bsz * tp)

    kv_tail = (2, NSA_KV_HEADS, HEAD_DIM)
    win_all = jnp.concatenate([cache_win_kv[layer], win_n.reshape((bsz, tp) + kv_tail)[:, :t]], axis=1)
    keep = min(WINDOW, n_prev + t)
    return (y.reshape(bsz, tp, d)[:, :t],
            proj_b.reshape(bsz, tp, PROJ_B_WIDTH)[:, :t, COL_CMP:COL_CMP + KV_WIDTH].reshape((bsz, t) + kv_tail),
            slc_n.reshape((bsz, tp) + kv_tail)[:, :t],
            win_all[:, n_prev + t - keep:],
            proj_a.reshape(bsz, tp, PROJ_A_WIDTH)[:, t - (CONV_W - 1):t, 0:3 * GDN_WIDTH],
            s_new)


def kernel(x_prompt, x_sample, cache_cmp_kv, cache_slc_kv, cache_win_kv, cache_gdn_conv, state_gdn, page_table, attn_norm_g, w_in, gdn_conv_w, gdn_a_log, gdn_dt_bias, gdn_norm_g, q_norm_g, k_norm_g, cmp_pos_w, cmp_phi, w_o, mlp_norm_g, w_up, w_down):
    depth = w_in.shape[0]
    yp, ys = x_prompt, x_sample
    per_layer = []
    for layer in range(depth):
        p = {
            "attn_norm_g": attn_norm_g[layer], "w_in": _split_w_in(w_in[layer]),
            "gdn_conv_w": gdn_conv_w[layer], "gdn_a_log": gdn_a_log[layer], "gdn_dt_bias": gdn_dt_bias[layer],
            "gdn_norm_g": gdn_norm_g[layer], "q_norm_g": q_norm_g[layer], "k_norm_g": k_norm_g[layer],
            "cmp_w": _pos_weights(cmp_pos_w[layer]), "cmp_phi": cmp_phi[layer],
            "w_o": w_o[layer], "mlp_norm_g": mlp_norm_g[layer],
            "w_up": w_up[layer], "w_down": w_down[layer].astype(BF16),
        }
        yp, cmp_p, slc_p, win_p, conv_p, s_p = _prompt_layer(yp, p)
        ys, cmp_s, slc_s, win_s, conv_s, s_s = _sample_layer(
            ys, layer, cache_cmp_kv, cache_slc_kv, cache_win_kv, cache_gdn_conv[layer],
            state_gdn[layer], page_table, p)
        per_layer.append((cmp_p, cmp_s, slc_p, slc_s, win_p, win_s, conv_p, conv_s, s_p, s_s))
    st = [jnp.stack(z, axis=0) for z in zip(*per_layer)]
    return (yp, ys) + tuple(st)
```

```python
import functools
import math

import numpy as np
import jax
import jax.numpy as jnp
from jax import lax
from jax.experimental import pallas as pl
from jax.experimental.pallas import tpu as pltpu

F32 = jnp.float32
BF16 = jnp.bfloat16
HI = lax.Precision.HIGHEST

HEAD_DIM = 128
GDN_HEADS = 16
NSA_HEADS = 16
NSA_KV_HEADS = 4
NSA_GROUP = NSA_HEADS // NSA_KV_HEADS
GDN_WIDTH = GDN_HEADS * HEAD_DIM
NSA_WIDTH = NSA_HEADS * HEAD_DIM
KV_WIDTH = 2 * NSA_KV_HEADS * HEAD_DIM
KV_SLABS = 2 * NSA_KV_HEADS
CONV_W = 4
GDN_CHUNK = 64
GDN_HEAD_GROUP = 8
CMP_BLOCK = 32
CMP_STRIDE = 16
SLC_BLOCK = 64
SLC_TOPK = 16
SLC_LOCAL = 2
WINDOW = 512
PAGE_SIZE = 128
EPS = 1e-6
FORCE_SCORE = 1e9
NEG = -1e30
SCALE = HEAD_DIM ** -0.5

LANES = 128
SUBLANES = 8
VMEM_LIMIT = 52 * 1024 * 1024

COL_Q, COL_K, COL_V, COL_Z = 0, GDN_WIDTH, 2 * GDN_WIDTH, 3 * GDN_WIDTH
PROJ_A_WIDTH = 4 * GDN_WIDTH
COL_NQ = 0
COL_CMP = COL_NQ + NSA_WIDTH
COL_SLC = COL_CMP + KV_WIDTH
COL_WIN = COL_SLC + KV_WIDTH
PROJ_B_WIDTH = COL_WIN + KV_WIDTH
SMALL_B, SMALL_A, SMALL_GATE = 0, GDN_HEADS, 2 * GDN_HEADS
W_IN_B0 = PROJ_A_WIDTH
W_IN_N0 = W_IN_B0 + 2 * GDN_HEADS
W_IN_G0 = W_IN_N0 + PROJ_B_WIDTH


def _cparams(sem):
    return pltpu.CompilerParams(dimension_semantics=sem, vmem_limit_bytes=VMEM_LIMIT)


def _dot(a, b, prec=None):
    return jnp.dot(a, b, preferred_element_type=F32, precision=prec)


def _dot_nt(a, b, prec=None):
    return lax.dot_general(a, b, (((1,), (1,)), ((), ())), preferred_element_type=F32, precision=prec)


def _dot_tn(a, b, prec=None):
    return lax.dot_general(a, b, (((0,), (0,)), ((), ())), preferred_element_type=F32, precision=prec)


def _rms(x, g):
    return x * lax.rsqrt(jnp.mean(x * x, axis=-1, keepdims=True) + EPS) * g


def _silu(x):
    return x * jax.nn.sigmoid(x)


def _softplus(x):
    return jnp.maximum(x, 0.0) + jnp.log1p(jnp.exp(-jnp.abs(x)))


def _rms_cast_kernel(x_ref, g_ref, o_ref):
    o_ref[...] = _rms(x_ref[...], g_ref[...]).astype(o_ref.dtype)


def _rms_cast(x, g, *, tm):
    m, k = x.shape
    assert m % tm == 0
    return pl.pallas_call(
        _rms_cast_kernel,
        grid=(m // tm,),
        in_specs=[pl.BlockSpec((tm, k), lambda i: (i, 0)), pl.BlockSpec((1, k), lambda i: (0, 0))],
        out_specs=pl.BlockSpec((tm, k), lambda i: (i, 0)),
        out_shape=jax.ShapeDtypeStruct((m, k), BF16),
        compiler_params=_cparams(("parallel",)),
        name="rms_cast",
    )(x, g.reshape(1, k))


def _mm_nw_kernel(*refs, n_a, relu2, has_res, w_t):
    a_refs = refs[:n_a]
    w_ref = refs[n_a]
    r_ref = refs[n_a + 1] if has_res else None
    o_ref, wbf_ref = refs[-2:]

    @pl.when(pl.program_id(1) == 0)
    def _():
        w = w_ref[...]
        wbf_ref[...] = (w.T if w_t else w).astype(BF16)

    y, k0 = None, 0
    for a_ref in a_refs:
        part = _dot(a_ref[...], wbf_ref[k0:k0 + a_ref.shape[1], :])
        y = part if y is None else y + part
        k0 += a_ref.shape[1]
    if relu2:
        y = jnp.square(jnp.maximum(y, 0.0))
    if has_res:
        y = y + r_ref[...]
    o_ref[...] = y.astype(o_ref.dtype)


def _mm_nw(acts, w, *, n, tm, tn, res=None, relu2=False, out_dtype=F32, w_t=False):
    m = acts[0].shape[0]
    k = sum(a.shape[1] for a in acts)
    assert m % tm == 0 and n % tn == 0 and w.shape[1 if w_t else 0] == k
    in_specs = [pl.BlockSpec((tm, a.shape[1]), lambda j, i: (i, 0)) for a in acts]
    in_specs.append(pl.BlockSpec((tn, k), lambda j, i: (j, 0)) if w_t else pl.BlockSpec((k, tn), lambda j, i: (0, j)))
    args = list(acts) + [w]
    if res is not None:
        in_specs.append(pl.BlockSpec((tm, tn), lambda j, i: (i, j)))
        args.append(res)
    return pl.pallas_call(
        functools.partial(_mm_nw_kernel, n_a=len(acts), relu2=relu2, has_res=res is not None, w_t=w_t),
        grid=(n // tn, m // tm),
        in_specs=in_specs,
        out_specs=pl.BlockSpec((tm, tn), lambda j, i: (i, j)),
        out_shape=jax.ShapeDtypeStruct((m, n), out_dtype),
        scratch_shapes=[pltpu.VMEM((k, tn), BF16)],
        compiler_params=_cparams(("parallel", "arbitrary")),
        name="matmul_wcast",
    )(*args)


def _mm_res_kernel(a_ref, w_ref, r_ref, o_ref, acc_ref, *, nk):
    kk = pl.program_id(2)

    @pl.when(kk == 0)
    def _():
        acc_ref[...] = jnp.zeros_like(acc_ref)

    acc_ref[...] += _dot(a_ref[...], w_ref[...])

    @pl.when(kk == nk - 1)
    def _():
        o_ref[...] = r_ref[...] + acc_ref[...]


def _mm_res(a, w, res, *, tm, tn, tk):
    m, k = a.shape
    n = w.shape[1]
    assert m % tm == 0 and n % tn == 0 and k % tk == 0
    nk = k // tk
    return pl.pallas_call(
        functools.partial(_mm_res_kernel, nk=nk),
        grid=(m // tm, n // tn, nk),
        in_specs=[pl.BlockSpec((tm, tk), lambda i, j, kk: (i, kk)),
                  pl.BlockSpec((tk, tn), lambda i, j, kk: (kk, j)),
                  pl.BlockSpec((tm, tn), lambda i, j, kk: (i, j))],
        out_specs=pl.BlockSpec((tm, tn), lambda i, j, kk: (i, j)),
        out_shape=jax.ShapeDtypeStruct((m, n), F32),
        scratch_shapes=[pltpu.VMEM((tm, tn), F32)],
        compiler_params=_cparams(("parallel", "parallel", "arbitrary")),
        name="matmul_residual",
    )(a, w, res)


def _gdn_kernel(q_ref, k_ref, v_ref, z_ref, sm_ref, cbuf_ref, s0_ref, cw_ref, alog_ref, dt_ref, ng_ref,
                o_ref, sout_ref, xp_ref, st_ref, *, rows, nc, t_valid):
    C = GDN_CHUNK
    W = GDN_WIDTH
    c = pl.program_id(1)

    @pl.when(c == 0)
    def _():
        xp_ref[...] = jnp.zeros(xp_ref.shape, F32)
        xp_ref[SUBLANES - (CONV_W - 1):SUBLANES, :] = cbuf_ref[0]
        st_ref[...] = s0_ref[0]

    @pl.when(c > 0)
    def _():
        xp_ref[0:SUBLANES, :] = xp_ref[C:C + SUBLANES, :]

    xp_ref[SUBLANES:SUBLANES + rows, 0:W] = q_ref[...]
    xp_ref[SUBLANES:SUBLANES + rows, W:2 * W] = k_ref[...]
    xp_ref[SUBLANES:SUBLANES + rows, 2 * W:3 * W] = v_ref[...]

    row = lax.broadcasted_iota(jnp.int32, (C, LANES), 0)
    lane = lax.broadcasted_iota(jnp.int32, (C, LANES), 1)
    valid = (c * C + row) < t_valid
    if rows < C:
        sm = jnp.concatenate([sm_ref[...], jnp.zeros((C - rows, LANES), F32)], axis=0)
    else:
        sm = sm_ref[...]
    beta_all = jnp.where(valid, jax.nn.sigmoid(sm), 0.0)
    g_all = jnp.where(valid, -jnp.exp(alog_ref[...]) * _softplus(sm + dt_ref[...]), 0.0)
    ri = lax.broadcasted_iota(jnp.int32, (C, C), 0)
    ci = lax.broadcasted_iota(jnp.int32, (C, C), 1)
    tri = ri >= ci
    strict = ri > ci
    gcum = _dot(tri.astype(F32), g_all, HI)
    li = lax.broadcasted_iota(jnp.int32, (LANES, LANES), 0)
    lj = lax.broadcasted_iota(jnp.int32, (LANES, LANES), 1)
    gcum_t = _dot_nt((li == lj).astype(F32), gcum, HI)
    valid_col = valid[:, 0:1]
    eye = (ri == ci).astype(F32)
    n_sq = int(math.log2(C)) - 1

    def mm(a, b):
        return _dot(a.astype(BF16), b.astype(BF16))

    def mm_nt(a, b):
        return _dot_nt(a.astype(BF16), b.astype(BF16))

    def conv(col):
        acc = None
        for j in range(CONV_W):
            term = (xp_ref[pl.ds(SUBLANES - (CONV_W - 1) + j, C), col:col + LANES]
                    * cw_ref[j:j + 1, col:col + LANES])
            acc = term if acc is None else acc + term
        return _silu(acc)

    def head_group(hs):
        n = len(hs)
        qs = [conv(h * HEAD_DIM) for h in hs]
        ks = [conv(W + h * HEAD_DIM) for h in hs]
        vs = [conv(2 * W + h * HEAD_DIM) for h in hs]
        qs = [q * lax.rsqrt(jnp.sum(q * q, axis=-1, keepdims=True) + EPS) * SCALE for q in qs]
        ks = [jnp.where(valid_col, k * lax.rsqrt(jnp.sum(k * k, axis=-1, keepdims=True) + EPS), 0.0) for k in ks]
        gcs = [gcum[:, SMALL_A + h:SMALL_A + h + 1] for h in hs]
        bhs = [beta_all[:, SMALL_B + h:SMALL_B + h + 1] for h in hs]
        grs = [gcum_t[SMALL_A + h:SMALL_A + h + 1, :] for h in hs]
        decays = [jnp.where(tri, jnp.exp(jnp.where(tri, gc - gr, 0.0)), 0.0) for gc, gr in zip(gcs, grs)]
        egs = [jnp.exp(gc) for gc in gcs]

        kbs = [k * b for k, b in zip(ks, bhs)]
        kqs = [mm_nt(jnp.concatenate([kb, q], axis=0), k) for kb, q, k in zip(kbs, qs, ks)]
        nmats = [jnp.where(strict, -(kq[0:C] * d), 0.0) for kq, d in zip(kqs, decays)]
        aqks = [jnp.where(tri, kq[C:2 * C] * d, 0.0) for kq, d in zip(kqs, decays)]
        pinvs = [eye + nm for nm in nmats]
        npows = nmats
        for _ in range(n_sq):
            npows = [mm(np_, np_) for np_ in npows]
            pinvs = [p + mm(p, np_) for p, np_ in zip(pinvs, npows)]
        uws = [mm(p, jnp.concatenate([v * b, kb * eg], axis=1))
               for p, v, b, kb, eg in zip(pinvs, vs, bhs, kbs, egs)]

        ss = [st_ref[h] for h in hs]
        wss = [mm(jnp.concatenate([uw[:, HEAD_DIM:2 * HEAD_DIM], q * eg], axis=0), s)
               for uw, q, eg, s in zip(uws, qs, egs, ss)]
        v_news = [uw[:, 0:HEAD_DIM] - ws[0:C] for uw, ws in zip(uws, wss)]
        os_ = [ws[C:2 * C] + mm(aqk, vn) for ws, aqk, vn in zip(wss, aqks, v_news)]
        for i in range(n):
            g_last = gcs[i][C - 1:C, :]
            k_dec = ks[i] * jnp.exp(g_last - gcs[i])
            st_ref[hs[i]] = ss[i] * jnp.exp(g_last) + _dot_tn(k_dec.astype(BF16), v_news[i].astype(BF16))
        for i in range(n):
            col = hs[i] * HEAD_DIM
            o = _rms(os_[i], ng_ref[...])
            o_ref[:, col:col + HEAD_DIM] = (o[0:rows] * _silu(z_ref[:, col:col + HEAD_DIM])).astype(o_ref.dtype)

    for h0 in range(0, GDN_HEADS, GDN_HEAD_GROUP):
        head_group(list(range(h0, h0 + GDN_HEAD_GROUP)))

    @pl.when(c == nc - 1)
    def _():
        sout_ref[0] = st_ref[...]


def _gdn(proj, small, conv_buf, s0, conv_w, a_log, dt_bias, norm_g, *, bsz, rows_per_seq, t_valid):
    C = GDN_CHUNK
    if rows_per_seq >= C:
        assert rows_per_seq % C == 0
        rows, nc = C, rows_per_seq // C
    else:
        assert rows_per_seq % SUBLANES == 0
        rows, nc = rows_per_seq, 1
    wblk = GDN_WIDTH
    zeros = jnp.zeros((LANES - 2 * GDN_HEADS,), F32)
    alog_row = jnp.concatenate([jnp.zeros((GDN_HEADS,), F32), a_log, zeros]).reshape(1, LANES)
    dt_row = jnp.concatenate([jnp.zeros((GDN_HEADS,), F32), dt_bias, zeros]).reshape(1, LANES)
    row_map = lambda b, c: b * nc + c
    in_specs = [
        pl.BlockSpec((rows, wblk), lambda b, c: (row_map(b, c), COL_Q // wblk)),
        pl.BlockSpec((rows, wblk), lambda b, c: (row_map(b, c), COL_K // wblk)),
        pl.BlockSpec((rows, wblk), lambda b, c: (row_map(b, c), COL_V // wblk)),
        pl.BlockSpec((rows, wblk), lambda b, c: (row_map(b, c), COL_Z // wblk)),
        pl.BlockSpec((rows, LANES), lambda b, c: (row_map(b, c), 0)),
        pl.BlockSpec((1, CONV_W - 1, 3 * GDN_WIDTH), lambda b, c: (b, 0, 0)),
        pl.BlockSpec((1, GDN_HEADS, HEAD_DIM, HEAD_DIM), lambda b, c: (b, 0, 0, 0)),
        pl.BlockSpec((CONV_W, 3 * GDN_WIDTH), lambda b, c: (0, 0)),
        pl.BlockSpec((1, LANES), lambda b, c: (0, 0)),
        pl.BlockSpec((1, LANES), lambda b, c: (0, 0)),
        pl.BlockSpec((1, HEAD_DIM), lambda b, c: (0, 0)),
    ]
    out_specs = [
        pl.BlockSpec((rows, wblk), lambda b, c: (row_map(b, c), 0)),
        pl.BlockSpec((1, GDN_HEADS, HEAD_DIM, HEAD_DIM), lambda b, c: (b, 0, 0, 0)),
    ]
    return pl.pallas_call(
        functools.partial(_gdn_kernel, rows=rows, nc=nc, t_valid=t_valid),
        grid=(bsz, nc),
        in_specs=in_specs,
        out_specs=out_specs,
        out_shape=[jax.ShapeDtypeStruct((bsz * rows_per_seq, GDN_WIDTH), BF16),
                   jax.ShapeDtypeStruct((bsz, GDN_HEADS, HEAD_DIM, HEAD_DIM), F32)],
        scratch_shapes=[pltpu.VMEM((C + SUBLANES, 3 * GDN_WIDTH), F32),
                        pltpu.VMEM((GDN_HEADS, HEAD_DIM, HEAD_DIM), F32)],
        compiler_params=_cparams(("parallel", "arbitrary")),
        name="gdn_chunk_scan",
    )(proj, proj, proj, proj, small, conv_buf, s0, conv_w, alog_row, dt_row, norm_g.reshape(1, HEAD_DIM))


def _prep_kernel(nq_ref, cmp_ref, slc_ref, win_ref, sm_ref, qg_ref, kg_ref,
                 q_out, cmp_out, slc_out, win_out, gate_out, slc_bf_out, win_bf_out):
    tm = nq_ref.shape[0]
    qg = qg_ref[...]
    for h in range(NSA_HEADS):
        sl = slice(h * HEAD_DIM, (h + 1) * HEAD_DIM)
        q_out[:, sl] = _rms(nq_ref[:, sl], qg)
    for slab in range(KV_SLABS):
        sl = slice(slab * HEAD_DIM, (slab + 1) * HEAD_DIM)
        cmp_out[pl.ds(slab, tm, stride=KV_SLABS), :] = cmp_ref[:, sl]
    for src, dst, dst_bf, gi in ((slc_ref, slc_out, slc_bf_out, 1), (win_ref, win_out, win_bf_out, 2)):
        kg = kg_ref[gi:gi + 1, :]
        for slab in range(KV_SLABS):
            sl = slice(slab * HEAD_DIM, (slab + 1) * HEAD_DIM)
            x = _rms(src[:, sl], kg) if slab < NSA_KV_HEADS else src[:, sl]
            dst[pl.ds(slab, tm, stride=KV_SLABS), :] = x
            dst_bf[:, sl] = x.astype(BF16)
    sig = jax.nn.sigmoid(sm_ref[...])
    per = NSA_GROUP * 3
    for g in range(NSA_KV_HEADS):
        gate_out[g] = pltpu.roll(sig, LANES - (SMALL_GATE + per * g), axis=1)


def _nsa_prep(proj, small, q_norm_g, k_norm_g, *, tm):
    m = proj.shape[0]
    assert m % tm == 0
    return pl.pallas_call(
        _prep_kernel,
        grid=(m // tm,),
        in_specs=[pl.BlockSpec((tm, NSA_WIDTH), lambda i: (i, COL_NQ // NSA_WIDTH)),
                  pl.BlockSpec((tm, KV_WIDTH), lambda i: (i, COL_CMP // KV_WIDTH)),
                  pl.BlockSpec((tm, KV_WIDTH), lambda i: (i, COL_SLC // KV_WIDTH)),
                  pl.BlockSpec((tm, KV_WIDTH), lambda i: (i, COL_WIN // KV_WIDTH)),
                  pl.BlockSpec((tm, LANES), lambda i: (i, 0)),
                  pl.BlockSpec((1, HEAD_DIM), lambda i: (0, 0)),
                  pl.BlockSpec((3, HEAD_DIM), lambda i: (0, 0))],
        out_specs=[pl.BlockSpec((tm, NSA_WIDTH), lambda i: (i, 0)),
                   pl.BlockSpec((tm * KV_SLABS, HEAD_DIM), lambda i: (i, 0)),
                   pl.BlockSpec((tm * KV_SLABS, HEAD_DIM), lambda i: (i, 0)),
                   pl.BlockSpec((tm * KV_SLABS, HEAD_DIM), lambda i: (i, 0)),
                   pl.BlockSpec((NSA_KV_HEADS, tm, LANES), lambda i: (0, i, 0)),
                   pl.BlockSpec((tm, KV_WIDTH), lambda i: (i, 0)),
                   pl.BlockSpec((tm, KV_WIDTH), lambda i: (i, 0))],
        out_shape=[jax.ShapeDtypeStruct((m, NSA_WIDTH), F32),
                   jax.ShapeDtypeStruct((m * KV_SLABS, HEAD_DIM), F32),
                   jax.ShapeDtypeStruct((m * KV_SLABS, HEAD_DIM), F32),
                   jax.ShapeDtypeStruct((m * KV_SLABS, HEAD_DIM), F32),
                   jax.ShapeDtypeStruct((NSA_KV_HEADS, m, LANES), F32),
                   jax.ShapeDtypeStruct((m, KV_WIDTH), BF16),
                   jax.ShapeDtypeStruct((m, KV_WIDTH), BF16)],
        compiler_params=_cparams(("parallel",)),
        name="nsa_prep",
    )(proj, proj, proj, proj, small, q_norm_g.reshape(1, HEAD_DIM), k_norm_g)


def _pool_kernel(tbl_ref, *refs, pg):
    del tbl_ref
    page_refs = refs[:pg]
    w_ref, s0_ref, s1_ref = refs[pg:]
    sub = PAGE_SIZE // CMP_STRIDE
    out_rows = sub * KV_SLABS
    for u in range(pg):
        x = page_refs[u][...].reshape(sub, CMP_STRIDE, KV_SLABS, HEAD_DIM)
        s0_ref[0, u * out_rows:(u + 1) * out_rows, :] = jnp.sum(x * w_ref[0][None], axis=1).reshape(out_rows, HEAD_DIM)
        s1_ref[0, u * out_rows:(u + 1) * out_rows, :] = jnp.sum(x * w_ref[1][None], axis=1).reshape(out_rows, HEAD_DIM)


def _pool(rows2d, table, wexp):
    bsz, n_pages = table.shape
    pg = math.gcd(n_pages, 8)
    sub = PAGE_SIZE // CMP_STRIDE
    page_rows = PAGE_SIZE * KV_SLABS
    out_rows = pg * sub * KV_SLABS

    def page_spec(u):
        return pl.BlockSpec((page_rows, HEAD_DIM), lambda b, p, tbl: (tbl[b, p * pg + u], 0))

    grid_spec = pltpu.PrefetchScalarGridSpec(
        num_scalar_prefetch=1,
        grid=(bsz, n_pages // pg),
        in_specs=[page_spec(u) for u in range(pg)]
        + [pl.BlockSpec((2, CMP_STRIDE, KV_SLABS, HEAD_DIM), lambda b, p, tbl: (0, 0, 0, 0))],
        out_specs=[pl.BlockSpec((1, out_rows, HEAD_DIM), lambda b, p, tbl: (b, p, 0)),
                   pl.BlockSpec((1, out_rows, HEAD_DIM), lambda b, p, tbl: (b, p, 0))],
    )
    shape = jax.ShapeDtypeStruct((bsz, n_pages * sub * KV_SLABS, HEAD_DIM), F32)
    return pl.pallas_call(
        functools.partial(_pool_kernel, pg=pg), grid_spec=grid_spec, out_shape=[shape, shape],
        compiler_params=_cparams(("parallel", "arbitrary")), name="cmp_pool",
    )(table, *([rows2d] * pg), wexp)


def _cmp_fin_kernel(s0_ref, s1_ref, phi_ref, kg_ref, kc_ref, vc_ref, *, n_cmp):
    nb = s0_ref.shape[1] // KV_SLABS
    live = lax.broadcasted_iota(jnp.int32, (nb, HEAD_DIM), 0) < n_cmp
    zero_row = jnp.zeros((1, HEAD_DIM), F32)

    def pooled(slab):
        first = s0_ref[0, pl.ds(slab, nb, stride=KV_SLABS), :]
        second = s1_ref[0, pl.ds(KV_SLABS + slab, nb - 1, stride=KV_SLABS), :]
        return first + jnp.concatenate([second, zero_row], axis=0)

    for g in range(NSA_KV_HEADS):
        kc_ref[0, g] = jnp.where(live, _rms(_dot(pooled(g), phi_ref[0, g], HI), kg_ref[...]), 0.0)
        vc_ref[0, g] = jnp.where(live, _dot(pooled(NSA_KV_HEADS + g), phi_ref[1, g], HI), 0.0)


def _cmp_finish(s0, s1, phi, kg, *, n_cmp):
    bsz = s0.shape[0]
    nb = s0.shape[1] // KV_SLABS
    assert n_cmp <= nb - 1
    shape = jax.ShapeDtypeStruct((bsz, NSA_KV_HEADS, nb, HEAD_DIM), F32)
    return pl.pallas_call(
        functools.partial(_cmp_fin_kernel, n_cmp=n_cmp),
        grid=(bsz,),
        in_specs=[pl.BlockSpec((1, nb * KV_SLABS, HEAD_DIM), lambda b: (b, 0, 0)),
                  pl.BlockSpec((1, nb * KV_SLABS, HEAD_DIM), lambda b: (b, 0, 0)),
                  pl.BlockSpec((2, NSA_KV_HEADS, HEAD_DIM, HEAD_DIM), lambda b: (0, 0, 0, 0)),
                  pl.BlockSpec((1, HEAD_DIM), lambda b: (0, 0))],
        out_specs=[pl.BlockSpec((1, NSA_KV_HEADS, nb, HEAD_DIM), lambda b: (b, 0, 0, 0)),
                   pl.BlockSpec((1, NSA_KV_HEADS, nb, HEAD_DIM), lambda b: (b, 0, 0, 0))],
        out_shape=[shape, shape],
        compiler_params=_cparams(("parallel",)),
        name="cmp_finish",
    )(s0, s1, phi, kg.reshape(1, HEAD_DIM))


def _stack_heads(ref, g, tq, dtype):
    parts = [ref[:, (g * NSA_GROUP + r) * HEAD_DIM:(g * NSA_GROUP + r + 1) * HEAD_DIM].astype(dtype)
             for r in range(NSA_GROUP)]
    return jnp.concatenate(parts, axis=0)


def _cmp_topk_kernel(q_ref, kc_ref, vc_ref, ov_ref, ocmp_ref, sel_ref, *, tq, n_cmp, n_slc, offset, token_lanes):
    i = pl.program_id(1)
    nb = kc_ref.shape[2]
    nsp = sel_ref.shape[2]
    rows = NSA_GROUP * tq
    r4 = lax.broadcasted_iota(jnp.int32, (rows, nb), 0)
    n4 = lax.broadcasted_iota(jnp.int32, (rows, nb), 1)
    tpos = offset + i * tq + (r4 & (tq - 1))
    valid = (n4 * CMP_STRIDE + (CMP_BLOCK - 1) <= tpos) & (n4 < n_cmp)

    if token_lanes:
        nsr = -(-n_slc // SUBLANES) * SUBLANES
        shape, blk_axis, tok_axis = (nsr, tq), 0, 1
    else:
        shape, blk_axis, tok_axis = (tq, nsp), 1, 0
    blk = lax.broadcasted_iota(jnp.int32, shape, blk_axis)
    tq_pos = offset + i * tq + lax.broadcasted_iota(jnp.int32, shape, tok_axis)
    cur = tq_pos // SLC_BLOCK
    forced = (blk == 0) | ((blk <= cur) & (blk > cur - SLC_LOCAL))
    future = blk > cur
    in_range = blk < n_slc
    topk = min(SLC_TOPK, n_slc)

    for g in range(NSA_KV_HEADS):
        q4 = _stack_heads(q_ref, g, tq, F32)
        s = _dot_nt(q4, kc_ref[0, g], HI) * SCALE
        s = jnp.where(valid, s, -jnp.inf)
        m = jnp.max(s, axis=-1, keepdims=True)
        m = jnp.where(m == -jnp.inf, 0.0, m)
        p = jnp.exp(s - m)
        p = p / jnp.maximum(jnp.sum(p, axis=-1, keepdims=True), jnp.finfo(jnp.float32).tiny)
        o = _dot(p, vc_ref[0, g], HI)
        psum = p[0:tq]
        for r in range(NSA_GROUP):
            col = (g * NSA_GROUP + r) * HEAD_DIM
            ocmp_ref[:, col:col + HEAD_DIM] = o[r * tq:(r + 1) * tq]
            if r:
                psum = psum + p[r * tq:(r + 1) * tq]
        if token_lanes:
            imp = _dot_nt(ov_ref[0:shape[0], :], psum, HI)
        else:
            imp = _dot(psum, ov_ref[...], HI)
        score = jnp.where(forced, FORCE_SCORE, jnp.where(future, -FORCE_SCORE, imp))
        score = jnp.where(in_range, score, -jnp.inf)
        rank = jnp.zeros(shape, jnp.int32)
        for j in range(n_slc):
            cj = score[j:j + 1, :] if token_lanes else score[:, j:j + 1]
            ahead = (cj > score) | ((cj == score) & (blk > j))
            rank = rank + ahead.astype(jnp.int32)
        sel = ((rank < topk) & in_range).astype(F32)
        if token_lanes:
            sel = jnp.concatenate([sel, jnp.zeros((nsp - shape[0], tq), F32)], axis=0).T
        sel_ref[g] = sel


def _cmp_topk(qn, kc, vc, overlap, *, bsz, rows_per_seq, tq, n_cmp, n_slc, offset):
    m = qn.shape[0]
    nt = rows_per_seq // tq
    nb = kc.shape[2]
    nsp = overlap.shape[1]
    token_lanes = tq % LANES == 0 and nsp == tq
    if token_lanes:
        overlap = overlap.T
    return pl.pallas_call(
        functools.partial(_cmp_topk_kernel, tq=tq, n_cmp=n_cmp, n_slc=n_slc, offset=offset,
                          token_lanes=token_lanes),
        grid=(bsz, nt),
        in_specs=[pl.BlockSpec((tq, NSA_WIDTH), lambda b, i: (b * nt + i, 0)),
                  pl.BlockSpec((1, NSA_KV_HEADS, nb, HEAD_DIM), lambda b, i: (b, 0, 0, 0)),
                  pl.BlockSpec((1, NSA_KV_HEADS, nb, HEAD_DIM), lambda b, i: (b, 0, 0, 0)),
                  pl.BlockSpec(overlap.shape, lambda b, i: (0, 0))],
        out_specs=[pl.BlockSpec((tq, NSA_WIDTH), lambda b, i: (b * nt + i, 0)),
                   pl.BlockSpec((NSA_KV_HEADS, tq, nsp), lambda b, i: (0, b * nt + i, 0))],
        out_shape=[jax.ShapeDtypeStruct((m, NSA_WIDTH), F32),
                   jax.ShapeDtypeStruct((NSA_KV_HEADS, m, nsp), F32)],
        compiler_params=_cparams(("parallel", "arbitrary")),
        name="cmp_attn_topk",
    )(qn, kc, vc, overlap)


def _overlap_matrix(nb, nsp, n_cmp, n_slc):
    cs = np.arange(nb) * CMP_STRIDE
    ss = np.arange(nsp) * SLC_BLOCK
    lo = np.maximum(cs[:, None], ss[None, :])
    hi = np.minimum(cs[:, None] + CMP_BLOCK, ss[None, :] + SLC_BLOCK)
    ov = (np.maximum(hi - lo, 0) / CMP_BLOCK).astype(np.float32)
    ov[n_cmp:, :] = 0.0
    ov[:, n_slc:] = 0.0
    return jnp.asarray(ov)


def _flash_init(m_ref, l_ref, acc_ref):
    m_ref[...] = jnp.full(m_ref.shape, NEG, F32)
    l_ref[...] = jnp.zeros(l_ref.shape, F32)
    acc_ref[...] = jnp.zeros(acc_ref.shape, F32)


def _lane_rep(col, width):
    tile = jnp.broadcast_to(col, (col.shape[0], LANES))
    return tile if width == LANES else jnp.concatenate([tile] * (width // LANES), axis=1)


def _flash_update(s, mask, v, m_ref, l_ref, acc_ref):
    rows, tk = s.shape
    reps = tk // LANES
    s = jnp.where(mask, s, NEG)
    m_prev = m_ref[...]
    m_new = jnp.maximum(m_prev, jnp.broadcast_to(jnp.max(s, axis=-1, keepdims=True), (rows, LANES)))
    alpha = jnp.exp(m_prev - m_new)
    p = jnp.exp(s - jnp.concatenate([m_new] * reps, axis=1))
    l_ref[...] = alpha * l_ref[...] + jnp.broadcast_to(jnp.sum(p, axis=-1, keepdims=True), (rows, LANES))
    acc_ref[...] = alpha * acc_ref[...] + _dot(p.astype(v.dtype), v)
    m_ref[...] = m_new


def _pattn_kernel(q_ref, ks_ref, vs_ref, kw_ref, vw_ref, ocmp_ref, sel_ref, gate_ref, exp_ref, o_ref,
                  sexp_ref, m_ref, l_ref, acc_ref, *, tq, tk, wlen):
    i = pl.program_id(2)
    t = ks_ref.shape[0]
    q4 = jnp.concatenate([(q_ref[:, r * HEAD_DIM:(r + 1) * HEAD_DIM] * SCALE).astype(BF16)
                          for r in range(NSA_GROUP)], axis=0)
    sexp_ref[...] = _dot(sel_ref[0].astype(BF16), exp_ref[...])
    qpos = i * tq + lax.broadcasted_iota(jnp.int32, (tq, tk), 0)
    lane = lax.broadcasted_iota(jnp.int32, (tq, tk), 1)

    def tile4(x):
        return jnp.concatenate([x] * NSA_GROUP, axis=0)

    def slc_step(j, carry):
        start = pl.multiple_of(j * tk, tk)
        s = _dot_nt(q4, ks_ref[pl.ds(start, tk), :])
        mask = (sexp_ref[:, pl.ds(start, tk)] > 0.5) & (start + lane <= qpos)
        _flash_update(s, tile4(mask), vs_ref[pl.ds(start, tk), :], m_ref, l_ref, acc_ref)
        return carry

    _flash_init(m_ref, l_ref, acc_ref)
    lax.fori_loop(0, ((i + 1) * tq + tk - 1) // tk, slc_step, 0)
    o_slc = acc_ref[...] / l_ref[...]

    wstart = pl.multiple_of(jnp.clip((i + 1) * tq - wlen, 0, t - wlen), tq)
    s = _dot_nt(q4, kw_ref[pl.ds(wstart, wlen), :])
    wdiff = (i * tq + lax.broadcasted_iota(jnp.int32, (tq, wlen), 0)
             - (wstart + lax.broadcasted_iota(jnp.int32, (tq, wlen), 1)))
    s = jnp.where(tile4((wdiff >= 0) & (wdiff < WINDOW)), s, NEG)
    p = jnp.exp(s - _lane_rep(jnp.max(s, axis=-1, keepdims=True), wlen))
    o_win = _dot(p.astype(BF16), vw_ref[pl.ds(wstart, wlen), :]) / _lane_rep(jnp.sum(p, axis=-1, keepdims=True), HEAD_DIM)

    gates = gate_ref[0]
    for r in range(NSA_GROUP):
        sl = slice(r * tq, (r + 1) * tq)
        col = slice(r * HEAD_DIM, (r + 1) * HEAD_DIM)
        o = (gates[:, 3 * r:3 * r + 1] * ocmp_ref[:, col]
             + gates[:, 3 * r + 1:3 * r + 2] * o_slc[sl]
             + gates[:, 3 * r + 2:3 * r + 3] * o_win[sl])
        o_ref[:, col] = o.astype(o_ref.dtype)


def _prompt_attn(qn, slc_bf, win_bf, ocmp, sel, gates, *, bsz, t, tq):
    m = qn.shape[0]
    nt = t // tq
    gw = NSA_GROUP * HEAD_DIM
    half = NSA_KV_HEADS
    nsp = sel.shape[2]
    n_slc = -(-t // SLC_BLOCK)
    expand = np.zeros((nsp, t), np.float32)
    expand[np.arange(t) // SLC_BLOCK, np.arange(t)] = 1.0
    assert n_slc <= nsp
    tk = next(c for c in (512, 256, 128) if t % c == 0 and c >= tq)
    wlen = min(t, WINDOW + tq)
    assert wlen % tq == 0 and t % tq == 0
    kv_spec = lambda which: pl.BlockSpec((t, HEAD_DIM), lambda b, g, i: (b, which * half + g))
    return pl.pallas_call(
        functools.partial(_pattn_kernel, tq=tq, tk=tk, wlen=wlen),
        grid=(bsz, NSA_KV_HEADS, nt),
        in_specs=[pl.BlockSpec((tq, gw), lambda b, g, i: (b * nt + i, g)),
                  kv_spec(0), kv_spec(1), kv_spec(0), kv_spec(1),
                  pl.BlockSpec((tq, gw), lambda b, g, i: (b * nt + i, g)),
                  pl.BlockSpec((1, tq, nsp), lambda b, g, i: (g, b * nt + i, 0)),
                  pl.BlockSpec((1, tq, LANES), lambda b, g, i: (g, b * nt + i, 0)),
                  pl.BlockSpec((nsp, t), lambda b, g, i: (0, 0))],
        out_specs=pl.BlockSpec((tq, gw), lambda b, g, i: (b * nt + i, g)),
        out_shape=jax.ShapeDtypeStruct((m, NSA_WIDTH), BF16),
        scratch_shapes=[pltpu.VMEM((tq, t), F32),
                        pltpu.VMEM((NSA_GROUP * tq, LANES), F32),
                        pltpu.VMEM((NSA_GROUP * tq, LANES), F32),
                        pltpu.VMEM((NSA_GROUP * tq, HEAD_DIM), F32)],
        compiler_params=_cparams(("parallel", "parallel", "arbitrary")),
        name="nsa_prompt_attn",
    )(qn, slc_bf, slc_bf, win_bf, win_bf, ocmp, sel, gates, jnp.asarray(expand, BF16))


def _sattn_kernel(tbl_ref, q_ref, *refs, tq, n_steps, pg, past):
    del tbl_ref
    page_refs = refs[:pg]
    (snew_ref, wcache_ref, wnew_ref, ocmp_ref, sel_ref, gate_ref, exp_ref, o_ref, s_buf, v_buf) = refs[pg:]
    j = pl.program_id(1)
    rows = NSA_GROUP * tq
    half = NSA_KV_HEADS * HEAD_DIM
    nsp = sel_ref.shape[2]
    n_prev = wcache_ref.shape[0] // KV_SLABS

    def tile4(x):
        return jnp.concatenate([x] * NSA_GROUP, axis=0)

    q4_bf = [_stack_heads(q_ref, g, tq, BF16) for g in range(NSA_KV_HEADS)]
    for u in range(pg):
        key0 = pl.multiple_of((j * pg + u) * PAGE_SIZE, PAGE_SIZE)
        for g in range(NSA_KV_HEADS):
            k = page_refs[u][pl.ds(g, PAGE_SIZE, stride=KV_SLABS), :].astype(BF16)
            s_buf[g * rows:(g + 1) * rows, pl.ds(key0, PAGE_SIZE)] = _dot_nt(q4_bf[g], k)
            v_buf[g, pl.ds(key0, PAGE_SIZE), :] = (
                page_refs[u][pl.ds(NSA_KV_HEADS + g, PAGE_SIZE, stride=KV_SLABS), :].astype(BF16))

    @pl.when(j == n_steps - 1)
    def _():
        blk_lane = lax.broadcasted_iota(jnp.int32, (tq, nsp), 1)
        trow = lax.broadcasted_iota(jnp.int32, (tq, tq), 0)
        tcol = lax.broadcasted_iota(jnp.int32, (tq, tq), 1)
        causal = tcol <= trow
        wrow = lax.broadcasted_iota(jnp.int32, (tq, n_prev), 0)
        wcol = lax.broadcasted_iota(jnp.int32, (tq, n_prev), 1)
        wdiff = wrow + n_prev - wcol
        wmask = (wdiff >= 0) & (wdiff < WINDOW)
        ndiff = trow - tcol
        nmask = (ndiff >= 0) & (ndiff < WINDOW)
        gates_all = gate_ref[...]
        for g in range(NSA_KV_HEADS):
            q4 = _stack_heads(q_ref, g, tq, F32)
            selg = sel_ref[g]
            sexp = _dot(selg.astype(BF16), exp_ref[...])
            sc = jnp.where(tile4(sexp > 0.5), s_buf[g * rows:(g + 1) * rows, :] * SCALE, NEG)
            kn = snew_ref[pl.ds(g, tq, stride=KV_SLABS), :]
            vn = snew_ref[pl.ds(NSA_KV_HEADS + g, tq, stride=KV_SLABS), :]
            flag = jnp.zeros((tq, tq), jnp.bool_)
            for u in range(tq):
                col = jnp.sum(jnp.where(blk_lane == (past + u) // SLC_BLOCK, selg, 0.0), axis=1, keepdims=True) > 0.5
                flag = flag | (col & (tcol == u))
            sn = jnp.where(tile4(flag & causal), _dot_nt(q4, kn, HI) * SCALE, NEG)
            mx = jnp.maximum(jnp.max(sc, axis=-1, keepdims=True), jnp.max(sn, axis=-1, keepdims=True))
            pc = jnp.exp(sc - mx)
            pn = jnp.exp(sn - mx)
            den = jnp.sum(pc, axis=-1, keepdims=True) + jnp.sum(pn, axis=-1, keepdims=True)
            o_slc = (_dot(pc.astype(BF16), v_buf[g]) + _dot(pn, vn, HI)) / den

            kc = wcache_ref[pl.ds(g, n_prev, stride=KV_SLABS), :].astype(BF16)
            vc = wcache_ref[pl.ds(NSA_KV_HEADS + g, n_prev, stride=KV_SLABS), :].astype(BF16)
            kw = wnew_ref[pl.ds(g, tq, stride=KV_SLABS), :]
            vw = wnew_ref[pl.ds(NSA_KV_HEADS + g, tq, stride=KV_SLABS), :]
            sc = jnp.where(tile4(wmask), _dot_nt(q4.astype(BF16), kc) * SCALE, NEG)
            sn = jnp.where(tile4(nmask), _dot_nt(q4, kw, HI) * SCALE, NEG)
            mx = jnp.maximum(jnp.max(sc, axis=-1, keepdims=True), jnp.max(sn, axis=-1, keepdims=True))
            pc = jnp.exp(sc - mx)
            pn = jnp.exp(sn - mx)
            den = jnp.sum(pc, axis=-1, keepdims=True) + jnp.sum(pn, axis=-1, keepdims=True)
            o_win = (_dot(pc.astype(BF16), vc) + _dot(pn, vw, HI)) / den

            gates = gates_all[g]
            for r in range(NSA_GROUP):
                sl = slice(r * tq, (r + 1) * tq)
                col = slice((g * NSA_GROUP + r) * HEAD_DIM, (g * NSA_GROUP + r + 1) * HEAD_DIM)
                o = (gates[:, 3 * r:3 * r + 1] * ocmp_ref[:, col]
                     + gates[:, 3 * r + 1:3 * r + 2] * o_slc[sl]
                     + gates[:, 3 * r + 2:3 * r + 3] * o_win[sl])
                o_ref[:, col] = o.astype(o_ref.dtype)


def _sample_attn(qn, cache2d, page_table, slc_new, win_cache2d, win_row0, win_new, ocmp, sel, gates,
                 *, bsz, tq, past, n_prev):
    n_pages = page_table.shape[1]
    pg = math.gcd(n_pages, 8)
    n_steps = n_pages // pg
    nsp = sel.shape[2]
    rows = NSA_GROUP * tq
    page_rows = PAGE_SIZE * KV_SLABS
    expand = np.zeros((nsp, past), np.float32)
    expand[np.arange(past) // SLC_BLOCK, np.arange(past)] = 1.0

    def page_spec(u):
        return pl.BlockSpec((page_rows, HEAD_DIM), lambda b, j, tbl: (tbl[b, j * pg + u], 0))

    grid_spec = pltpu.PrefetchScalarGridSpec(
        num_scalar_prefetch=1,
        grid=(bsz, n_steps),
        in_specs=[pl.BlockSpec((tq, NSA_WIDTH), lambda b, j, tbl: (b, 0))]
        + [page_spec(u) for u in range(pg)]
        + [pl.BlockSpec((tq * KV_SLABS, HEAD_DIM), lambda b, j, tbl: (b, 0)),
           pl.BlockSpec((n_prev * KV_SLABS, HEAD_DIM), lambda b, j, tbl: (win_row0 + b, 0)),
           pl.BlockSpec((tq * KV_SLABS, HEAD_DIM), lambda b, j, tbl: (b, 0)),
           pl.BlockSpec((tq, NSA_WIDTH), lambda b, j, tbl: (b, 0)),
           pl.BlockSpec((NSA_KV_HEADS, tq, nsp), lambda b, j, tbl: (0, b, 0)),
           pl.BlockSpec((NSA_KV_HEADS, tq, LANES), lambda b, j, tbl: (0, b, 0)),
           pl.BlockSpec((nsp, past), lambda b, j, tbl: (0, 0))],
        out_specs=pl.BlockSpec((tq, NSA_WIDTH), lambda b, j, tbl: (b, 0)),
        scratch_shapes=[pltpu.VMEM((NSA_KV_HEADS * rows, past), F32),
                        pltpu.VMEM((NSA_KV_HEADS, past, HEAD_DIM), BF16)],
    )
    return pl.pallas_call(
        functools.partial(_sattn_kernel, tq=tq, n_steps=n_steps, pg=pg, past=past),
        grid_spec=grid_spec,
        out_shape=jax.ShapeDtypeStruct((bsz * tq, NSA_WIDTH), BF16),
        compiler_params=_cparams(("parallel", "arbitrary")),
        name="nsa_sample_attn",
    )(page_table, qn, *([cache2d] * pg), slc_new, win_cache2d, win_new, ocmp, sel, gates,
      jnp.asarray(expand, BF16))


def _split_w_in(w_in):
    wt = w_in.T
    pad = jnp.zeros((LANES - 2 * GDN_HEADS - 3 * NSA_HEADS, wt.shape[1]), w_in.dtype)
    wt_small = jnp.concatenate([wt[W_IN_B0:W_IN_N0], wt[W_IN_G0:], pad], axis=0)
    return wt, wt[W_IN_N0:W_IN_G0], wt_small


def _in_proj(x2d, p, *, tm):
    xn = _rms_cast(x2d, p["attn_norm_g"], tm=min(tm, 512))
    w_full, w_b, w_small = p["w_in"]
    proj_a = _mm_nw([xn], w_full, n=PROJ_A_WIDTH, tm=tm, tn=512, w_t=True)
    proj_b = _mm_nw([xn], w_b, n=PROJ_B_WIDTH, tm=tm, tn=512, w_t=True)
    small = _mm_nw([xn], w_small, n=LANES, tm=tm, tn=LANES, w_t=True)
    return proj_a, proj_b, small


def _pos_weights(cmp_pos_w):
    w = cmp_pos_w.reshape(2, CMP_BLOCK // CMP_STRIDE, CMP_STRIDE, NSA_KV_HEADS)
    w = jnp.transpose(w, (1, 2, 0, 3))
    return jnp.broadcast_to(w[..., None], w.shape + (HEAD_DIM,)).reshape(2, CMP_STRIDE, KV_SLABS, HEAD_DIM)


def _mixers_out(x2d, o_gdn, o_nsa, w_o, mlp_norm_g, w_up, w_down, *, tm):
    x1 = _mm_nw([o_gdn, o_nsa], w_o, n=w_o.shape[1], tm=tm, tn=512, res=x2d)
    xn = _rms_cast(x1, mlp_norm_g, tm=min(tm, 512))
    hid = _mm_nw([xn], w_up, n=w_up.shape[1], tm=tm, tn=512, relu2=True, out_dtype=BF16)
    return _mm_res(hid, w_down, x1, tm=tm, tn=min(1024, w_down.shape[1]), tk=min(2048, w_down.shape[0]))


def _round_up(x, n):
    return -(-x // n) * n


def _prompt_layer(x, p):
    bsz, t, d = x.shape
    m = bsz * t
    x2d = x.reshape(m, d)
    proj_a, proj_b, small = _in_proj(x2d, p, tm=1024)
    conv0 = jnp.zeros((bsz, CONV_W - 1, 3 * GDN_WIDTH), F32)
    s_zero = jnp.zeros((bsz, GDN_HEADS, HEAD_DIM, HEAD_DIM), F32)
    o_gdn, s_new = _gdn(proj_a, small, conv0, s_zero, p["gdn_conv_w"], p["gdn_a_log"], p["gdn_dt_bias"],
                        p["gdn_norm_g"], bsz=bsz, rows_per_seq=t, t_valid=t)
    qn, cmp_n, slc_n, win_n, gates, slc_bf, win_bf = _nsa_prep(proj_b, small, p["q_norm_g"], p["k_norm_g"], tm=512)

    n_cmp = (t - CMP_BLOCK) // CMP_STRIDE + 1
    n_slc = -(-t // SLC_BLOCK)
    n_pages = t // PAGE_SIZE
    table = jnp.arange(bsz * n_pages, dtype=jnp.int32).reshape(bsz, n_pages)
    kv_tail = (2, NSA_KV_HEADS, HEAD_DIM)
    s0, s1 = _pool(cmp_n, table, p["cmp_w"])
    kc, vc = _cmp_finish(s0, s1, p["cmp_phi"], p["k_norm_g"][0], n_cmp=n_cmp)
    nb = kc.shape[2]
    nsp = _round_up(n_slc, LANES)
    tq = 128
    ocmp, sel = _cmp_topk(qn, kc, vc, _overlap_matrix(nb, nsp, n_cmp, n_slc), bsz=bsz, rows_per_seq=t, tq=tq,
                          n_cmp=n_cmp, n_slc=n_slc, offset=0)
    o_nsa = _prompt_attn(qn, slc_bf, win_bf, ocmp, sel, gates, bsz=bsz, t=t, tq=tq)
    y = _mixers_out(x2d, o_gdn, o_nsa, p["w_o"], p["mlp_norm_g"], p["w_up"], p["w_down"], tm=1024)

    keep = min(WINDOW, t)
    return (y.reshape(bsz, t, d),
            cmp_n.reshape((bsz, t) + kv_tail),
            slc_n.reshape((bsz, t) + kv_tail),
            win_n.reshape((bsz, t) + kv_tail)[:, t - keep:],
            proj_a.reshape(bsz, t, PROJ_A_WIDTH)[:, t - (CONV_W - 1):, 0:3 * GDN_WIDTH],
            s_new)


def _sample_layer(x, layer, cache_cmp_kv, cache_slc_kv, cache_win_kv, conv_buf, s0_state, page_table, p):
    bsz, t, d = x.shape
    tp = _round_up(t, SUBLANES)
    n_pages = page_table.shape[1]
    past = n_pages * PAGE_SIZE
    assert t <= SLC_BLOCK and past % SLC_BLOCK == 0 and t >= CONV_W - 1
    x2d = jnp.pad(x, ((0, 0), (0, tp - t), (0, 0))).reshape(bsz * tp, d)
    proj_a, proj_b, small = _in_proj(x2d, p, tm=bsz * tp)
    o_gdn, s_new = _gdn(proj_a, small, conv_buf, s0_state, p["gdn_conv_w"], p["gdn_a_log"], p["gdn_dt_bias"],
                        p["gdn_norm_g"], bsz=bsz, rows_per_seq=tp, t_valid=t)
    qn, cmp_n, slc_n, win_n, gates, _, _ = _nsa_prep(proj_b, small, p["q_norm_g"], p["k_norm_g"], tm=bsz * tp)

    total = past + t
    n_cmp = (total - CMP_BLOCK) // CMP_STRIDE + 1
    n_slc = -(-total // SLC_BLOCK)
    assert (n_cmp - 1) * CMP_STRIDE + CMP_BLOCK <= past
    depth, n_phys = cache_cmp_kv.shape[:2]
    table = page_table + layer * n_phys
    cache_rows = depth * n_phys * PAGE_SIZE * KV_SLABS
    s0, s1 = _pool(cache_cmp_kv.reshape(cache_rows, HEAD_DIM), table, p["cmp_w"])
    kc, vc = _cmp_finish(s0, s1, p["cmp_phi"], p["k_norm_g"][0], n_cmp=n_cmp)
    nb = kc.shape[2]
    nsp = _round_up(n_slc, LANES)
    ocmp, sel = _cmp_topk(qn, kc, vc, _overlap_matrix(nb, nsp, n_cmp, n_slc), bsz=bsz, rows_per_seq=tp, tq=tp,
                          n_cmp=n_cmp, n_slc=n_slc, offset=past)
    n_prev = cache_win_kv.shape[2]
    o_nsa = _sample_attn(qn, cache_slc_kv.reshape(cache_rows, HEAD_DIM), table, slc_n,
                         cache_win_kv.reshape(depth * bsz * n_prev * KV_SLABS, HEAD_DIM), layer * bsz,
                         win_n, ocmp, sel, gates, bsz=bsz, tq=tp, past=past, n_prev=n_prev)
    y = _mixers_out(x2d, o_gdn, o_nsa, p["w_o"], p["mlp_norm_g"], p["w_up"], p["w_down"], tm=bsz * tp)

    kv_tail = (2, NSA_KV_HEADS, HEAD_DIM)
    win_all = jnp.concatenate([cache_win_kv[layer], win_n.reshape((bsz, tp) + kv_tail)[:, :t]], axis=1)
    keep = min(WINDOW, n_prev + t)
    return (y.reshape(bsz, tp, d)[:, :t],
            cmp_n.reshape((bsz, tp) + kv_tail)[:, :t],
            slc_n.reshape((bsz, tp) + kv_tail)[:, :t],
            win_all[:, n_prev + t - keep:],
            proj_a.reshape(bsz, tp, PROJ_A_WIDTH)[:, t - (CONV_W - 1):t, 0:3 * GDN_WIDTH],
            s_new)


def kernel(x_prompt, x_sample, cache_cmp_kv, cache_slc_kv, cache_win_kv, cache_gdn_conv, state_gdn, page_table, attn_norm_g, w_in, gdn_conv_w, gdn_a_log, gdn_dt_bias, gdn_norm_g, q_norm_g, k_norm_g, cmp_pos_w, cmp_phi, w_o, mlp_norm_g, w_up, w_down):
    depth = w_in.shape[0]
    yp, ys = x_prompt, x_sample
    per_layer = []
    for layer in range(depth):
        p = {
            "attn_norm_g": attn_norm_g[layer], "w_in": _split_w_in(w_in[layer]),
            "gdn_conv_w": gdn_conv_w[layer], "gdn_a_log": gdn_a_log[layer], "gdn_dt_bias": gdn_dt_bias[layer],
            "gdn_norm_g": gdn_norm_g[layer], "q_norm_g": q_norm_g[layer], "k_norm_g": k_norm_g[layer],
            "cmp_w": _pos_weights(cmp_pos_w[layer]), "cmp_phi": cmp_phi[layer],
            "w_o": w_o[layer], "mlp_norm_g": mlp_norm_g[layer],
            "w_up": w_up[layer], "w_down": w_down[layer].astype(BF16),
        }
        yp, cmp_p, slc_p, win_p, conv_p, s_p = _prompt_layer(yp, p)
        ys, cmp_s, slc_s, win_s, conv_s, s_s = _sample_layer(
            ys, layer, cache_cmp_kv, cache_slc_kv, cache_win_kv, cache_gdn_conv[layer],
            state_gdn[layer], page_table, p)
        per_layer.append((cmp_p, cmp_s, slc_p, slc_s, win_p, win_s, conv_p, conv_s, s_p, s_s))
    st = [jnp.stack(z, axis=0) for z in zip(*per_layer)]
    return (yp, ys) + tuple(st)
```

```python
import functools
import math

import numpy as np
import jax
import jax.numpy as jnp
from jax import lax
from jax.experimental import pallas as pl
from jax.experimental.pallas import tpu as pltpu

F32 = jnp.float32
BF16 = jnp.bfloat16
HI = lax.Precision.HIGHEST

HEAD_DIM = 128
GDN_HEADS = 16
NSA_HEADS = 16
NSA_KV_HEADS = 4
NSA_GROUP = NSA_HEADS // NSA_KV_HEADS
GDN_WIDTH = GDN_HEADS * HEAD_DIM
NSA_WIDTH = NSA_HEADS * HEAD_DIM
KV_WIDTH = 2 * NSA_KV_HEADS * HEAD_DIM
KV_SLABS = 2 * NSA_KV_HEADS
CONV_W = 4
GDN_CHUNK = 64
GDN_HEAD_GROUP = 8
CMP_BLOCK = 32
CMP_STRIDE = 16
SLC_BLOCK = 64
SLC_TOPK = 16
SLC_LOCAL = 2
WINDOW = 512
PAGE_SIZE = 128
EPS = 1e-6
FORCE_SCORE = 1e9
NEG = -1e30
SCALE = HEAD_DIM ** -0.5

LANES = 128
SUBLANES = 8
VMEM_LIMIT = 52 * 1024 * 1024

COL_Q, COL_K, COL_V, COL_Z = 0, GDN_WIDTH, 2 * GDN_WIDTH, 3 * GDN_WIDTH
PROJ_A_WIDTH = 4 * GDN_WIDTH
COL_NQ = 0
COL_CMP = COL_NQ + NSA_WIDTH
COL_SLC = COL_CMP + KV_WIDTH
COL_WIN = COL_SLC + KV_WIDTH
PROJ_B_WIDTH = COL_WIN + KV_WIDTH
SMALL_B, SMALL_A, SMALL_GATE = 0, GDN_HEADS, 2 * GDN_HEADS
W_IN_B0 = PROJ_A_WIDTH
W_IN_N0 = W_IN_B0 + 2 * GDN_HEADS
W_IN_G0 = W_IN_N0 + PROJ_B_WIDTH


def _cparams(sem):
    return pltpu.CompilerParams(dimension_semantics=sem, vmem_limit_bytes=VMEM_LIMIT)


def _dot(a, b, prec=None):
    return jnp.dot(a, b, preferred_element_type=F32, precision=prec)


def _dot_nt(a, b, prec=None):
    return lax.dot_general(a, b, (((1,), (1,)), ((), ())), preferred_element_type=F32, precision=prec)


def _dot_tn(a, b, prec=None):
    return lax.dot_general(a, b, (((0,), (0,)), ((), ())), preferred_element_type=F32, precision=prec)


def _rms(x, g):
    return x * lax.rsqrt(jnp.mean(x * x, axis=-1, keepdims=True) + EPS) * g


def _silu(x):
    return x * jax.nn.sigmoid(x)


def _softplus(x):
    return jnp.maximum(x, 0.0) + jnp.log1p(jnp.exp(-jnp.abs(x)))


def _rms_cast_kernel(x_ref, g_ref, o_ref):
    o_ref[...] = _rms(x_ref[...], g_ref[...]).astype(o_ref.dtype)


def _rms_cast(x, g, *, tm):
    m, k = x.shape
    assert m % tm == 0
    return pl.pallas_call(
        _rms_cast_kernel,
        grid=(m // tm,),
        in_specs=[pl.BlockSpec((tm, k), lambda i: (i, 0)), pl.BlockSpec((1, k), lambda i: (0, 0))],
        out_specs=pl.BlockSpec((tm, k), lambda i: (i, 0)),
        out_shape=jax.ShapeDtypeStruct((m, k), BF16),
        compiler_params=_cparams(("parallel",)),
        name="rms_cast",
    )(x, g.reshape(1, k))


def _mm_nw_kernel(*refs, n_a, relu2, has_res, w_t, has_side):
    a_refs = refs[:n_a]
    w_ref = refs[n_a]
    r_ref = refs[n_a + 1] if has_res else None
    if has_side:
        side_ref, o_ref, side_out, wbf_ref = refs[-4:]
        side_out[...] = side_ref[...].astype(BF16)
    else:
        o_ref, wbf_ref = refs[-2:]

    @pl.when(pl.program_id(1) == 0)
    def _():
        w = w_ref[...]
        wbf_ref[...] = (w.T if w_t else w).astype(BF16)

    y, k0 = None, 0
    for a_ref in a_refs:
        part = _dot(a_ref[...], wbf_ref[k0:k0 + a_ref.shape[1], :])
        y = part if y is None else y + part
        k0 += a_ref.shape[1]
    if relu2:
        y = jnp.square(jnp.maximum(y, 0.0))
    if has_res:
        y = y + r_ref[...]
    o_ref[...] = y.astype(o_ref.dtype)


def _mm_nw(acts, w, *, n, tm, tn, res=None, relu2=False, out_dtype=F32, w_t=False, cast_side=None):
    m = acts[0].shape[0]
    k = sum(a.shape[1] for a in acts)
    assert m % tm == 0 and n % tn == 0 and w.shape[1 if w_t else 0] == k
    nm = m // tm
    in_specs = [pl.BlockSpec((tm, a.shape[1]), lambda j, i: (i, 0)) for a in acts]
    in_specs.append(pl.BlockSpec((tn, k), lambda j, i: (j, 0)) if w_t else pl.BlockSpec((k, tn), lambda j, i: (0, j)))
    args = list(acts) + [w]
    if res is not None:
        in_specs.append(pl.BlockSpec((tm, tn), lambda j, i: (i, j)))
        args.append(res)
    out_specs = pl.BlockSpec((tm, tn), lambda j, i: (i, j))
    out_shape = jax.ShapeDtypeStruct((m, n), out_dtype)
    if cast_side is not None:
        steps = (n // tn) * nm
        sr, sc = cast_side.shape
        assert sr % steps == 0 and (sr // steps) % 16 == 0
        side_spec = pl.BlockSpec((sr // steps, sc), lambda j, i: (j * nm + i, 0))
        in_specs.append(side_spec)
        args.append(cast_side)
        out_specs = [out_specs, side_spec]
        out_shape = [out_shape, jax.ShapeDtypeStruct((sr, sc), BF16)]
    return pl.pallas_call(
        functools.partial(_mm_nw_kernel, n_a=len(acts), relu2=relu2, has_res=res is not None, w_t=w_t,
                          has_side=cast_side is not None),
        grid=(n // tn, nm),
        in_specs=in_specs,
        out_specs=out_specs,
        out_shape=out_shape,
        scratch_shapes=[pltpu.VMEM((k, tn), BF16)],
        compiler_params=_cparams(("parallel", "arbitrary")),
        name="matmul_wcast",
    )(*args)


def _mm_res_kernel(a_ref, w_ref, r_ref, o_ref, acc_ref, *, nk):
    kk = pl.program_id(2)

    @pl.when(kk == 0)
    def _():
        acc_ref[...] = jnp.zeros_like(acc_ref)

    acc_ref[...] += _dot(a_ref[...], w_ref[...])

    @pl.when(kk == nk - 1)
    def _():
        o_ref[...] = r_ref[...] + acc_ref[...]


def _mm_res(a, w, res, *, tm, tn, tk):
    m, k = a.shape
    n = w.shape[1]
    assert m % tm == 0 and n % tn == 0 and k % tk == 0
    nk = k // tk
    return pl.pallas_call(
        functools.partial(_mm_res_kernel, nk=nk),
        grid=(m // tm, n // tn, nk),
        in_specs=[pl.BlockSpec((tm, tk), lambda i, j, kk: (i, kk)),
                  pl.BlockSpec((tk, tn), lambda i, j, kk: (kk, j)),
                  pl.BlockSpec((tm, tn), lambda i, j, kk: (i, j))],
        out_specs=pl.BlockSpec((tm, tn), lambda i, j, kk: (i, j)),
        out_shape=jax.ShapeDtypeStruct((m, n), F32),
        scratch_shapes=[pltpu.VMEM((tm, tn), F32)],
        compiler_params=_cparams(("parallel", "parallel", "arbitrary")),
        name="matmul_residual",
    )(a, w, res)


def _gdn_kernel(q_ref, k_ref, v_ref, z_ref, sm_ref, cbuf_ref, s0_ref, cw_ref, alog_ref, dt_ref, ng_ref,
                o_ref, sout_ref, xp_ref, st_ref, *, rows, nc, t_valid):
    C = GDN_CHUNK
    W = GDN_WIDTH
    c = pl.program_id(1)

    @pl.when(c == 0)
    def _():
        xp_ref[...] = jnp.zeros(xp_ref.shape, F32)
        xp_ref[SUBLANES - (CONV_W - 1):SUBLANES, :] = cbuf_ref[0]
        st_ref[...] = s0_ref[0]

    @pl.when(c > 0)
    def _():
        xp_ref[0:SUBLANES, :] = xp_ref[C:C + SUBLANES, :]

    xp_ref[SUBLANES:SUBLANES + rows, 0:W] = q_ref[...]
    xp_ref[SUBLANES:SUBLANES + rows, W:2 * W] = k_ref[...]
    xp_ref[SUBLANES:SUBLANES + rows, 2 * W:3 * W] = v_ref[...]

    row = lax.broadcasted_iota(jnp.int32, (C, LANES), 0)
    lane = lax.broadcasted_iota(jnp.int32, (C, LANES), 1)
    valid = (c * C + row) < t_valid
    if rows < C:
        sm = jnp.concatenate([sm_ref[...], jnp.zeros((C - rows, LANES), F32)], axis=0)
    else:
        sm = sm_ref[...]
    beta_all = jnp.where(valid, jax.nn.sigmoid(sm), 0.0)
    g_all = jnp.where(valid, -jnp.exp(alog_ref[...]) * _softplus(sm + dt_ref[...]), 0.0)
    ri = lax.broadcasted_iota(jnp.int32, (C, C), 0)
    ci = lax.broadcasted_iota(jnp.int32, (C, C), 1)
    tri = ri >= ci
    strict = ri > ci
    gcum = _dot(tri.astype(F32), g_all, HI)
    li = lax.broadcasted_iota(jnp.int32, (LANES, LANES), 0)
    lj = lax.broadcasted_iota(jnp.int32, (LANES, LANES), 1)
    gcum_t = _dot_nt((li == lj).astype(F32), gcum, HI)
    valid_col = valid[:, 0:1]
    eye = (ri == ci).astype(F32)
    n_sq = int(math.log2(C)) - 1

    def mm(a, b):
        return _dot(a.astype(BF16), b.astype(BF16))

    def mm_nt(a, b):
        return _dot_nt(a.astype(BF16), b.astype(BF16))

    def conv(col):
        acc = None
        for j in range(CONV_W):
            term = (xp_ref[pl.ds(SUBLANES - (CONV_W - 1) + j, C), col:col + LANES]
                    * cw_ref[j:j + 1, col:col + LANES])
            acc = term if acc is None else acc + term
        return _silu(acc)

    def head_group(hs):
        n = len(hs)
        qs = [conv(h * HEAD_DIM) for h in hs]
        ks = [conv(W + h * HEAD_DIM) for h in hs]
        vs = [conv(2 * W + h * HEAD_DIM) for h in hs]
        qs = [q * lax.rsqrt(jnp.sum(q * q, axis=-1, keepdims=True) + EPS) * SCALE for q in qs]
        ks = [jnp.where(valid_col, k * lax.rsqrt(jnp.sum(k * k, axis=-1, keepdims=True) + EPS), 0.0) for k in ks]
        gcs = [gcum[:, SMALL_A + h:SMALL_A + h + 1] for h in hs]
        bhs = [beta_all[:, SMALL_B + h:SMALL_B + h + 1] for h in hs]
        grs = [gcum_t[SMALL_A + h:SMALL_A + h + 1, :] for h in hs]
        decays = [jnp.where(tri, jnp.exp(jnp.where(tri, gc - gr, 0.0)), 0.0) for gc, gr in zip(gcs, grs)]
        egs = [jnp.exp(gc) for gc in gcs]

        kbs = [k * b for k, b in zip(ks, bhs)]
        kqs = [mm_nt(jnp.concatenate([kb, q], axis=0), k) for kb, q, k in zip(kbs, qs, ks)]
        nmats = [jnp.where(strict, -(kq[0:C] * d), 0.0) for kq, d in zip(kqs, decays)]
        aqks = [jnp.where(tri, kq[C:2 * C] * d, 0.0) for kq, d in zip(kqs, decays)]
        pinvs = [eye + nm for nm in nmats]
        npows = nmats
        for _ in range(n_sq):
            npows = [mm(np_, np_) for np_ in npows]
            pinvs = [p + mm(p, np_) for p, np_ in zip(pinvs, npows)]
        uws = [mm(p, jnp.concatenate([v * b, kb * eg], axis=1))
               for p, v, b, kb, eg in zip(pinvs, vs, bhs, kbs, egs)]

        ss = [st_ref[h] for h in hs]
        wss = [mm(jnp.concatenate([uw[:, HEAD_DIM:2 * HEAD_DIM], q * eg], axis=0), s)
               for uw, q, eg, s in zip(uws, qs, egs, ss)]
        v_news = [uw[:, 0:HEAD_DIM] - ws[0:C] for uw, ws in zip(uws, wss)]
        os_ = [ws[C:2 * C] + mm(aqk, vn) for ws, aqk, vn in zip(wss, aqks, v_news)]
        for i in range(n):
            g_last = gcs[i][C - 1:C, :]
            k_dec = ks[i] * jnp.exp(g_last - gcs[i])
            st_ref[hs[i]] = ss[i] * jnp.exp(g_last) + _dot_tn(k_dec.astype(BF16), v_news[i].astype(BF16))
        for i in range(n):
            col = hs[i] * HEAD_DIM
            o = _rms(os_[i], ng_ref[...])
            o_ref[:, col:col + HEAD_DIM] = (o[0:rows] * _silu(z_ref[:, col:col + HEAD_DIM])).astype(o_ref.dtype)

    for h0 in range(0, GDN_HEADS, GDN_HEAD_GROUP):
        head_group(list(range(h0, h0 + GDN_HEAD_GROUP)))

    @pl.when(c == nc - 1)
    def _():
        sout_ref[0] = st_ref[...]


def _gdn(proj, small, conv_buf, s0, conv_w, a_log, dt_bias, norm_g, *, bsz, rows_per_seq, t_valid):
    C = GDN_CHUNK
    if rows_per_seq >= C:
        assert rows_per_seq % C == 0
        rows, nc = C, rows_per_seq // C
    else:
        assert rows_per_seq % SUBLANES == 0
        rows, nc = rows_per_seq, 1
    wblk = GDN_WIDTH
    zeros = jnp.zeros((LANES - 2 * GDN_HEADS,), F32)
    alog_row = jnp.concatenate([jnp.zeros((GDN_HEADS,), F32), a_log, zeros]).reshape(1, LANES)
    dt_row = jnp.concatenate([jnp.zeros((GDN_HEADS,), F32), dt_bias, zeros]).reshape(1, LANES)
    row_map = lambda b, c: b * nc + c
    in_specs = [
        pl.BlockSpec((rows, wblk), lambda b, c: (row_map(b, c), COL_Q // wblk)),
        pl.BlockSpec((rows, wblk), lambda b, c: (row_map(b, c), COL_K // wblk)),
        pl.BlockSpec((rows, wblk), lambda b, c: (row_map(b, c), COL_V // wblk)),
        pl.BlockSpec((rows, wblk), lambda b, c: (row_map(b, c), COL_Z // wblk)),
        pl.BlockSpec((rows, LANES), lambda b, c: (row_map(b, c), 0)),
        pl.BlockSpec((1, CONV_W - 1, 3 * GDN_WIDTH), lambda b, c: (b, 0, 0)),
        pl.BlockSpec((1, GDN_HEADS, HEAD_DIM, HEAD_DIM), lambda b, c: (b, 0, 0, 0)),
        pl.BlockSpec((CONV_W, 3 * GDN_WIDTH), lambda b, c: (0, 0)),
        pl.BlockSpec((1, LANES), lambda b, c: (0, 0)),
        pl.BlockSpec((1, LANES), lambda b, c: (0, 0)),
        pl.BlockSpec((1, HEAD_DIM), lambda b, c: (0, 0)),
    ]
    out_specs = [
        pl.BlockSpec((rows, wblk), lambda b, c: (row_map(b, c), 0)),
        pl.BlockSpec((1, GDN_HEADS, HEAD_DIM, HEAD_DIM), lambda b, c: (b, 0, 0, 0)),
    ]
    return pl.pallas_call(
        functools.partial(_gdn_kernel, rows=rows, nc=nc, t_valid=t_valid),
        grid=(bsz, nc),
        in_specs=in_specs,
        out_specs=out_specs,
        out_shape=[jax.ShapeDtypeStruct((bsz * rows_per_seq, GDN_WIDTH), BF16),
                   jax.ShapeDtypeStruct((bsz, GDN_HEADS, HEAD_DIM, HEAD_DIM), F32)],
        scratch_shapes=[pltpu.VMEM((C + SUBLANES, 3 * GDN_WIDTH), F32),
                        pltpu.VMEM((GDN_HEADS, HEAD_DIM, HEAD_DIM), F32)],
        compiler_params=_cparams(("parallel", "arbitrary")),
        name="gdn_chunk_scan",
    )(proj, proj, proj, proj, small, conv_buf, s0, conv_w, alog_row, dt_row, norm_g.reshape(1, HEAD_DIM))


def _prep_kernel(nq_ref, cmp_ref, slc_ref, win_ref, sm_ref, qg_ref, kg_ref,
                 q_out, cmp_out, slc_out, win_out, gate_out, slc_bf_out, win_bf_out):
    tm = nq_ref.shape[0]
    qg = qg_ref[...]
    for h in range(NSA_HEADS):
        sl = slice(h * HEAD_DIM, (h + 1) * HEAD_DIM)
        q_out[:, sl] = _rms(nq_ref[:, sl], qg)
    for slab in range(KV_SLABS):
        sl = slice(slab * HEAD_DIM, (slab + 1) * HEAD_DIM)
        cmp_out[pl.ds(slab, tm, stride=KV_SLABS), :] = cmp_ref[:, sl]
    for src, dst, dst_bf, gi in ((slc_ref, slc_out, slc_bf_out, 1), (win_ref, win_out, win_bf_out, 2)):
        kg = kg_ref[gi:gi + 1, :]
        for slab in range(KV_SLABS):
            sl = slice(slab * HEAD_DIM, (slab + 1) * HEAD_DIM)
            x = _rms(src[:, sl], kg) if slab < NSA_KV_HEADS else src[:, sl]
            dst[pl.ds(slab, tm, stride=KV_SLABS), :] = x
            dst_bf[:, sl] = x.astype(BF16)
    sig = jax.nn.sigmoid(sm_ref[...])
    per = NSA_GROUP * 3
    for g in range(NSA_KV_HEADS):
        gate_out[g] = pltpu.roll(sig, LANES - (SMALL_GATE + per * g), axis=1)


def _nsa_prep(proj, small, q_norm_g, k_norm_g, *, tm):
    m = proj.shape[0]
    assert m % tm == 0
    return pl.pallas_call(
        _prep_kernel,
        grid=(m // tm,),
        in_specs=[pl.BlockSpec((tm, NSA_WIDTH), lambda i: (i, COL_NQ // NSA_WIDTH)),
                  pl.BlockSpec((tm, KV_WIDTH), lambda i: (i, COL_CMP // KV_WIDTH)),
                  pl.BlockSpec((tm, KV_WIDTH), lambda i: (i, COL_SLC // KV_WIDTH)),
                  pl.BlockSpec((tm, KV_WIDTH), lambda i: (i, COL_WIN // KV_WIDTH)),
                  pl.BlockSpec((tm, LANES), lambda i: (i, 0)),
                  pl.BlockSpec((1, HEAD_DIM), lambda i: (0, 0)),
                  pl.BlockSpec((3, HEAD_DIM), lambda i: (0, 0))],
        out_specs=[pl.BlockSpec((tm, NSA_WIDTH), lambda i: (i, 0)),
                   pl.BlockSpec((tm * KV_SLABS, HEAD_DIM), lambda i: (i, 0)),
                   pl.BlockSpec((tm * KV_SLABS, HEAD_DIM), lambda i: (i, 0)),
                   pl.BlockSpec((tm * KV_SLABS, HEAD_DIM), lambda i: (i, 0)),
                   pl.BlockSpec((NSA_KV_HEADS, tm, LANES), lambda i: (0, i, 0)),
                   pl.BlockSpec((tm, KV_WIDTH), lambda i: (i, 0)),
                   pl.BlockSpec((tm, KV_WIDTH), lambda i: (i, 0))],
        out_shape=[jax.ShapeDtypeStruct((m, NSA_WIDTH), F32),
                   jax.ShapeDtypeStruct((m * KV_SLABS, HEAD_DIM), F32),
                   jax.ShapeDtypeStruct((m * KV_SLABS, HEAD_DIM), F32),
                   jax.ShapeDtypeStruct((m * KV_SLABS, HEAD_DIM), F32),
                   jax.ShapeDtypeStruct((NSA_KV_HEADS, m, LANES), F32),
                   jax.ShapeDtypeStruct((m, KV_WIDTH), BF16),
                   jax.ShapeDtypeStruct((m, KV_WIDTH), BF16)],
        compiler_params=_cparams(("parallel",)),
        name="nsa_prep",
    )(proj, proj, proj, proj, small, q_norm_g.reshape(1, HEAD_DIM), k_norm_g)


def _pool_kernel(tbl_ref, *refs, pg):
    del tbl_ref
    page_refs = refs[:pg]
    w_ref, s0_ref, s1_ref = refs[pg:]
    sub = PAGE_SIZE // CMP_STRIDE
    out_rows = sub * KV_SLABS
    for u in range(pg):
        x = page_refs[u][...].reshape(sub, CMP_STRIDE, KV_SLABS, HEAD_DIM)
        s0_ref[0, u * out_rows:(u + 1) * out_rows, :] = jnp.sum(x * w_ref[0][None], axis=1).reshape(out_rows, HEAD_DIM)
        s1_ref[0, u * out_rows:(u + 1) * out_rows, :] = jnp.sum(x * w_ref[1][None], axis=1).reshape(out_rows, HEAD_DIM)


def _pool(rows2d, table, wexp):
    bsz, n_pages = table.shape
    pg = math.gcd(n_pages, 16)
    sub = PAGE_SIZE // CMP_STRIDE
    page_rows = PAGE_SIZE * KV_SLABS
    out_rows = pg * sub * KV_SLABS

    def page_spec(u):
        return pl.BlockSpec((page_rows, HEAD_DIM), lambda b, p, tbl: (tbl[b, p * pg + u], 0))

    grid_spec = pltpu.PrefetchScalarGridSpec(
        num_scalar_prefetch=1,
        grid=(bsz, n_pages // pg),
        in_specs=[page_spec(u) for u in range(pg)]
        + [pl.BlockSpec((2, CMP_STRIDE, KV_SLABS, HEAD_DIM), lambda b, p, tbl: (0, 0, 0, 0))],
        out_specs=[pl.BlockSpec((1, out_rows, HEAD_DIM), lambda b, p, tbl: (b, p, 0)),
                   pl.BlockSpec((1, out_rows, HEAD_DIM), lambda b, p, tbl: (b, p, 0))],
    )
    shape = jax.ShapeDtypeStruct((bsz, n_pages * sub * KV_SLABS, HEAD_DIM), F32)
    return pl.pallas_call(
        functools.partial(_pool_kernel, pg=pg), grid_spec=grid_spec, out_shape=[shape, shape],
        compiler_params=_cparams(("parallel", "arbitrary")), name="cmp_pool",
    )(table, *([rows2d] * pg), wexp)


def _cmp_fin_kernel(s0_ref, s1_ref, phi_ref, kg_ref, kc_ref, vc_ref, *, n_cmp):
    nb = s0_ref.shape[1] // KV_SLABS
    live = lax.broadcasted_iota(jnp.int32, (nb, HEAD_DIM), 0) < n_cmp
    zero_row = jnp.zeros((1, HEAD_DIM), F32)

    def pooled(slab):
        first = s0_ref[0, pl.ds(slab, nb, stride=KV_SLABS), :]
        second = s1_ref[0, pl.ds(KV_SLABS + slab, nb - 1, stride=KV_SLABS), :]
        return first + jnp.concatenate([second, zero_row], axis=0)

    for g in range(NSA_KV_HEADS):
        kc_ref[0, g] = jnp.where(live, _rms(_dot(pooled(g), phi_ref[0, g], HI), kg_ref[...]), 0.0)
        vc_ref[0, g] = jnp.where(live, _dot(pooled(NSA_KV_HEADS + g), phi_ref[1, g], HI), 0.0)


def _cmp_finish(s0, s1, phi, kg, *, n_cmp):
    bsz = s0.shape[0]
    nb = s0.shape[1] // KV_SLABS
    assert n_cmp <= nb - 1
    shape = jax.ShapeDtypeStruct((bsz, NSA_KV_HEADS, nb, HEAD_DIM), F32)
    return pl.pallas_call(
        functools.partial(_cmp_fin_kernel, n_cmp=n_cmp),
        grid=(bsz,),
        in_specs=[pl.BlockSpec((1, nb * KV_SLABS, HEAD_DIM), lambda b: (b, 0, 0)),
                  pl.BlockSpec((1, nb * KV_SLABS, HEAD_DIM), lambda b: (b, 0, 0)),
                  pl.BlockSpec((2, NSA_KV_HEADS, HEAD_DIM, HEAD_DIM), lambda b: (0, 0, 0, 0)),
                  pl.BlockSpec((1, HEAD_DIM), lambda b: (0, 0))],
        out_specs=[pl.BlockSpec((1, NSA_KV_HEADS, nb, HEAD_DIM), lambda b: (b, 0, 0, 0)),
                   pl.BlockSpec((1, NSA_KV_HEADS, nb, HEAD_DIM), lambda b: (b, 0, 0, 0))],
        out_shape=[shape, shape],
        compiler_params=_cparams(("parallel",)),
        name="cmp_finish",
    )(s0, s1, phi, kg.reshape(1, HEAD_DIM))


def _stack_heads(ref, g, tq, dtype):
    parts = [ref[:, (g * NSA_GROUP + r) * HEAD_DIM:(g * NSA_GROUP + r + 1) * HEAD_DIM].astype(dtype)
             for r in range(NSA_GROUP)]
    return jnp.concatenate(parts, axis=0)


def _cmp_topk_kernel(q_ref, kc_ref, vc_ref, ov_ref, ocmp_ref, sel_ref, *, tq, n_cmp, n_slc, offset, token_lanes):
    i = pl.program_id(1)
    nb = kc_ref.shape[2]
    nsp = sel_ref.shape[2]
    rows = NSA_GROUP * tq
    r4 = lax.broadcasted_iota(jnp.int32, (rows, nb), 0)
    n4 = lax.broadcasted_iota(jnp.int32, (rows, nb), 1)
    tpos = offset + i * tq + (r4 & (tq - 1))
    valid = (n4 * CMP_STRIDE + (CMP_BLOCK - 1) <= tpos) & (n4 < n_cmp)

    if token_lanes:
        nsr = -(-n_slc // SUBLANES) * SUBLANES
        shape, blk_axis, tok_axis = (nsr, tq), 0, 1
    else:
        shape, blk_axis, tok_axis = (tq, nsp), 1, 0
    blk = lax.broadcasted_iota(jnp.int32, shape, blk_axis)
    tq_pos = offset + i * tq + lax.broadcasted_iota(jnp.int32, shape, tok_axis)
    cur = tq_pos // SLC_BLOCK
    forced = (blk == 0) | ((blk <= cur) & (blk > cur - SLC_LOCAL))
    future = blk > cur
    in_range = blk < n_slc
    topk = min(SLC_TOPK, n_slc)

    for g in range(NSA_KV_HEADS):
        q4 = _stack_heads(q_ref, g, tq, F32)
        s = _dot_nt(q4, kc_ref[0, g], HI) * SCALE
        s = jnp.where(valid, s, -jnp.inf)
        m = jnp.max(s, axis=-1, keepdims=True)
        m = jnp.where(m == -jnp.inf, 0.0, m)
        p = jnp.exp(s - m)
        p = p / jnp.maximum(jnp.sum(p, axis=-1, keepdims=True), jnp.finfo(jnp.float32).tiny)
        o = _dot(p.astype(BF16), vc_ref[0, g].astype(BF16))
        psum = p[0:tq]
        for r in range(NSA_GROUP):
            col = (g * NSA_GROUP + r) * HEAD_DIM
            ocmp_ref[:, col:col + HEAD_DIM] = o[r * tq:(r + 1) * tq]
            if r:
                psum = psum + p[r * tq:(r + 1) * tq]
        if token_lanes:
            imp = _dot_nt(ov_ref[0:shape[0], :], psum, HI)
        else:
            imp = _dot(psum, ov_ref[...], HI)
        score = jnp.where(forced, FORCE_SCORE, jnp.where(future, -FORCE_SCORE, imp))
        score = jnp.where(in_range, score, -jnp.inf)
        rank = jnp.zeros(shape, jnp.int32)
        for j in range(n_slc):
            cj = score[j:j + 1, :] if token_lanes else score[:, j:j + 1]
            ahead = (cj > score) | ((cj == score) & (blk > j))
            rank = rank + ahead.astype(jnp.int32)
        sel = ((rank < topk) & in_range).astype(F32)
        if token_lanes:
            sel = jnp.concatenate([sel, jnp.zeros((nsp - shape[0], tq), F32)], axis=0).T
        sel_ref[g] = sel


def _cmp_topk(qn, kc, vc, overlap, *, bsz, rows_per_seq, tq, n_cmp, n_slc, offset):
    m = qn.shape[0]
    nt = rows_per_seq // tq
    nb = kc.shape[2]
    nsp = overlap.shape[1]
    token_lanes = tq % LANES == 0 and nsp == tq
    if token_lanes:
        overlap = overlap.T
    return pl.pallas_call(
        functools.partial(_cmp_topk_kernel, tq=tq, n_cmp=n_cmp, n_slc=n_slc, offset=offset,
                          token_lanes=token_lanes),
        grid=(bsz, nt),
        in_specs=[pl.BlockSpec((tq, NSA_WIDTH), lambda b, i: (b * nt + i, 0)),
                  pl.BlockSpec((1, NSA_KV_HEADS, nb, HEAD_DIM), lambda b, i: (b, 0, 0, 0)),
                  pl.BlockSpec((1, NSA_KV_HEADS, nb, HEAD_DIM), lambda b, i: (b, 0, 0, 0)),
                  pl.BlockSpec(overlap.shape, lambda b, i: (0, 0))],
        out_specs=[pl.BlockSpec((tq, NSA_WIDTH), lambda b, i: (b * nt + i, 0)),
                   pl.BlockSpec((NSA_KV_HEADS, tq, nsp), lambda b, i: (0, b * nt + i, 0))],
        out_shape=[jax.ShapeDtypeStruct((m, NSA_WIDTH), F32),
                   jax.ShapeDtypeStruct((NSA_KV_HEADS, m, nsp), F32)],
        compiler_params=_cparams(("parallel", "arbitrary")),
        name="cmp_attn_topk",
    )(qn, kc, vc, overlap)


def _overlap_matrix(nb, nsp, n_cmp, n_slc):
    cs = np.arange(nb) * CMP_STRIDE
    ss = np.arange(nsp) * SLC_BLOCK
    lo = np.maximum(cs[:, None], ss[None, :])
    hi = np.minimum(cs[:, None] + CMP_BLOCK, ss[None, :] + SLC_BLOCK)
    ov = (np.maximum(hi - lo, 0) / CMP_BLOCK).astype(np.float32)
    ov[n_cmp:, :] = 0.0
    ov[:, n_slc:] = 0.0
    return jnp.asarray(ov)


def _flash_init(m_ref, l_ref, acc_ref):
    m_ref[...] = jnp.full(m_ref.shape, NEG, F32)
    l_ref[...] = jnp.zeros(l_ref.shape, F32)
    acc_ref[...] = jnp.zeros(acc_ref.shape, F32)


def _lane_rep(col, width):
    tile = jnp.broadcast_to(col, (col.shape[0], LANES))
    return tile if width == LANES else jnp.concatenate([tile] * (width // LANES), axis=1)


def _flash_update(s, mask, v, m_ref, l_ref, acc_ref):
    rows, tk = s.shape
    reps = tk // LANES
    s = jnp.where(mask, s, NEG)
    m_prev = m_ref[...]
    m_new = jnp.maximum(m_prev, jnp.broadcast_to(jnp.max(s, axis=-1, keepdims=True), (rows, LANES)))
    alpha = jnp.exp(m_prev - m_new)
    p = jnp.exp(s - jnp.concatenate([m_new] * reps, axis=1))
    l_ref[...] = alpha * l_ref[...] + jnp.broadcast_to(jnp.sum(p, axis=-1, keepdims=True), (rows, LANES))
    acc_ref[...] = alpha * acc_ref[...] + _dot(p.astype(v.dtype), v)
    m_ref[...] = m_new


def _pattn_kernel(q_ref, ks_ref, vs_ref, kw_ref, vw_ref, ocmp_ref, sel_ref, gate_ref, exp_ref, o_ref,
                  sexp_ref, m_ref, l_ref, acc_ref, *, tq, tk, wlen):
    i = pl.program_id(2)
    t = ks_ref.shape[0]
    q4 = jnp.concatenate([(q_ref[:, r * HEAD_DIM:(r + 1) * HEAD_DIM] * SCALE).astype(BF16)
                          for r in range(NSA_GROUP)], axis=0)
    sexp_ref[...] = _dot(sel_ref[0].astype(BF16), exp_ref[...])
    qpos = i * tq + lax.broadcasted_iota(jnp.int32, (tq, tk), 0)
    lane = lax.broadcasted_iota(jnp.int32, (tq, tk), 1)

    def tile4(x):
        return jnp.concatenate([x] * NSA_GROUP, axis=0)

    def slc_step(j, carry):
        start = pl.multiple_of(j * tk, tk)
        s = _dot_nt(q4, ks_ref[pl.ds(start, tk), :])
        mask = (sexp_ref[:, pl.ds(start, tk)] > 0.5) & (start + lane <= qpos)
        _flash_update(s, tile4(mask), vs_ref[pl.ds(start, tk), :], m_ref, l_ref, acc_ref)
        return carry

    _flash_init(m_ref, l_ref, acc_ref)
    lax.fori_loop(0, ((i + 1) * tq + tk - 1) // tk, slc_step, 0)
    o_slc = acc_ref[...] / l_ref[...]

    wstart = pl.multiple_of(jnp.clip((i + 1) * tq - wlen, 0, t - wlen), tq)
    s = _dot_nt(q4, kw_ref[pl.ds(wstart, wlen), :])
    wdiff = (i * tq + lax.broadcasted_iota(jnp.int32, (tq, wlen), 0)
             - (wstart + lax.broadcasted_iota(jnp.int32, (tq, wlen), 1)))
    s = jnp.where(tile4((wdiff >= 0) & (wdiff < WINDOW)), s, NEG)
    p = jnp.exp(s - _lane_rep(jnp.max(s, axis=-1, keepdims=True), wlen))
    o_win = _dot(p.astype(BF16), vw_ref[pl.ds(wstart, wlen), :]) / _lane_rep(jnp.sum(p, axis=-1, keepdims=True), HEAD_DIM)

    gates = gate_ref[0]
    for r in range(NSA_GROUP):
        sl = slice(r * tq, (r + 1) * tq)
        col = slice(r * HEAD_DIM, (r + 1) * HEAD_DIM)
        o = (gates[:, 3 * r:3 * r + 1] * ocmp_ref[:, col]
             + gates[:, 3 * r + 1:3 * r + 2] * o_slc[sl]
             + gates[:, 3 * r + 2:3 * r + 3] * o_win[sl])
        o_ref[:, col] = o.astype(o_ref.dtype)


def _prompt_attn(qn, slc_bf, win_bf, ocmp, sel, gates, *, bsz, t, tq):
    m = qn.shape[0]
    nt = t // tq
    gw = NSA_GROUP * HEAD_DIM
    half = NSA_KV_HEADS
    nsp = sel.shape[2]
    n_slc = -(-t // SLC_BLOCK)
    expand = np.zeros((nsp, t), np.float32)
    expand[np.arange(t) // SLC_BLOCK, np.arange(t)] = 1.0
    assert n_slc <= nsp
    tk = next(c for c in (512, 256, 128) if t % c == 0 and c >= tq)
    wlen = min(t, WINDOW + tq)
    assert wlen % tq == 0 and t % tq == 0
    kv_spec = lambda which: pl.BlockSpec((t, HEAD_DIM), lambda b, g, i: (b, which * half + g))
    return pl.pallas_call(
        functools.partial(_pattn_kernel, tq=tq, tk=tk, wlen=wlen),
        grid=(bsz, NSA_KV_HEADS, nt),
        in_specs=[pl.BlockSpec((tq, gw), lambda b, g, i: (b * nt + i, g)),
                  kv_spec(0), kv_spec(1), kv_spec(0), kv_spec(1),
                  pl.BlockSpec((tq, gw), lambda b, g, i: (b * nt + i, g)),
                  pl.BlockSpec((1, tq, nsp), lambda b, g, i: (g, b * nt + i, 0)),
                  pl.BlockSpec((1, tq, LANES), lambda b, g, i: (g, b * nt + i, 0)),
                  pl.BlockSpec((nsp, t), lambda b, g, i: (0, 0))],
        out_specs=pl.BlockSpec((tq, gw), lambda b, g, i: (b * nt + i, g)),
        out_shape=jax.ShapeDtypeStruct((m, NSA_WIDTH), BF16),
        scratch_shapes=[pltpu.VMEM((tq, t), F32),
                        pltpu.VMEM((NSA_GROUP * tq, LANES), F32),
                        pltpu.VMEM((NSA_GROUP * tq, LANES), F32),
                        pltpu.VMEM((NSA_GROUP * tq, HEAD_DIM), F32)],
        compiler_params=_cparams(("parallel", "parallel", "arbitrary")),
        name="nsa_prompt_attn",
    )(qn, slc_bf, slc_bf, win_bf, win_bf, ocmp, sel, gates, jnp.asarray(expand, BF16))


def _sattn_kernel(tbl_ref, q_ref, *refs, tq, n_steps, pg, past):
    del tbl_ref
    page_refs = refs[:pg]
    (snew_ref, wcache_ref, wnew_ref, ocmp_ref, sel_ref, gate_ref, exp_ref, o_ref, s_buf, v_buf) = refs[pg:]
    j = pl.program_id(1)
    rows = NSA_GROUP * tq
    half = NSA_KV_HEADS * HEAD_DIM
    nsp = sel_ref.shape[2]
    n_prev = wcache_ref.shape[0] // KV_SLABS

    def tile4(x):
        return jnp.concatenate([x] * NSA_GROUP, axis=0)

    q4_bf = [_stack_heads(q_ref, g, tq, BF16) for g in range(NSA_KV_HEADS)]
    for u in range(pg):
        key0 = pl.multiple_of((j * pg + u) * PAGE_SIZE, PAGE_SIZE)
        for g in range(NSA_KV_HEADS):
            k = page_refs[u][pl.ds(g, PAGE_SIZE, stride=KV_SLABS), :].astype(BF16)
            s_buf[g * rows:(g + 1) * rows, pl.ds(key0, PAGE_SIZE)] = _dot_nt(q4_bf[g], k)
            v_buf[g, pl.ds(key0, PAGE_SIZE), :] = (
                page_refs[u][pl.ds(NSA_KV_HEADS + g, PAGE_SIZE, stride=KV_SLABS), :].astype(BF16))

    @pl.when(j == n_steps - 1)
    def _():
        blk_lane = lax.broadcasted_iota(jnp.int32, (tq, nsp), 1)
        trow = lax.broadcasted_iota(jnp.int32, (tq, tq), 0)
        tcol = lax.broadcasted_iota(jnp.int32, (tq, tq), 1)
        causal = tcol <= trow
        wrow = lax.broadcasted_iota(jnp.int32, (tq, n_prev), 0)
        wcol = lax.broadcasted_iota(jnp.int32, (tq, n_prev), 1)
        wdiff = wrow + n_prev - wcol
        wmask = (wdiff >= 0) & (wdiff < WINDOW)
        ndiff = trow - tcol
        nmask = (ndiff >= 0) & (ndiff < WINDOW)
        gates_all = gate_ref[...]
        for g in range(NSA_KV_HEADS):
            q4 = _stack_heads(q_ref, g, tq, F32)
            selg = sel_ref[g]
            sexp = _dot(selg.astype(BF16), exp_ref[...])
            sc = jnp.where(tile4(sexp > 0.5), s_buf[g * rows:(g + 1) * rows, :] * SCALE, NEG)
            kn = snew_ref[pl.ds(g, tq, stride=KV_SLABS), :]
            vn = snew_ref[pl.ds(NSA_KV_HEADS + g, tq, stride=KV_SLABS), :]
            flag = jnp.zeros((tq, tq), jnp.bool_)
            for u in range(tq):
                col = jnp.sum(jnp.where(blk_lane == (past + u) // SLC_BLOCK, selg, 0.0), axis=1, keepdims=True) > 0.5
                flag = flag | (col & (tcol == u))
            sn = jnp.where(tile4(flag & causal), _dot_nt(q4, kn, HI) * SCALE, NEG)
            mx = jnp.maximum(jnp.max(sc, axis=-1, keepdims=True), jnp.max(sn, axis=-1, keepdims=True))
            pc = jnp.exp(sc - mx)
            pn = jnp.exp(sn - mx)
            den = jnp.sum(pc, axis=-1, keepdims=True) + jnp.sum(pn, axis=-1, keepdims=True)
            o_slc = (_dot(pc.astype(BF16), v_buf[g]) + _dot(pn, vn, HI)) / den

            kc = wcache_ref[pl.ds(g, n_prev, stride=KV_SLABS), :].astype(BF16)
            vc = wcache_ref[pl.ds(NSA_KV_HEADS + g, n_prev, stride=KV_SLABS), :].astype(BF16)
            kw = wnew_ref[pl.ds(g, tq, stride=KV_SLABS), :]
            vw = wnew_ref[pl.ds(NSA_KV_HEADS + g, tq, stride=KV_SLABS), :]
            sc = jnp.where(tile4(wmask), _dot_nt(q4.astype(BF16), kc) * SCALE, NEG)
            sn = jnp.where(tile4(nmask), _dot_nt(q4, kw, HI) * SCALE, NEG)
            mx = jnp.maximum(jnp.max(sc, axis=-1, keepdims=True), jnp.max(sn, axis=-1, keepdims=True))
            pc = jnp.exp(sc - mx)
            pn = jnp.exp(sn - mx)
            den = jnp.sum(pc, axis=-1, keepdims=True) + jnp.sum(pn, axis=-1, keepdims=True)
            o_win = (_dot(pc.astype(BF16), vc) + _dot(pn, vw, HI)) / den

            gates = gates_all[g]
            for r in range(NSA_GROUP):
                sl = slice(r * tq, (r + 1) * tq)
                col = slice((g * NSA_GROUP + r) * HEAD_DIM, (g * NSA_GROUP + r + 1) * HEAD_DIM)
                o = (gates[:, 3 * r:3 * r + 1] * ocmp_ref[:, col]
                     + gates[:, 3 * r + 1:3 * r + 2] * o_slc[sl]
                     + gates[:, 3 * r + 2:3 * r + 3] * o_win[sl])
                o_ref[:, col] = o.astype(o_ref.dtype)


def _sample_attn(qn, cache2d, page_table, slc_new, win_cache2d, win_row0, win_new, ocmp, sel, gates,
                 *, bsz, tq, past, n_prev):
    n_pages = page_table.shape[1]
    pg = math.gcd(n_pages, 16)
    n_steps = n_pages // pg
    nsp = sel.shape[2]
    rows = NSA_GROUP * tq
    page_rows = PAGE_SIZE * KV_SLABS
    expand = np.zeros((nsp, past), np.float32)
    expand[np.arange(past) // SLC_BLOCK, np.arange(past)] = 1.0

    def page_spec(u):
        return pl.BlockSpec((page_rows, HEAD_DIM), lambda b, j, tbl: (tbl[b, j * pg + u], 0))

    grid_spec = pltpu.PrefetchScalarGridSpec(
        num_scalar_prefetch=1,
        grid=(bsz, n_steps),
        in_specs=[pl.BlockSpec((tq, NSA_WIDTH), lambda b, j, tbl: (b, 0))]
        + [page_spec(u) for u in range(pg)]
        + [pl.BlockSpec((tq * KV_SLABS, HEAD_DIM), lambda b, j, tbl: (b, 0)),
           pl.BlockSpec((n_prev * KV_SLABS, HEAD_DIM), lambda b, j, tbl: (win_row0 + b, 0)),
           pl.BlockSpec((tq * KV_SLABS, HEAD_DIM), lambda b, j, tbl: (b, 0)),
           pl.BlockSpec((tq, NSA_WIDTH), lambda b, j, tbl: (b, 0)),
           pl.BlockSpec((NSA_KV_HEADS, tq, nsp), lambda b, j, tbl: (0, b, 0)),
           pl.BlockSpec((NSA_KV_HEADS, tq, LANES), lambda b, j, tbl: (0, b, 0)),
           pl.BlockSpec((nsp, past), lambda b, j, tbl: (0, 0))],
        out_specs=pl.BlockSpec((tq, NSA_WIDTH), lambda b, j, tbl: (b, 0)),
        scratch_shapes=[pltpu.VMEM((NSA_KV_HEADS * rows, past), F32),
                        pltpu.VMEM((NSA_KV_HEADS, past, HEAD_DIM), BF16)],
    )
    return pl.pallas_call(
        functools.partial(_sattn_kernel, tq=tq, n_steps=n_steps, pg=pg, past=past),
        grid_spec=grid_spec,
        out_shape=jax.ShapeDtypeStruct((bsz * tq, NSA_WIDTH), BF16),
        compiler_params=_cparams(("parallel", "arbitrary")),
        name="nsa_sample_attn",
    )(page_table, qn, *([cache2d] * pg), slc_new, win_cache2d, win_new, ocmp, sel, gates,
      jnp.asarray(expand, BF16))


def _split_w_in(w_in):
    wt = w_in.T
    pad = jnp.zeros((LANES - 2 * GDN_HEADS - 3 * NSA_HEADS, wt.shape[1]), w_in.dtype)
    wt_small = jnp.concatenate([wt[W_IN_B0:W_IN_N0], wt[W_IN_G0:], pad], axis=0)
    return wt, wt[W_IN_N0:W_IN_G0], wt_small


def _in_proj(x2d, p, *, tm):
    xn = _rms_cast(x2d, p["attn_norm_g"], tm=min(tm, 512))
    w_full, w_b, w_small = p["w_in"]
    proj_a = _mm_nw([xn], w_full, n=PROJ_A_WIDTH, tm=tm, tn=512, w_t=True)
    proj_b = _mm_nw([xn], w_b, n=PROJ_B_WIDTH, tm=tm, tn=512, w_t=True)
    small = _mm_nw([xn], w_small, n=LANES, tm=tm, tn=LANES, w_t=True)
    return proj_a, proj_b, small


def _pos_weights(cmp_pos_w):
    w = cmp_pos_w.reshape(2, CMP_BLOCK // CMP_STRIDE, CMP_STRIDE, NSA_KV_HEADS)
    w = jnp.transpose(w, (1, 2, 0, 3))
    return jnp.broadcast_to(w[..., None], w.shape + (HEAD_DIM,)).reshape(2, CMP_STRIDE, KV_SLABS, HEAD_DIM)


def _mixers_out(x2d, o_gdn, o_nsa, p, *, tm):
    w_o, w_up = p["w_o"], p["w_up"]
    x1 = _mm_nw([o_gdn, o_nsa], w_o, n=w_o.shape[1], tm=tm, tn=512, res=x2d)
    xn = _rms_cast(x1, p["mlp_norm_g"], tm=min(tm, 512))
    if "w_down_bf" not in p:
        hid, p["w_down_bf"] = _mm_nw([xn], w_up, n=w_up.shape[1], tm=tm, tn=512, relu2=True, out_dtype=BF16,
                                     cast_side=p["w_down"])
    else:
        hid = _mm_nw([xn], w_up, n=w_up.shape[1], tm=tm, tn=512, relu2=True, out_dtype=BF16)
    w_down = p["w_down_bf"]
    return _mm_res(hid, w_down, x1, tm=tm, tn=min(1024, w_down.shape[1]), tk=min(2048, w_down.shape[0]))


def _round_up(x, n):
    return -(-x // n) * n


def _prompt_layer(x, p):
    bsz, t, d = x.shape
    m = bsz * t
    x2d = x.reshape(m, d)
    proj_a, proj_b, small = _in_proj(x2d, p, tm=1024)
    conv0 = jnp.zeros((bsz, CONV_W - 1, 3 * GDN_WIDTH), F32)
    s_zero = jnp.zeros((bsz, GDN_HEADS, HEAD_DIM, HEAD_DIM), F32)
    o_gdn, s_new = _gdn(proj_a, small, conv0, s_zero, p["gdn_conv_w"], p["gdn_a_log"], p["gdn_dt_bias"],
                        p["gdn_norm_g"], bsz=bsz, rows_per_seq=t, t_valid=t)
    qn, cmp_n, slc_n, win_n, gates, slc_bf, win_bf = _nsa_prep(proj_b, small, p["q_norm_g"], p["k_norm_g"], tm=512)

    n_cmp = (t - CMP_BLOCK) // CMP_STRIDE + 1
    n_slc = -(-t // SLC_BLOCK)
    n_pages = t // PAGE_SIZE
    table = jnp.arange(bsz * n_pages, dtype=jnp.int32).reshape(bsz, n_pages)
    kv_tail = (2, NSA_KV_HEADS, HEAD_DIM)
    s0, s1 = _pool(cmp_n, table, p["cmp_w"])
    kc, vc = _cmp_finish(s0, s1, p["cmp_phi"], p["k_norm_g"][0], n_cmp=n_cmp)
    nb = kc.shape[2]
    nsp = _round_up(n_slc, LANES)
    tq = 128
    ocmp, sel = _cmp_topk(qn, kc, vc, _overlap_matrix(nb, nsp, n_cmp, n_slc), bsz=bsz, rows_per_seq=t, tq=tq,
                          n_cmp=n_cmp, n_slc=n_slc, offset=0)
    o_nsa = _prompt_attn(qn, slc_bf, win_bf, ocmp, sel, gates, bsz=bsz, t=t, tq=tq)
    y = _mixers_out(x2d, o_gdn, o_nsa, p, tm=1024)

    keep = min(WINDOW, t)
    return (y.reshape(bsz, t, d),
            cmp_n.reshape((bsz, t) + kv_tail),
            slc_n.reshape((bsz, t) + kv_tail),
            win_n.reshape((bsz, t) + kv_tail)[:, t - keep:],
            proj_a.reshape(bsz, t, PROJ_A_WIDTH)[:, t - (CONV_W - 1):, 0:3 * GDN_WIDTH],
            s_new)


def _sample_layer(x, layer, cache_cmp_kv, cache_slc_kv, cache_win_kv, conv_buf, s0_state, page_table, p):
    bsz, t, d = x.shape
    tp = _round_up(t, SUBLANES)
    n_pages = page_table.shape[1]
    past = n_pages * PAGE_SIZE
    assert t <= SLC_BLOCK and past % SLC_BLOCK == 0 and t >= CONV_W - 1
    x2d = jnp.pad(x, ((0, 0), (0, tp - t), (0, 0))).reshape(bsz * tp, d)
    proj_a, proj_b, small = _in_proj(x2d, p, tm=bsz * tp)
    o_gdn, s_new = _gdn(proj_a, small, conv_buf, s0_state, p["gdn_conv_w"], p["gdn_a_log"], p["gdn_dt_bias"],
                        p["gdn_norm_g"], bsz=bsz, rows_per_seq=tp, t_valid=t)
    qn, cmp_n, slc_n, win_n, gates, _, _ = _nsa_prep(proj_b, small, p["q_norm_g"], p["k_norm_g"], tm=bsz * tp)

    total = past + t
    n_cmp = (total - CMP_BLOCK) // CMP_STRIDE + 1
    n_slc = -(-total // SLC_BLOCK)
    assert (n_cmp - 1) * CMP_STRIDE + CMP_BLOCK <= past
    depth, n_phys = cache_cmp_kv.shape[:2]
    table = page_table + layer * n_phys
    cache_rows = depth * n_phys * PAGE_SIZE * KV_SLABS
    s0, s1 = _pool(cache_cmp_kv.reshape(cache_rows, HEAD_DIM), table, p["cmp_w"])
    kc, vc = _cmp_finish(s0, s1, p["cmp_phi"], p["k_norm_g"][0], n_cmp=n_cmp)
    nb = kc.shape[2]
    nsp = _round_up(n_slc, LANES)
    ocmp, sel = _cmp_topk(qn, kc, vc, _overlap_matrix(nb, nsp, n_cmp, n_slc), bsz=bsz, rows_per_seq=tp, tq=tp,
                          n_cmp=n_cmp, n_slc=n_slc, offset=past)
    n_prev = cache_win_kv.shape[2]
    o_nsa = _sample_attn(qn, cache_slc_kv.reshape(cache_rows, HEAD_DIM), table, slc_n,
                         cache_win_kv.reshape(depth * bsz * n_prev * KV_SLABS, HEAD_DIM), layer * bsz,
                         win_n, ocmp, sel, gates, bsz=bsz, tq=tp, past=past, n_prev=n_prev)
    y = _mixers_out(x2d, o_gdn, o_nsa, p, tm=bsz * tp)

    kv_tail = (2, NSA_KV_HEADS, HEAD_DIM)
    win_all = jnp.concatenate([cache_win_kv[layer], win_n.reshape((bsz, tp) + kv_tail)[:, :t]], axis=1)
    keep = min(WINDOW, n_prev + t)
    return (y.reshape(bsz, tp, d)[:, :t],
            cmp_n.reshape((bsz, tp) + kv_tail)[:, :t],
            slc_n.reshape((bsz, tp) + kv_tail)[:, :t],
            win_all[:, n_prev + t - keep:],
            proj_a.reshape(bsz, tp, PROJ_A_WIDTH)[:, t - (CONV_W - 1):t, 0:3 * GDN_WIDTH],
            s_new)


def kernel(x_prompt, x_sample, cache_cmp_kv, cache_slc_kv, cache_win_kv, cache_gdn_conv, state_gdn, page_table, attn_norm_g, w_in, gdn_conv_w, gdn_a_log, gdn_dt_bias, gdn_norm_g, q_norm_g, k_norm_g, cmp_pos_w, cmp_phi, w_o, mlp_norm_g, w_up, w_down):
    depth = w_in.shape[0]
    yp, ys = x_prompt, x_sample
    per_layer = []
    for layer in range(depth):
        p = {
            "attn_norm_g": attn_norm_g[layer], "w_in": _split_w_in(w_in[layer]),
            "gdn_conv_w": gdn_conv_w[layer], "gdn_a_log": gdn_a_log[layer], "gdn_dt_bias": gdn_dt_bias[layer],
            "gdn_norm_g": gdn_norm_g[layer], "q_norm_g": q_norm_g[layer], "k_norm_g": k_norm_g[layer],
            "cmp_w": _pos_weights(cmp_pos_w[layer]), "cmp_phi": cmp_phi[layer],
            "w_o": w_o[layer], "mlp_norm_g": mlp_norm_g[layer],
            "w_up": w_up[layer], "w_down": w_down[layer],
        }
        yp, cmp_p, slc_p, win_p, conv_p, s_p = _prompt_layer(yp, p)
        ys, cmp_s, slc_s, win_s, conv_s, s_s = _sample_layer(
            ys, layer, cache_cmp_kv, cache_slc_kv, cache_win_kv, cache_gdn_conv[layer],
            state_gdn[layer], page_table, p)
        per_layer.append((cmp_p, cmp_s, slc_p, slc_s, win_p, win_s, conv_p, conv_s, s_p, s_s))
    st = [jnp.stack(z, axis=0) for z in zip(*per_layer)]
    return (yp, ys) + tuple(st)
```

```python
import functools
import math

import numpy as np
import jax
import jax.numpy as jnp
from jax import lax
from jax.experimental import pallas as pl
from jax.experimental.pallas import tpu as pltpu

F32 = jnp.float32
BF16 = jnp.bfloat16
HI = lax.Precision.HIGHEST

HEAD_DIM = 128
GDN_HEADS = 16
NSA_HEADS = 16
NSA_KV_HEADS = 4
NSA_GROUP = NSA_HEADS // NSA_KV_HEADS
GDN_WIDTH = GDN_HEADS * HEAD_DIM
NSA_WIDTH = NSA_HEADS * HEAD_DIM
KV_WIDTH = 2 * NSA_KV_HEADS * HEAD_DIM
KV_SLABS = 2 * NSA_KV_HEADS
CONV_W = 4
GDN_CHUNK = 64
GDN_HEAD_GROUP = 16
CMP_BLOCK = 32
CMP_STRIDE = 16
SLC_BLOCK = 64
SLC_TOPK = 16
SLC_LOCAL = 2
WINDOW = 512
PAGE_SIZE = 128
EPS = 1e-6
FORCE_SCORE = 1e9
NEG = -1e30
SCALE = HEAD_DIM ** -0.5

LANES = 128
SUBLANES = 8
VMEM_LIMIT = 52 * 1024 * 1024

COL_Q, COL_K, COL_V, COL_Z = 0, GDN_WIDTH, 2 * GDN_WIDTH, 3 * GDN_WIDTH
PROJ_A_WIDTH = 4 * GDN_WIDTH
COL_NQ = 0
COL_CMP = COL_NQ + NSA_WIDTH
COL_SLC = COL_CMP + KV_WIDTH
COL_WIN = COL_SLC + KV_WIDTH
PROJ_B_WIDTH = COL_WIN + KV_WIDTH
SMALL_B, SMALL_A, SMALL_GATE = 0, GDN_HEADS, 2 * GDN_HEADS
W_IN_B0 = PROJ_A_WIDTH
W_IN_N0 = W_IN_B0 + 2 * GDN_HEADS
W_IN_G0 = W_IN_N0 + PROJ_B_WIDTH


def _cparams(sem):
    return pltpu.CompilerParams(dimension_semantics=sem, vmem_limit_bytes=VMEM_LIMIT)


def _dot(a, b, prec=None):
    return jnp.dot(a, b, preferred_element_type=F32, precision=prec)


def _dot_nt(a, b, prec=None):
    return lax.dot_general(a, b, (((1,), (1,)), ((), ())), preferred_element_type=F32, precision=prec)


def _dot_tn(a, b, prec=None):
    return lax.dot_general(a, b, (((0,), (0,)), ((), ())), preferred_element_type=F32, precision=prec)


def _rms(x, g):
    return x * lax.rsqrt(jnp.mean(x * x, axis=-1, keepdims=True) + EPS) * g


def _silu(x):
    return x * jax.nn.sigmoid(x)


def _softplus(x):
    return jnp.maximum(x, 0.0) + jnp.log1p(jnp.exp(-jnp.abs(x)))


def _rms_cast_kernel(x_ref, g_ref, o_ref):
    o_ref[...] = _rms(x_ref[...], g_ref[...]).astype(o_ref.dtype)


def _rms_cast(x, g, *, tm):
    m, k = x.shape
    assert m % tm == 0
    return pl.pallas_call(
        _rms_cast_kernel,
        grid=(m // tm,),
        in_specs=[pl.BlockSpec((tm, k), lambda i: (i, 0)), pl.BlockSpec((1, k), lambda i: (0, 0))],
        out_specs=pl.BlockSpec((tm, k), lambda i: (i, 0)),
        out_shape=jax.ShapeDtypeStruct((m, k), BF16),
        compiler_params=_cparams(("parallel",)),
        name="rms_cast",
    )(x, g.reshape(1, k))


def _mm_nw_kernel(*refs, n_a, relu2, has_res, w_t, has_side):
    a_refs = refs[:n_a]
    w_ref = refs[n_a]
    r_ref = refs[n_a + 1] if has_res else None
    if has_side:
        side_ref, o_ref, side_out, wbf_ref = refs[-4:]
        side_out[...] = side_ref[...].astype(BF16)
    else:
        o_ref, wbf_ref = refs[-2:]

    @pl.when(pl.program_id(1) == 0)
    def _():
        w = w_ref[...]
        wbf_ref[...] = (w.T if w_t else w).astype(BF16)

    y, k0 = None, 0
    for a_ref in a_refs:
        part = _dot(a_ref[...], wbf_ref[k0:k0 + a_ref.shape[1], :])
        y = part if y is None else y + part
        k0 += a_ref.shape[1]
    if relu2:
        y = jnp.square(jnp.maximum(y, 0.0))
    if has_res:
        y = y + r_ref[...]
    o_ref[...] = y.astype(o_ref.dtype)


def _mm_nw(acts, w, *, n, tm, tn, res=None, relu2=False, out_dtype=F32, w_t=False, cast_side=None):
    m = acts[0].shape[0]
    k = sum(a.shape[1] for a in acts)
    assert m % tm == 0 and n % tn == 0 and w.shape[1 if w_t else 0] == k
    nm = m // tm
    in_specs = [pl.BlockSpec((tm, a.shape[1]), lambda j, i: (i, 0)) for a in acts]
    in_specs.append(pl.BlockSpec((tn, k), lambda j, i: (j, 0)) if w_t else pl.BlockSpec((k, tn), lambda j, i: (0, j)))
    args = list(acts) + [w]
    if res is not None:
        in_specs.append(pl.BlockSpec((tm, tn), lambda j, i: (i, j)))
        args.append(res)
    out_specs = pl.BlockSpec((tm, tn), lambda j, i: (i, j))
    out_shape = jax.ShapeDtypeStruct((m, n), out_dtype)
    if cast_side is not None:
        steps = (n // tn) * nm
        sr, sc = cast_side.shape
        assert sr % steps == 0 and (sr // steps) % 16 == 0
        side_spec = pl.BlockSpec((sr // steps, sc), lambda j, i: (j * nm + i, 0))
        in_specs.append(side_spec)
        args.append(cast_side)
        out_specs = [out_specs, side_spec]
        out_shape = [out_shape, jax.ShapeDtypeStruct((sr, sc), BF16)]
    return pl.pallas_call(
        functools.partial(_mm_nw_kernel, n_a=len(acts), relu2=relu2, has_res=res is not None, w_t=w_t,
                          has_side=cast_side is not None),
        grid=(n // tn, nm),
        in_specs=in_specs,
        out_specs=out_specs,
        out_shape=out_shape,
        scratch_shapes=[pltpu.VMEM((k, tn), BF16)],
        compiler_params=_cparams(("parallel", "arbitrary")),
        name="matmul_wcast",
    )(*args)


def _mm_res_kernel(a_ref, w_ref, r_ref, o_ref, acc_ref, *, nk):
    kk = pl.program_id(2)

    @pl.when(kk == 0)
    def _():
        acc_ref[...] = jnp.zeros_like(acc_ref)

    acc_ref[...] += _dot(a_ref[...], w_ref[...])

    @pl.when(kk == nk - 1)
    def _():
        o_ref[...] = r_ref[...] + acc_ref[...]


def _mm_res(a, w, res, *, tm, tn, tk):
    m, k = a.shape
    n = w.shape[1]
    assert m % tm == 0 and n % tn == 0 and k % tk == 0
    nk = k // tk
    return pl.pallas_call(
        functools.partial(_mm_res_kernel, nk=nk),
        grid=(m // tm, n // tn, nk),
        in_specs=[pl.BlockSpec((tm, tk), lambda i, j, kk: (i, kk)),
                  pl.BlockSpec((tk, tn), lambda i, j, kk: (kk, j)),
                  pl.BlockSpec((tm, tn), lambda i, j, kk: (i, j))],
        out_specs=pl.BlockSpec((tm, tn), lambda i, j, kk: (i, j)),
        out_shape=jax.ShapeDtypeStruct((m, n), F32),
        scratch_shapes=[pltpu.VMEM((tm, tn), F32)],
        compiler_params=_cparams(("parallel", "parallel", "arbitrary")),
        name="matmul_residual",
    )(a, w, res)


def _gdn_kernel(q_ref, k_ref, v_ref, z_ref, sm_ref, cbuf_ref, s0_ref, cw_ref, alog_ref, dt_ref, ng_ref,
                o_ref, sout_ref, xp_ref, st_ref, *, rows, nc, t_valid):
    C = GDN_CHUNK
    W = GDN_WIDTH
    c = pl.program_id(1)

    @pl.when(c == 0)
    def _():
        xp_ref[...] = jnp.zeros(xp_ref.shape, F32)
        xp_ref[SUBLANES - (CONV_W - 1):SUBLANES, :] = cbuf_ref[0]
        st_ref[...] = s0_ref[0]

    @pl.when(c > 0)
    def _():
        xp_ref[0:SUBLANES, :] = xp_ref[C:C + SUBLANES, :]

    xp_ref[SUBLANES:SUBLANES + rows, 0:W] = q_ref[...]
    xp_ref[SUBLANES:SUBLANES + rows, W:2 * W] = k_ref[...]
    xp_ref[SUBLANES:SUBLANES + rows, 2 * W:3 * W] = v_ref[...]

    row = lax.broadcasted_iota(jnp.int32, (C, LANES), 0)
    valid = (c * C + row) < t_valid
    if rows < C:
        sm = jnp.concatenate([sm_ref[...], jnp.zeros((C - rows, LANES), F32)], axis=0)
    else:
        sm = sm_ref[...]
    beta_all = jnp.where(valid, jax.nn.sigmoid(sm), 0.0)
    g_all = jnp.where(valid, -jnp.exp(alog_ref[...]) * _softplus(sm + dt_ref[...]), 0.0)
    ri = lax.broadcasted_iota(jnp.int32, (C, C), 0)
    ci = lax.broadcasted_iota(jnp.int32, (C, C), 1)
    tri = ri >= ci
    strict = ri > ci
    gcum = _dot(tri.astype(F32), g_all, HI)
    li = lax.broadcasted_iota(jnp.int32, (LANES, LANES), 0)
    lj = lax.broadcasted_iota(jnp.int32, (LANES, LANES), 1)
    gcum_t = _dot_nt((li == lj).astype(F32), gcum, HI)
    valid_col = valid[:, 0:1]
    eye = (ri == ci).astype(F32)
    n_sq = int(math.log2(C)) - 1

    def mm(a, b):
        return _dot(a.astype(BF16), b.astype(BF16))

    def mm_nt(a, b):
        return _dot_nt(a.astype(BF16), b.astype(BF16))

    def conv(col):
        acc = None
        for j in range(CONV_W):
            term = (xp_ref[pl.ds(SUBLANES - (CONV_W - 1) + j, C), col:col + LANES]
                    * cw_ref[j:j + 1, col:col + LANES])
            acc = term if acc is None else acc + term
        return _silu(acc)

    def head_group(hs):
        n = len(hs)
        qs = [conv(h * HEAD_DIM) for h in hs]
        ks = [conv(W + h * HEAD_DIM) for h in hs]
        vs = [conv(2 * W + h * HEAD_DIM) for h in hs]
        qs = [q * lax.rsqrt(jnp.sum(q * q, axis=-1, keepdims=True) + EPS) * SCALE for q in qs]
        ks = [jnp.where(valid_col, k * lax.rsqrt(jnp.sum(k * k, axis=-1, keepdims=True) + EPS), 0.0) for k in ks]
        gcs = [gcum[:, SMALL_A + h:SMALL_A + h + 1] for h in hs]
        bhs = [beta_all[:, SMALL_B + h:SMALL_B + h + 1] for h in hs]
        grs = [gcum_t[SMALL_A + h:SMALL_A + h + 1, :] for h in hs]
        decays = [jnp.where(tri, jnp.exp(jnp.where(tri, gc - gr, 0.0)), 0.0) for gc, gr in zip(gcs, grs)]
        egs = [jnp.exp(gc) for gc in gcs]

        kbs = [k * b for k, b in zip(ks, bhs)]
        kqs = [mm_nt(jnp.concatenate([kb, q], axis=0), k) for kb, q, k in zip(kbs, qs, ks)]
        nmats = [jnp.where(strict, -(kq[0:C] * d), 0.0) for kq, d in zip(kqs, decays)]
        aqks = [jnp.where(tri, kq[C:2 * C] * d, 0.0) for kq, d in zip(kqs, decays)]
        pinvs = [eye + nm for nm in nmats]
        npows = nmats
        for _ in range(n_sq):
            npows = [mm(np_, np_) for np_ in npows]
            pinvs = [p + mm(p, np_) for p, np_ in zip(pinvs, npows)]
        uws = [mm(p, jnp.concatenate([v * b, kb * eg], axis=1))
               for p, v, b, kb, eg in zip(pinvs, vs, bhs, kbs, egs)]

        ss = [st_ref[h] for h in hs]
        wss = [mm(jnp.concatenate([uw[:, HEAD_DIM:2 * HEAD_DIM], q * eg], axis=0), s)
               for uw, q, eg, s in zip(uws, qs, egs, ss)]
        v_news = [uw[:, 0:HEAD_DIM] - ws[0:C] for uw, ws in zip(uws, wss)]
        os_ = [ws[C:2 * C] + mm(aqk, vn) for ws, aqk, vn in zip(wss, aqks, v_news)]
        for i in range(n):
            g_last = gcs[i][C - 1:C, :]
            k_dec = ks[i] * jnp.exp(g_last - gcs[i])
            st_ref[hs[i]] = ss[i] * jnp.exp(g_last) + _dot_tn(k_dec.astype(BF16), v_news[i].astype(BF16))
        for i in range(n):
            col = hs[i] * HEAD_DIM
            o = _rms(os_[i], ng_ref[...])
            o_ref[:, col:col + HEAD_DIM] = (o[0:rows] * _silu(z_ref[:, col:col + HEAD_DIM])).astype(o_ref.dtype)

    for h0 in range(0, GDN_HEADS, GDN_HEAD_GROUP):
        head_group(list(range(h0, h0 + GDN_HEAD_GROUP)))

    @pl.when(c == nc - 1)
    def _():
        sout_ref[0] = st_ref[...]


def _gdn(proj, small, conv_buf, s0, conv_w, a_log, dt_bias, norm_g, *, bsz, rows_per_seq, t_valid):
    C = GDN_CHUNK
    if rows_per_seq >= C:
        assert rows_per_seq % C == 0
        rows, nc = C, rows_per_seq // C
    else:
        assert rows_per_seq % SUBLANES == 0
        rows, nc = rows_per_seq, 1
    wblk = GDN_WIDTH
    zeros = jnp.zeros((LANES - 2 * GDN_HEADS,), F32)
    alog_row = jnp.concatenate([jnp.zeros((GDN_HEADS,), F32), a_log, zeros]).reshape(1, LANES)
    dt_row = jnp.concatenate([jnp.zeros((GDN_HEADS,), F32), dt_bias, zeros]).reshape(1, LANES)
    row_map = lambda b, c: b * nc + c
    in_specs = [
        pl.BlockSpec((rows, wblk), lambda b, c: (row_map(b, c), COL_Q // wblk)),
        pl.BlockSpec((rows, wblk), lambda b, c: (row_map(b, c), COL_K // wblk)),
        pl.BlockSpec((rows, wblk), lambda b, c: (row_map(b, c), COL_V // wblk)),
        pl.BlockSpec((rows, wblk), lambda b, c: (row_map(b, c), COL_Z // wblk)),
        pl.BlockSpec((rows, LANES), lambda b, c: (row_map(b, c), 0)),
        pl.BlockSpec((1, CONV_W - 1, 3 * GDN_WIDTH), lambda b, c: (b, 0, 0)),
        pl.BlockSpec((1, GDN_HEADS, HEAD_DIM, HEAD_DIM), lambda b, c: (b, 0, 0, 0)),
        pl.BlockSpec((CONV_W, 3 * GDN_WIDTH), lambda b, c: (0, 0)),
        pl.BlockSpec((1, LANES), lambda b, c: (0, 0)),
        pl.BlockSpec((1, LANES), lambda b, c: (0, 0)),
        pl.BlockSpec((1, HEAD_DIM), lambda b, c: (0, 0)),
    ]
    out_specs = [
        pl.BlockSpec((rows, wblk), lambda b, c: (row_map(b, c), 0)),
        pl.BlockSpec((1, GDN_HEADS, HEAD_DIM, HEAD_DIM), lambda b, c: (b, 0, 0, 0)),
    ]
    return pl.pallas_call(
        functools.partial(_gdn_kernel, rows=rows, nc=nc, t_valid=t_valid),
        grid=(bsz, nc),
        in_specs=in_specs,
        out_specs=out_specs,
        out_shape=[jax.ShapeDtypeStruct((bsz * rows_per_seq, GDN_WIDTH), BF16),
                   jax.ShapeDtypeStruct((bsz, GDN_HEADS, HEAD_DIM, HEAD_DIM), F32)],
        scratch_shapes=[pltpu.VMEM((C + SUBLANES, 3 * GDN_WIDTH), F32),
                        pltpu.VMEM((GDN_HEADS, HEAD_DIM, HEAD_DIM), F32)],
        compiler_params=_cparams(("parallel", "arbitrary")),
        name="gdn_chunk_scan",
    )(proj, proj, proj, proj, small, conv_buf, s0, conv_w, alog_row, dt_row, norm_g.reshape(1, HEAD_DIM))


def _prep_kernel(nq_ref, cmp_ref, slc_ref, win_ref, sm_ref, qg_ref, kg_ref,
                 q_out, cmp_out, slc_out, win_out, gate_out, slc_bf_out, win_bf_out):
    tm = nq_ref.shape[0]
    qg = qg_ref[...]
    for h in range(NSA_HEADS):
        sl = slice(h * HEAD_DIM, (h + 1) * HEAD_DIM)
        q_out[:, sl] = _rms(nq_ref[:, sl], qg)
    for slab in range(KV_SLABS):
        sl = slice(slab * HEAD_DIM, (slab + 1) * HEAD_DIM)
        cmp_out[pl.ds(slab, tm, stride=KV_SLABS), :] = cmp_ref[:, sl]
    for src, dst, dst_bf, gi in ((slc_ref, slc_out, slc_bf_out, 1), (win_ref, win_out, win_bf_out, 2)):
        kg = kg_ref[gi:gi + 1, :]
        for slab in range(KV_SLABS):
            sl = slice(slab * HEAD_DIM, (slab + 1) * HEAD_DIM)
            x = _rms(src[:, sl], kg) if slab < NSA_KV_HEADS else src[:, sl]
            dst[pl.ds(slab, tm, stride=KV_SLABS), :] = x
            dst_bf[:, sl] = x.astype(BF16)
    sig = jax.nn.sigmoid(sm_ref[...])
    per = NSA_GROUP * 3
    for g in range(NSA_KV_HEADS):
        gate_out[g] = pltpu.roll(sig, LANES - (SMALL_GATE + per * g), axis=1)


def _nsa_prep(proj, small, q_norm_g, k_norm_g, *, tm):
    m = proj.shape[0]
    assert m % tm == 0
    return pl.pallas_call(
        _prep_kernel,
        grid=(m // tm,),
        in_specs=[pl.BlockSpec((tm, NSA_WIDTH), lambda i: (i, COL_NQ // NSA_WIDTH)),
                  pl.BlockSpec((tm, KV_WIDTH), lambda i: (i, COL_CMP // KV_WIDTH)),
                  pl.BlockSpec((tm, KV_WIDTH), lambda i: (i, COL_SLC // KV_WIDTH)),
                  pl.BlockSpec((tm, KV_WIDTH), lambda i: (i, COL_WIN // KV_WIDTH)),
                  pl.BlockSpec((tm, LANES), lambda i: (i, 0)),
                  pl.BlockSpec((1, HEAD_DIM), lambda i: (0, 0)),
                  pl.BlockSpec((3, HEAD_DIM), lambda i: (0, 0))],
        out_specs=[pl.BlockSpec((tm, NSA_WIDTH), lambda i: (i, 0)),
                   pl.BlockSpec((tm * KV_SLABS, HEAD_DIM), lambda i: (i, 0)),
                   pl.BlockSpec((tm * KV_SLABS, HEAD_DIM), lambda i: (i, 0)),
                   pl.BlockSpec((tm * KV_SLABS, HEAD_DIM), lambda i: (i, 0)),
                   pl.BlockSpec((NSA_KV_HEADS, tm, LANES), lambda i: (0, i, 0)),
                   pl.BlockSpec((tm, KV_WIDTH), lambda i: (i, 0)),
                   pl.BlockSpec((tm, KV_WIDTH), lambda i: (i, 0))],
        out_shape=[jax.ShapeDtypeStruct((m, NSA_WIDTH), F32),
                   jax.ShapeDtypeStruct((m * KV_SLABS, HEAD_DIM), F32),
                   jax.ShapeDtypeStruct((m * KV_SLABS, HEAD_DIM), F32),
                   jax.ShapeDtypeStruct((m * KV_SLABS, HEAD_DIM), F32),
                   jax.ShapeDtypeStruct((NSA_KV_HEADS, m, LANES), F32),
                   jax.ShapeDtypeStruct((m, KV_WIDTH), BF16),
                   jax.ShapeDtypeStruct((m, KV_WIDTH), BF16)],
        compiler_params=_cparams(("parallel",)),
        name="nsa_prep",
    )(proj, proj, proj, proj, small, q_norm_g.reshape(1, HEAD_DIM), k_norm_g)


def _pool_kernel(tbl_ref, *refs, pg):
    del tbl_ref
    page_refs = refs[:pg]
    w_ref, s0_ref, s1_ref = refs[pg:]
    sub = PAGE_SIZE // CMP_STRIDE
    out_rows = sub * KV_SLABS
    for u in range(pg):
        x = page_refs[u][...].reshape(sub, CMP_STRIDE, KV_SLABS, HEAD_DIM)
        s0_ref[0, u * out_rows:(u + 1) * out_rows, :] = jnp.sum(x * w_ref[0][None], axis=1).reshape(out_rows, HEAD_DIM)
        s1_ref[0, u * out_rows:(u + 1) * out_rows, :] = jnp.sum(x * w_ref[1][None], axis=1).reshape(out_rows, HEAD_DIM)


def _pool(rows2d, table, wexp):
    bsz, n_pages = table.shape
    pg = math.gcd(n_pages, 16)
    sub = PAGE_SIZE // CMP_STRIDE
    page_rows = PAGE_SIZE * KV_SLABS
    out_rows = pg * sub * KV_SLABS

    def page_spec(u):
        return pl.BlockSpec((page_rows, HEAD_DIM), lambda b, p, tbl: (tbl[b, p * pg + u], 0))

    grid_spec = pltpu.PrefetchScalarGridSpec(
        num_scalar_prefetch=1,
        grid=(bsz, n_pages // pg),
        in_specs=[page_spec(u) for u in range(pg)]
        + [pl.BlockSpec((2, CMP_STRIDE, KV_SLABS, HEAD_DIM), lambda b, p, tbl: (0, 0, 0, 0))],
        out_specs=[pl.BlockSpec((1, out_rows, HEAD_DIM), lambda b, p, tbl: (b, p, 0)),
                   pl.BlockSpec((1, out_rows, HEAD_DIM), lambda b, p, tbl: (b, p, 0))],
    )
    shape = jax.ShapeDtypeStruct((bsz, n_pages * sub * KV_SLABS, HEAD_DIM), F32)
    return pl.pallas_call(
        functools.partial(_pool_kernel, pg=pg), grid_spec=grid_spec, out_shape=[shape, shape],
        compiler_params=_cparams(("parallel", "arbitrary")), name="cmp_pool",
    )(table, *([rows2d] * pg), wexp)


def _cmp_fin_kernel(s0_ref, s1_ref, phi_ref, kg_ref, kc_ref, vc_ref, *, n_cmp):
    nb = s0_ref.shape[1] // KV_SLABS
    live = lax.broadcasted_iota(jnp.int32, (nb, HEAD_DIM), 0) < n_cmp
    zero_row = jnp.zeros((1, HEAD_DIM), F32)

    def pooled(slab):
        first = s0_ref[0, pl.ds(slab, nb, stride=KV_SLABS), :]
        second = s1_ref[0, pl.ds(KV_SLABS + slab, nb - 1, stride=KV_SLABS), :]
        return first + jnp.concatenate([second, zero_row], axis=0)

    groups = range(NSA_KV_HEADS)
    ks = [_dot(pooled(g), phi_ref[0, g], HI) for g in groups]
    vs = [_dot(pooled(NSA_KV_HEADS + g), phi_ref[1, g], HI) for g in groups]
    for g in groups:
        kc_ref[0, g] = jnp.where(live, _rms(ks[g], kg_ref[...]), 0.0)
        vc_ref[0, g] = jnp.where(live, vs[g], 0.0)


def _cmp_finish(s0, s1, phi, kg, *, n_cmp):
    bsz = s0.shape[0]
    nb = s0.shape[1] // KV_SLABS
    assert n_cmp <= nb - 1
    shape = jax.ShapeDtypeStruct((bsz, NSA_KV_HEADS, nb, HEAD_DIM), F32)
    return pl.pallas_call(
        functools.partial(_cmp_fin_kernel, n_cmp=n_cmp),
        grid=(bsz,),
        in_specs=[pl.BlockSpec((1, nb * KV_SLABS, HEAD_DIM), lambda b: (b, 0, 0)),
                  pl.BlockSpec((1, nb * KV_SLABS, HEAD_DIM), lambda b: (b, 0, 0)),
                  pl.BlockSpec((2, NSA_KV_HEADS, HEAD_DIM, HEAD_DIM), lambda b: (0, 0, 0, 0)),
                  pl.BlockSpec((1, HEAD_DIM), lambda b: (0, 0))],
        out_specs=[pl.BlockSpec((1, NSA_KV_HEADS, nb, HEAD_DIM), lambda b: (b, 0, 0, 0)),
                   pl.BlockSpec((1, NSA_KV_HEADS, nb, HEAD_DIM), lambda b: (b, 0, 0, 0))],
        out_shape=[shape, shape],
        compiler_params=_cparams(("parallel",)),
        name="cmp_finish",
    )(s0, s1, phi, kg.reshape(1, HEAD_DIM))


def _stack_heads(ref, g, tq, dtype):
    parts = [ref[:, (g * NSA_GROUP + r) * HEAD_DIM:(g * NSA_GROUP + r + 1) * HEAD_DIM].astype(dtype)
             for r in range(NSA_GROUP)]
    return jnp.concatenate(parts, axis=0)


def _cmp_topk_kernel(q_ref, kc_ref, vc_ref, ov_ref, ocmp_ref, sel_ref, *, tq, n_cmp, n_slc, offset, token_lanes):
    i = pl.program_id(1)
    nb = kc_ref.shape[2]
    nsp = sel_ref.shape[2]
    rows = NSA_GROUP * tq
    r4 = lax.broadcasted_iota(jnp.int32, (rows, nb), 0)
    n4 = lax.broadcasted_iota(jnp.int32, (rows, nb), 1)
    tpos = offset + i * tq + (r4 & (tq - 1))
    valid = (n4 * CMP_STRIDE + (CMP_BLOCK - 1) <= tpos) & (n4 < n_cmp)

    if token_lanes:
        nsr = -(-n_slc // SUBLANES) * SUBLANES
        shape, blk_axis, tok_axis = (nsr, tq), 0, 1
    else:
        shape, blk_axis, tok_axis = (tq, nsp), 1, 0
    blk = lax.broadcasted_iota(jnp.int32, shape, blk_axis)
    tq_pos = offset + i * tq + lax.broadcasted_iota(jnp.int32, shape, tok_axis)
    cur = tq_pos // SLC_BLOCK
    forced = (blk == 0) | ((blk <= cur) & (blk > cur - SLC_LOCAL))
    future = blk > cur
    in_range = blk < n_slc
    topk = min(SLC_TOPK, n_slc)

    groups = range(NSA_KV_HEADS)
    ss = [jnp.where(valid, _dot_nt(_stack_heads(q_ref, g, tq, F32), kc_ref[0, g], HI) * SCALE, -jnp.inf)
          for g in groups]
    ms = [jnp.max(s, axis=-1, keepdims=True) for s in ss]
    ps = [jnp.exp(s - jnp.where(m == -jnp.inf, 0.0, m)) for s, m in zip(ss, ms)]
    ps = [p / jnp.maximum(jnp.sum(p, axis=-1, keepdims=True), jnp.finfo(jnp.float32).tiny) for p in ps]
    os_ = [_dot(p.astype(BF16), vc_ref[0, g].astype(BF16)) for g, p in zip(groups, ps)]
    psums = [functools.reduce(jnp.add, [p[r * tq:(r + 1) * tq] for r in range(NSA_GROUP)]) for p in ps]
    if token_lanes:
        imps = [_dot_nt(ov_ref[0:shape[0], :], psum, HI) for psum in psums]
    else:
        imps = [_dot(psum, ov_ref[...], HI) for psum in psums]
    scores = [jnp.where(in_range, jnp.where(forced, FORCE_SCORE, jnp.where(future, -FORCE_SCORE, imp)), -jnp.inf)
              for imp in imps]
    ranks = [jnp.zeros(shape, jnp.int32) for _ in groups]
    for j in range(n_slc):
        for g in groups:
            cj = scores[g][j:j + 1, :] if token_lanes else scores[g][:, j:j + 1]
            ahead = (cj > scores[g]) | ((cj == scores[g]) & (blk > j))
            ranks[g] = ranks[g] + ahead.astype(jnp.int32)
    for g in groups:
        for r in range(NSA_GROUP):
            col = (g * NSA_GROUP + r) * HEAD_DIM
            ocmp_ref[:, col:col + HEAD_DIM] = os_[g][r * tq:(r + 1) * tq]
        sel = ((ranks[g] < topk) & in_range).astype(F32)
        if token_lanes:
            sel = jnp.concatenate([sel, jnp.zeros((nsp - shape[0], tq), F32)], axis=0).T
        sel_ref[g] = sel


def _cmp_topk(qn, kc, vc, overlap, *, bsz, rows_per_seq, tq, n_cmp, n_slc, offset):
    m = qn.shape[0]
    nt = rows_per_seq // tq
    nb = kc.shape[2]
    nsp = overlap.shape[1]
    token_lanes = tq % LANES == 0 and nsp == tq
    if token_lanes:
        overlap = overlap.T
    return pl.pallas_call(
        functools.partial(_cmp_topk_kernel, tq=tq, n_cmp=n_cmp, n_slc=n_slc, offset=offset,
                          token_lanes=token_lanes),
        grid=(bsz, nt),
        in_specs=[pl.BlockSpec((tq, NSA_WIDTH), lambda b, i: (b * nt + i, 0)),
                  pl.BlockSpec((1, NSA_KV_HEADS, nb, HEAD_DIM), lambda b, i: (b, 0, 0, 0)),
                  pl.BlockSpec((1, NSA_KV_HEADS, nb, HEAD_DIM), lambda b, i: (b, 0, 0, 0)),
                  pl.BlockSpec(overlap.shape, lambda b, i: (0, 0))],
        out_specs=[pl.BlockSpec((tq, NSA_WIDTH), lambda b, i: (b * nt + i, 0)),
                   pl.BlockSpec((NSA_KV_HEADS, tq, nsp), lambda b, i: (0, b * nt + i, 0))],
        out_shape=[jax.ShapeDtypeStruct((m, NSA_WIDTH), F32),
                   jax.ShapeDtypeStruct((NSA_KV_HEADS, m, nsp), F32)],
        compiler_params=_cparams(("parallel", "arbitrary")),
        name="cmp_attn_topk",
    )(qn, kc, vc, overlap)


def _overlap_matrix(nb, nsp, n_cmp, n_slc):
    cs = np.arange(nb) * CMP_STRIDE
    ss = np.arange(nsp) * SLC_BLOCK
    lo = np.maximum(cs[:, None], ss[None, :])
    hi = np.minimum(cs[:, None] + CMP_BLOCK, ss[None, :] + SLC_BLOCK)
    ov = (np.maximum(hi - lo, 0) / CMP_BLOCK).astype(np.float32)
    ov[n_cmp:, :] = 0.0
    ov[:, n_slc:] = 0.0
    return jnp.asarray(ov)


def _flash_init(m_ref, l_ref, acc_ref):
    m_ref[...] = jnp.full(m_ref.shape, NEG, F32)
    l_ref[...] = jnp.zeros(l_ref.shape, F32)
    acc_ref[...] = jnp.zeros(acc_ref.shape, F32)


def _lane_rep(col, width):
    tile = jnp.broadcast_to(col, (col.shape[0], LANES))
    return tile if width == LANES else jnp.concatenate([tile] * (width // LANES), axis=1)


def _flash_update(s, mask, v, m_ref, l_ref, acc_ref):
    rows, tk = s.shape
    reps = tk // LANES
    s = jnp.where(mask, s, NEG)
    m_prev = m_ref[...]
    m_new = jnp.maximum(m_prev, jnp.broadcast_to(jnp.max(s, axis=-1, keepdims=True), (rows, LANES)))
    alpha = jnp.exp(m_prev - m_new)
    p = jnp.exp(s - jnp.concatenate([m_new] * reps, axis=1))
    l_ref[...] = alpha * l_ref[...] + jnp.broadcast_to(jnp.sum(p, axis=-1, keepdims=True), (rows, LANES))
    acc_ref[...] = alpha * acc_ref[...] + _dot(p.astype(v.dtype), v)
    m_ref[...] = m_new


def _pattn_kernel(q_ref, ks_ref, vs_ref, kw_ref, vw_ref, ocmp_ref, sel_ref, gate_ref, exp_ref, o_ref,
                  sexp_ref, m_ref, l_ref, acc_ref, *, tq, tk, wlen):
    i = pl.program_id(2)
    t = ks_ref.shape[0]
    q4 = jnp.concatenate([(q_ref[:, r * HEAD_DIM:(r + 1) * HEAD_DIM] * SCALE).astype(BF16)
                          for r in range(NSA_GROUP)], axis=0)
    sexp_ref[...] = _dot(sel_ref[0].astype(BF16), exp_ref[...])
    qpos = i * tq + lax.broadcasted_iota(jnp.int32, (tq, tk), 0)
    lane = lax.broadcasted_iota(jnp.int32, (tq, tk), 1)

    def tile4(x):
        return jnp.concatenate([x] * NSA_GROUP, axis=0)

    def slc_step(j, carry):
        start = pl.multiple_of(j * tk, tk)
        s = _dot_nt(q4, ks_ref[pl.ds(start, tk), :])
        mask = (sexp_ref[:, pl.ds(start, tk)] > 0.5) & (start + lane <= qpos)
        _flash_update(s, tile4(mask), vs_ref[pl.ds(start, tk), :], m_ref, l_ref, acc_ref)
        return carry

    _flash_init(m_ref, l_ref, acc_ref)
    lax.fori_loop(0, ((i + 1) * tq + tk - 1) // tk, slc_step, 0)
    o_slc = acc_ref[...] / l_ref[...]

    wstart = pl.multiple_of(jnp.clip((i + 1) * tq - wlen, 0, t - wlen), tq)
    s = _dot_nt(q4, kw_ref[pl.ds(wstart, wlen), :])
    wdiff = (i * tq + lax.broadcasted_iota(jnp.int32, (tq, wlen), 0)
             - (wstart + lax.broadcasted_iota(jnp.int32, (tq, wlen), 1)))
    s = jnp.where(tile4((wdiff >= 0) & (wdiff < WINDOW)), s, NEG)
    p = jnp.exp(s - _lane_rep(jnp.max(s, axis=-1, keepdims=True), wlen))
    o_win = _dot(p.astype(BF16), vw_ref[pl.ds(wstart, wlen), :]) / _lane_rep(jnp.sum(p, axis=-1, keepdims=True), HEAD_DIM)

    gates = gate_ref[0]
    for r in range(NSA_GROUP):
        sl = slice(r * tq, (r + 1) * tq)
        col = slice(r * HEAD_DIM, (r + 1) * HEAD_DIM)
        o = (gates[:, 3 * r:3 * r + 1] * ocmp_ref[:, col]
             + gates[:, 3 * r + 1:3 * r + 2] * o_slc[sl]
             + gates[:, 3 * r + 2:3 * r + 3] * o_win[sl])
        o_ref[:, col] = o.astype(o_ref.dtype)


def _prompt_attn(qn, slc_bf, win_bf, ocmp, sel, gates, *, bsz, t, tq):
    m = qn.shape[0]
    nt = t // tq
    gw = NSA_GROUP * HEAD_DIM
    half = NSA_KV_HEADS
    nsp = sel.shape[2]
    n_slc = -(-t // SLC_BLOCK)
    expand = np.zeros((nsp, t), np.float32)
    expand[np.arange(t) // SLC_BLOCK, np.arange(t)] = 1.0
    assert n_slc <= nsp
    tk = next(c for c in (512, 256, 128) if t % c == 0 and c >= tq)
    wlen = min(t, WINDOW + tq)
    assert wlen % tq == 0 and t % tq == 0
    kv_spec = lambda which: pl.BlockSpec((t, HEAD_DIM), lambda b, g, i: (b, which * half + g))
    return pl.pallas_call(
        functools.partial(_pattn_kernel, tq=tq, tk=tk, wlen=wlen),
        grid=(bsz, NSA_KV_HEADS, nt),
        in_specs=[pl.BlockSpec((tq, gw), lambda b, g, i: (b * nt + i, g)),
                  kv_spec(0), kv_spec(1), kv_spec(0), kv_spec(1),
                  pl.BlockSpec((tq, gw), lambda b, g, i: (b * nt + i, g)),
                  pl.BlockSpec((1, tq, nsp), lambda b, g, i: (g, b * nt + i, 0)),
                  pl.BlockSpec((1, tq, LANES), lambda b, g, i: (g, b * nt + i, 0)),
                  pl.BlockSpec((nsp, t), lambda b, g, i: (0, 0))],
        out_specs=pl.BlockSpec((tq, gw), lambda b, g, i: (b * nt + i, g)),
        out_shape=jax.ShapeDtypeStruct((m, NSA_WIDTH), BF16),
        scratch_shapes=[pltpu.VMEM((tq, t), F32),
                        pltpu.VMEM((NSA_GROUP * tq, LANES), F32),
                        pltpu.VMEM((NSA_GROUP * tq, LANES), F32),
                        pltpu.VMEM((NSA_GROUP * tq, HEAD_DIM), F32)],
        compiler_params=_cparams(("parallel", "parallel", "arbitrary")),
        name="nsa_prompt_attn",
    )(qn, slc_bf, slc_bf, win_bf, win_bf, ocmp, sel, gates, jnp.asarray(expand, BF16))


def _sattn_kernel(tbl_ref, q_ref, *refs, tq, n_steps, pg, past):
    del tbl_ref
    page_refs = refs[:pg]
    (snew_ref, wcache_ref, wnew_ref, ocmp_ref, sel_ref, gate_ref, exp_ref, o_ref, s_buf, v_buf) = refs[pg:]
    j = pl.program_id(1)
    rows = NSA_GROUP * tq
    nsp = sel_ref.shape[2]
    n_prev = wcache_ref.shape[0] // KV_SLABS

    def tile4(x):
        return jnp.concatenate([x] * NSA_GROUP, axis=0)

    q4_bf = [_stack_heads(q_ref, g, tq, BF16) for g in range(NSA_KV_HEADS)]
    for u in range(pg):
        key0 = pl.multiple_of((j * pg + u) * PAGE_SIZE, PAGE_SIZE)
        for g in range(NSA_KV_HEADS):
            k = page_refs[u][pl.ds(g, PAGE_SIZE, stride=KV_SLABS), :].astype(BF16)
            s_buf[g * rows:(g + 1) * rows, pl.ds(key0, PAGE_SIZE)] = _dot_nt(q4_bf[g], k)
            v_buf[g, pl.ds(key0, PAGE_SIZE), :] = (
                page_refs[u][pl.ds(NSA_KV_HEADS + g, PAGE_SIZE, stride=KV_SLABS), :].astype(BF16))

    @pl.when(j == n_steps - 1)
    def _():
        blk_lane = lax.broadcasted_iota(jnp.int32, (tq, nsp), 1)
        trow = lax.broadcasted_iota(jnp.int32, (tq, tq), 0)
        tcol = lax.broadcasted_iota(jnp.int32, (tq, tq), 1)
        causal = tcol <= trow
        wrow = lax.broadcasted_iota(jnp.int32, (tq, n_prev), 0)
        wcol = lax.broadcasted_iota(jnp.int32, (tq, n_prev), 1)
        wdiff = wrow + n_prev - wcol
        wmask = (wdiff >= 0) & (wdiff < WINDOW)
        ndiff = trow - tcol
        nmask = (ndiff >= 0) & (ndiff < WINDOW)
        gates_all = gate_ref[...]
        groups = range(NSA_KV_HEADS)
        q4s = [_stack_heads(q_ref, g, tq, F32) for g in groups]

        def joint_softmax(scs, sns, vcs, vns):
            mxs = [jnp.maximum(jnp.max(sc, axis=-1, keepdims=True), jnp.max(sn, axis=-1, keepdims=True))
                   for sc, sn in zip(scs, sns)]
            pcs = [jnp.exp(sc - mx) for sc, mx in zip(scs, mxs)]
            pns = [jnp.exp(sn - mx) for sn, mx in zip(sns, mxs)]
            dens = [jnp.sum(pc, axis=-1, keepdims=True) + jnp.sum(pn, axis=-1, keepdims=True)
                    for pc, pn in zip(pcs, pns)]
            return [(_dot(pc.astype(BF16), vc) + _dot(pn, vn, HI)) / den
                    for pc, pn, vc, vn, den in zip(pcs, pns, vcs, vns, dens)]

        flags = []
        for g in groups:
            flag = jnp.zeros((tq, tq), jnp.bool_)
            for u in range(tq):
                col = jnp.sum(jnp.where(blk_lane == (past + u) // SLC_BLOCK, sel_ref[g], 0.0),
                              axis=1, keepdims=True) > 0.5
                flag = flag | (col & (tcol == u))
            flags.append(flag & causal)
        scs = [jnp.where(tile4(_dot(sel_ref[g].astype(BF16), exp_ref[...]) > 0.5),
                         s_buf[g * rows:(g + 1) * rows, :] * SCALE, NEG) for g in groups]
        sns = [jnp.where(tile4(flags[g]),
                         _dot_nt(q4s[g], snew_ref[pl.ds(g, tq, stride=KV_SLABS), :], HI) * SCALE, NEG)
               for g in groups]
        o_slcs = joint_softmax(scs, sns, [v_buf[g] for g in groups],
                               [snew_ref[pl.ds(NSA_KV_HEADS + g, tq, stride=KV_SLABS), :] for g in groups])

        scs = [jnp.where(tile4(wmask), _dot_nt(q4s[g].astype(BF16),
                                               wcache_ref[pl.ds(g, n_prev, stride=KV_SLABS), :].astype(BF16)) * SCALE,
                         NEG) for g in groups]
        sns = [jnp.where(tile4(nmask),
                         _dot_nt(q4s[g], wnew_ref[pl.ds(g, tq, stride=KV_SLABS), :], HI) * SCALE, NEG)
               for g in groups]
        o_wins = joint_softmax(
            scs, sns,
            [wcache_ref[pl.ds(NSA_KV_HEADS + g, n_prev, stride=KV_SLABS), :].astype(BF16) for g in groups],
            [wnew_ref[pl.ds(NSA_KV_HEADS + g, tq, stride=KV_SLABS), :] for g in groups])

        for g in groups:
            gates = gates_all[g]
            for r in range(NSA_GROUP):
                sl = slice(r * tq, (r + 1) * tq)
                col = slice((g * NSA_GROUP + r) * HEAD_DIM, (g * NSA_GROUP + r + 1) * HEAD_DIM)
                o = (gates[:, 3 * r:3 * r + 1] * ocmp_ref[:, col]
                     + gates[:, 3 * r + 1:3 * r + 2] * o_slcs[g][sl]
                     + gates[:, 3 * r + 2:3 * r + 3] * o_wins[g][sl])
                o_ref[:, col] = o.astype(o_ref.dtype)


def _sample_attn(qn, cache2d, page_table, slc_new, win_cache2d, win_row0, win_new, ocmp, sel, gates,
                 *, bsz, tq, past, n_prev):
    n_pages = page_table.shape[1]
    pg = math.gcd(n_pages, 16)
    n_steps = n_pages // pg
    nsp = sel.shape[2]
    rows = NSA_GROUP * tq
    page_rows = PAGE_SIZE * KV_SLABS
    expand = np.zeros((nsp, past), np.float32)
    expand[np.arange(past) // SLC_BLOCK, np.arange(past)] = 1.0

    def page_spec(u):
        return pl.BlockSpec((page_rows, HEAD_DIM), lambda b, j, tbl: (tbl[b, j * pg + u], 0))

    grid_spec = pltpu.PrefetchScalarGridSpec(
        num_scalar_prefetch=1,
        grid=(bsz, n_steps),
        in_specs=[pl.BlockSpec((tq, NSA_WIDTH), lambda b, j, tbl: (b, 0))]
        + [page_spec(u) for u in range(pg)]
        + [pl.BlockSpec((tq * KV_SLABS, HEAD_DIM), lambda b, j, tbl: (b, 0)),
           pl.BlockSpec((n_prev * KV_SLABS, HEAD_DIM), lambda b, j, tbl: (win_row0 + b, 0)),
           pl.BlockSpec((tq * KV_SLABS, HEAD_DIM), lambda b, j, tbl: (b, 0)),
           pl.BlockSpec((tq, NSA_WIDTH), lambda b, j, tbl: (b, 0)),
           pl.BlockSpec((NSA_KV_HEADS, tq, nsp), lambda b, j, tbl: (0, b, 0)),
           pl.BlockSpec((NSA_KV_HEADS, tq, LANES), lambda b, j, tbl: (0, b, 0)),
           pl.BlockSpec((nsp, past), lambda b, j, tbl: (0, 0))],
        out_specs=pl.BlockSpec((tq, NSA_WIDTH), lambda b, j, tbl: (b, 0)),
        scratch_shapes=[pltpu.VMEM((NSA_KV_HEADS * rows, past), F32),
                        pltpu.VMEM((NSA_KV_HEADS, past, HEAD_DIM), BF16)],
    )
    return pl.pallas_call(
        functools.partial(_sattn_kernel, tq=tq, n_steps=n_steps, pg=pg, past=past),
        grid_spec=grid_spec,
        out_shape=jax.ShapeDtypeStruct((bsz * tq, NSA_WIDTH), BF16),
        compiler_params=_cparams(("parallel", "arbitrary")),
        name="nsa_sample_attn",
    )(page_table, qn, *([cache2d] * pg), slc_new, win_cache2d, win_new, ocmp, sel, gates,
      jnp.asarray(expand, BF16))


def _split_w_in(w_in):
    wt = w_in.T
    pad = jnp.zeros((LANES - 2 * GDN_HEADS - 3 * NSA_HEADS, wt.shape[1]), w_in.dtype)
    wt_small = jnp.concatenate([wt[W_IN_B0:W_IN_N0], wt[W_IN_G0:], pad], axis=0)
    return wt, wt[W_IN_N0:W_IN_G0], wt_small


def _in_proj(x2d, p, *, tm):
    xn = _rms_cast(x2d, p["attn_norm_g"], tm=min(tm, 512))
    w_full, w_b, w_small = p["w_in"]
    proj_a = _mm_nw([xn], w_full, n=PROJ_A_WIDTH, tm=tm, tn=512, w_t=True)
    proj_b = _mm_nw([xn], w_b, n=PROJ_B_WIDTH, tm=tm, tn=512, w_t=True)
    small = _mm_nw([xn], w_small, n=LANES, tm=tm, tn=LANES, w_t=True)
    return proj_a, proj_b, small


def _pos_weights(cmp_pos_w):
    w = cmp_pos_w.reshape(2, CMP_BLOCK // CMP_STRIDE, CMP_STRIDE, NSA_KV_HEADS)
    w = jnp.transpose(w, (1, 2, 0, 3))
    return jnp.broadcast_to(w[..., None], w.shape + (HEAD_DIM,)).reshape(2, CMP_STRIDE, KV_SLABS, HEAD_DIM)


def _mixers_out(x2d, o_gdn, o_nsa, p, *, tm):
    w_o, w_up = p["w_o"], p["w_up"]
    x1 = _mm_nw([o_gdn, o_nsa], w_o, n=w_o.shape[1], tm=tm, tn=512, res=x2d)
    xn = _rms_cast(x1, p["mlp_norm_g"], tm=min(tm, 512))
    if "w_down_bf" not in p:
        hid, p["w_down_bf"] = _mm_nw([xn], w_up, n=w_up.shape[1], tm=tm, tn=512, relu2=True, out_dtype=BF16,
                                     cast_side=p["w_down"])
    else:
        hid = _mm_nw([xn], w_up, n=w_up.shape[1], tm=tm, tn=512, relu2=True, out_dtype=BF16)
    w_down = p["w_down_bf"]
    return _mm_res(hid, w_down, x1, tm=tm, tn=min(1024, w_down.shape[1]), tk=min(2048, w_down.shape[0]))


def _round_up(x, n):
    return -(-x // n) * n


def _prompt_layer(x, p):
    bsz, t, d = x.shape
    m = bsz * t
    x2d = x.reshape(m, d)
    proj_a, proj_b, small = _in_proj(x2d, p, tm=1024)
    conv0 = jnp.zeros((bsz, CONV_W - 1, 3 * GDN_WIDTH), F32)
    s_zero = jnp.zeros((bsz, GDN_HEADS, HEAD_DIM, HEAD_DIM), F32)
    o_gdn, s_new = _gdn(proj_a, small, conv0, s_zero, p["gdn_conv_w"], p["gdn_a_log"], p["gdn_dt_bias"],
                        p["gdn_norm_g"], bsz=bsz, rows_per_seq=t, t_valid=t)
    qn, cmp_n, slc_n, win_n, gates, slc_bf, win_bf = _nsa_prep(proj_b, small, p["q_norm_g"], p["k_norm_g"], tm=512)

    n_cmp = (t - CMP_BLOCK) // CMP_STRIDE + 1
    n_slc = -(-t // SLC_BLOCK)
    n_pages = t // PAGE_SIZE
    table = jnp.arange(bsz * n_pages, dtype=jnp.int32).reshape(bsz, n_pages)
    kv_tail = (2, NSA_KV_HEADS, HEAD_DIM)
    s0, s1 = _pool(cmp_n, table, p["cmp_w"])
    kc, vc = _cmp_finish(s0, s1, p["cmp_phi"], p["k_norm_g"][0], n_cmp=n_cmp)
    nb = kc.shape[2]
    nsp = _round_up(n_slc, LANES)
    tq = 128
    ocmp, sel = _cmp_topk(qn, kc, vc, _overlap_matrix(nb, nsp, n_cmp, n_slc), bsz=bsz, rows_per_seq=t, tq=tq,
                          n_cmp=n_cmp, n_slc=n_slc, offset=0)
    o_nsa = _prompt_attn(qn, slc_bf, win_bf, ocmp, sel, gates, bsz=bsz, t=t, tq=tq)
    y = _mixers_out(x2d, o_gdn, o_nsa, p, tm=1024)

    keep = min(WINDOW, t)
    return (y.reshape(bsz, t, d),
            cmp_n.reshape((bsz, t) + kv_tail),
            slc_n.reshape((bsz, t) + kv_tail),
            win_n.reshape((bsz, t) + kv_tail)[:, t - keep:],
            proj_a.reshape(bsz, t, PROJ_A_WIDTH)[:, t - (CONV_W - 1):, 0:3 * GDN_WIDTH],
            s_new)


def _sample_layer(x, layer, cache_cmp_kv, cache_slc_kv, cache_win_kv, conv_buf, s0_state, page_table, p):
    bsz, t, d = x.shape
    tp = _round_up(t, SUBLANES)
    n_pages = page_table.shape[1]
    past = n_pages * PAGE_SIZE
    assert t <= SLC_BLOCK and past % SLC_BLOCK == 0 and t >= CONV_W - 1
    x2d = jnp.pad(x, ((0, 0), (0, tp - t), (0, 0))).reshape(bsz * tp, d)
    proj_a, proj_b, small = _in_proj(x2d, p, tm=bsz * tp)
    o_gdn, s_new = _gdn(proj_a, small, conv_buf, s0_state, p["gdn_conv_w"], p["gdn_a_log"], p["gdn_dt_bias"],
                        p["gdn_norm_g"], bsz=bsz, rows_per_seq=tp, t_valid=t)
    qn, cmp_n, slc_n, win_n, gates, _, _ = _nsa_prep(proj_b, small, p["q_norm_g"], p["k_norm_g"], tm=bsz * tp)

    total = past + t
    n_cmp = (total - CMP_BLOCK) // CMP_STRIDE + 1
    n_slc = -(-total // SLC_BLOCK)
    assert (n_cmp - 1) * CMP_STRIDE + CMP_BLOCK <= past
    depth, n_phys = cache_cmp_kv.shape[:2]
    table = page_table + layer * n_phys
    cache_rows = depth * n_phys * PAGE_SIZE * KV_SLABS
    s0, s1 = _pool(cache_cmp_kv.reshape(cache_rows, HEAD_DIM), table, p["cmp_w"])
    kc, vc = _cmp_finish(s0, s1, p["cmp_phi"], p["k_norm_g"][0], n_cmp=n_cmp)
    nb = kc.shape[2]
    nsp = _round_up(n_slc, LANES)
    ocmp, sel = _cmp_topk(qn, kc, vc, _overlap_matrix(nb, nsp, n_cmp, n_slc), bsz=bsz, rows_per_seq=tp, tq=tp,
                          n_cmp=n_cmp, n_slc=n_slc, offset=past)
    n_prev = cache_win_kv.shape[2]
    o_nsa = _sample_attn(qn, cache_slc_kv.reshape(cache_rows, HEAD_DIM), table, slc_n,
                         cache_win_kv.reshape(depth * bsz * n_prev * KV_SLABS, HEAD_DIM), layer * bsz,
                         win_n, ocmp, sel, gates, bsz=bsz, tq=tp, past=past, n_prev=n_prev)
    y = _mixers_out(x2d, o_gdn, o_nsa, p, tm=bsz * tp)

    kv_tail = (2, NSA_KV_HEADS, HEAD_DIM)
    win_all = jnp.concatenate([cache_win_kv[layer], win_n.reshape((bsz, tp) + kv_tail)[:, :t]], axis=1)
    keep = min(WINDOW, n_prev + t)
    return (y.reshape(bsz, tp, d)[:, :t],
            cmp_n.reshape((bsz, tp) + kv_tail)[:, :t],
            slc_n.reshape((bsz, tp) + kv_tail)[:, :t],
            win_all[:, n_prev + t - keep:],
            proj_a.reshape(bsz, tp, PROJ_A_WIDTH)[:, t - (CONV_W - 1):t, 0:3 * GDN_WIDTH],
            s_new)


def kernel(x_prompt, x_sample, cache_cmp_kv, cache_slc_kv, cache_win_kv, cache_gdn_conv, state_gdn, page_table, attn_norm_g, w_in, gdn_conv_w, gdn_a_log, gdn_dt_bias, gdn_norm_g, q_norm_g, k_norm_g, cmp_pos_w, cmp_phi, w_o, mlp_norm_g, w_up, w_down):
    depth = w_in.shape[0]
    yp, ys = x_prompt, x_sample
    per_layer = []
    for layer in range(depth):
        p = {
            "attn_norm_g": attn_norm_g[layer], "w_in": _split_w_in(w_in[layer]),
            "gdn_conv_w": gdn_conv_w[layer], "gdn_a_log": gdn_a_log[layer], "gdn_dt_bias": gdn_dt_bias[layer],
            "gdn_norm_g": gdn_norm_g[layer], "q_norm_g": q_norm_g[layer], "k_norm_g": k_norm_g[layer],
            "cmp_w": _pos_weights(cmp_pos_w[layer]), "cmp_phi": cmp_phi[layer],
            "w_o": w_o[layer], "mlp_norm_g": mlp_norm_g[layer],
            "w_up": w_up[layer], "w_down": w_down[layer],
        }
        yp, cmp_p, slc_p, win_p, conv_p, s_p = _prompt_layer(yp, p)
        ys, cmp_s, slc_s, win_s, conv_s, s_s = _sample_layer(
            ys, layer, cache_cmp_kv, cache_slc_kv, cache_win_kv, cache_gdn_conv[layer],
            state_gdn[layer], page_table, p)
        per_layer.append((cmp_p, cmp_s, slc_p, slc_s, win_p, win_s, conv_p, conv_s, s_p, s_s))
    st = [jnp.stack(z, axis=0) for z in zip(*per_layer)]
    return (yp, ys) + tuple(st)
```

```python
import functools
import math

import numpy as np
import jax
import jax.numpy as jnp
from jax import lax
from jax.experimental import pallas as pl
from jax.experimental.pallas import tpu as pltpu

F32 = jnp.float32
BF16 = jnp.bfloat16
HI = lax.Precision.HIGHEST

HEAD_DIM = 128
GDN_HEADS = 16
NSA_HEADS = 16
NSA_KV_HEADS = 4
NSA_GROUP = NSA_HEADS // NSA_KV_HEADS
GDN_WIDTH = GDN_HEADS * HEAD_DIM
NSA_WIDTH = NSA_HEADS * HEAD_DIM
KV_WIDTH = 2 * NSA_KV_HEADS * HEAD_DIM
KV_SLABS = 2 * NSA_KV_HEADS
CONV_W = 4
GDN_CHUNK = 64
GDN_HEAD_GROUP = 16
CMP_BLOCK = 32
CMP_STRIDE = 16
SLC_BLOCK = 64
SLC_TOPK = 16
SLC_LOCAL = 2
WINDOW = 512
PAGE_SIZE = 128
EPS = 1e-6
FORCE_SCORE = 1e9
NEG = -1e30
SCALE = HEAD_DIM ** -0.5

LANES = 128
SUBLANES = 8
VMEM_LIMIT = 52 * 1024 * 1024

COL_Q, COL_K, COL_V, COL_Z = 0, GDN_WIDTH, 2 * GDN_WIDTH, 3 * GDN_WIDTH
PROJ_A_WIDTH = 4 * GDN_WIDTH
COL_NQ = 0
COL_CMP = COL_NQ + NSA_WIDTH
COL_SLC = COL_CMP + KV_WIDTH
COL_WIN = COL_SLC + KV_WIDTH
PROJ_B_WIDTH = COL_WIN + KV_WIDTH
SMALL_B, SMALL_A, SMALL_GATE = 0, GDN_HEADS, 2 * GDN_HEADS
W_IN_B0 = PROJ_A_WIDTH
W_IN_N0 = W_IN_B0 + 2 * GDN_HEADS
W_IN_G0 = W_IN_N0 + PROJ_B_WIDTH


def _cparams(sem):
    return pltpu.CompilerParams(dimension_semantics=sem, vmem_limit_bytes=VMEM_LIMIT)


def _dot(a, b, prec=None):
    return jnp.dot(a, b, preferred_element_type=F32, precision=prec)


def _dot_nt(a, b, prec=None):
    return lax.dot_general(a, b, (((1,), (1,)), ((), ())), preferred_element_type=F32, precision=prec)


def _dot_tn(a, b, prec=None):
    return lax.dot_general(a, b, (((0,), (0,)), ((), ())), preferred_element_type=F32, precision=prec)


def _rms(x, g):
    return x * lax.rsqrt(jnp.mean(x * x, axis=-1, keepdims=True) + EPS) * g


def _silu(x):
    return x * jax.nn.sigmoid(x)


def _softplus(x):
    return jnp.maximum(x, 0.0) + jnp.log1p(jnp.exp(-jnp.abs(x)))


def _rms_cast_kernel(x_ref, g_ref, o_ref):
    o_ref[...] = _rms(x_ref[...], g_ref[...]).astype(o_ref.dtype)


def _rms_cast(x, g, *, tm):
    m, k = x.shape
    assert m % tm == 0
    return pl.pallas_call(
        _rms_cast_kernel,
        grid=(m // tm,),
        in_specs=[pl.BlockSpec((tm, k), lambda i: (i, 0)), pl.BlockSpec((1, k), lambda i: (0, 0))],
        out_specs=pl.BlockSpec((tm, k), lambda i: (i, 0)),
        out_shape=jax.ShapeDtypeStruct((m, k), BF16),
        compiler_params=_cparams(("parallel",)),
        name="rms_cast",
    )(x, g.reshape(1, k))


def _mm_nw_kernel(*refs, n_a, relu2, has_res, w_t, has_side):
    a_refs = refs[:n_a]
    w_ref = refs[n_a]
    r_ref = refs[n_a + 1] if has_res else None
    if has_side:
        side_ref, o_ref, side_out, wbf_ref = refs[-4:]
        side_out[...] = side_ref[...].astype(BF16)
    else:
        o_ref, wbf_ref = refs[-2:]

    @pl.when(pl.program_id(1) == 0)
    def _():
        w = w_ref[...]
        wbf_ref[...] = (w.T if w_t else w).astype(BF16)

    y, k0 = None, 0
    for a_ref in a_refs:
        part = _dot(a_ref[...], wbf_ref[k0:k0 + a_ref.shape[1], :])
        y = part if y is None else y + part
        k0 += a_ref.shape[1]
    if relu2:
        y = jnp.square(jnp.maximum(y, 0.0))
    if has_res:
        y = y + r_ref[...]
    o_ref[...] = y.astype(o_ref.dtype)


def _mm_nw(acts, w, *, n, tm, tn, res=None, relu2=False, out_dtype=F32, w_t=False, cast_side=None):
    m = acts[0].shape[0]
    k = sum(a.shape[1] for a in acts)
    assert m % tm == 0 and n % tn == 0 and w.shape[1 if w_t else 0] == k
    nm = m // tm
    in_specs = [pl.BlockSpec((tm, a.shape[1]), lambda j, i: (i, 0)) for a in acts]
    in_specs.append(pl.BlockSpec((tn, k), lambda j, i: (j, 0)) if w_t else pl.BlockSpec((k, tn), lambda j, i: (0, j)))
    args = list(acts) + [w]
    if res is not None:
        in_specs.append(pl.BlockSpec((tm, tn), lambda j, i: (i, j)))
        args.append(res)
    out_specs = pl.BlockSpec((tm, tn), lambda j, i: (i, j))
    out_shape = jax.ShapeDtypeStruct((m, n), out_dtype)
    if cast_side is not None:
        steps = (n // tn) * nm
        sr, sc = cast_side.shape
        assert sr % steps == 0 and (sr // steps) % 16 == 0
        side_spec = pl.BlockSpec((sr // steps, sc), lambda j, i: (j * nm + i, 0))
        in_specs.append(side_spec)
        args.append(cast_side)
        out_specs = [out_specs, side_spec]
        out_shape = [out_shape, jax.ShapeDtypeStruct((sr, sc), BF16)]
    return pl.pallas_call(
        functools.partial(_mm_nw_kernel, n_a=len(acts), relu2=relu2, has_res=res is not None, w_t=w_t,
                          has_side=cast_side is not None),
        grid=(n // tn, nm),
        in_specs=in_specs,
        out_specs=out_specs,
        out_shape=out_shape,
        scratch_shapes=[pltpu.VMEM((k, tn), BF16)],
        compiler_params=_cparams(("parallel", "arbitrary")),
        name="matmul_wcast",
    )(*args)


def _mm_res_kernel(a_ref, w_ref, r_ref, o_ref, acc_ref, *, nk):
    kk = pl.program_id(2)

    @pl.when(kk == 0)
    def _():
        acc_ref[...] = jnp.zeros_like(acc_ref)

    acc_ref[...] += _dot(a_ref[...], w_ref[...])

    @pl.when(kk == nk - 1)
    def _():
        o_ref[...] = r_ref[...] + acc_ref[...]


def _mm_res(a, w, res, *, tm, tn, tk):
    m, k = a.shape
    n = w.shape[1]
    assert m % tm == 0 and n % tn == 0 and k % tk == 0
    nk = k // tk
    return pl.pallas_call(
        functools.partial(_mm_res_kernel, nk=nk),
        grid=(m // tm, n // tn, nk),
        in_specs=[pl.BlockSpec((tm, tk), lambda i, j, kk: (i, kk)),
                  pl.BlockSpec((tk, tn), lambda i, j, kk: (kk, j)),
                  pl.BlockSpec((tm, tn), lambda i, j, kk: (i, j))],
        out_specs=pl.BlockSpec((tm, tn), lambda i, j, kk: (i, j)),
        out_shape=jax.ShapeDtypeStruct((m, n), F32),
        scratch_shapes=[pltpu.VMEM((tm, tn), F32)],
        compiler_params=_cparams(("parallel", "parallel", "arbitrary")),
        name="matmul_residual",
    )(a, w, res)


def _gdn_kernel(q_ref, k_ref, v_ref, z_ref, sm_ref, cbuf_ref, s0_ref, cw_ref, alog_ref, dt_ref, ng_ref,
                o_ref, sout_ref, xp_ref, st_ref, *, C, rows, nc, t_valid):
    W = GDN_WIDTH
    c = pl.program_id(1)

    @pl.when(c == 0)
    def _():
        xp_ref[...] = jnp.zeros(xp_ref.shape, F32)
        xp_ref[SUBLANES - (CONV_W - 1):SUBLANES, :] = cbuf_ref[0]
        st_ref[...] = s0_ref[0]

    @pl.when(c > 0)
    def _():
        xp_ref[0:SUBLANES, :] = xp_ref[C:C + SUBLANES, :]

    xp_ref[SUBLANES:SUBLANES + rows, 0:W] = q_ref[...]
    xp_ref[SUBLANES:SUBLANES + rows, W:2 * W] = k_ref[...]
    xp_ref[SUBLANES:SUBLANES + rows, 2 * W:3 * W] = v_ref[...]

    row = lax.broadcasted_iota(jnp.int32, (C, LANES), 0)
    valid = (c * C + row) < t_valid
    if rows < C:
        sm = jnp.concatenate([sm_ref[...], jnp.zeros((C - rows, LANES), F32)], axis=0)
    else:
        sm = sm_ref[...]
    beta_all = jnp.where(valid, jax.nn.sigmoid(sm), 0.0)
    g_all = jnp.where(valid, -jnp.exp(alog_ref[...]) * _softplus(sm + dt_ref[...]), 0.0)
    ri = lax.broadcasted_iota(jnp.int32, (C, C), 0)
    ci = lax.broadcasted_iota(jnp.int32, (C, C), 1)
    tri = ri >= ci
    strict = ri > ci
    gcum = _dot(tri.astype(F32), g_all, HI)
    li = lax.broadcasted_iota(jnp.int32, (LANES, LANES), 0)
    lj = lax.broadcasted_iota(jnp.int32, (LANES, LANES), 1)
    gcum_t = _dot_nt((li == lj).astype(F32), gcum, HI)
    valid_col = valid[:, 0:1]
    eye = (ri == ci).astype(F32)
    n_sq = int(math.log2(C)) - 1

    def mm(a, b):
        return _dot(a.astype(BF16), b.astype(BF16))

    def mm_nt(a, b):
        return _dot_nt(a.astype(BF16), b.astype(BF16))

    def conv(col):
        acc = None
        for j in range(CONV_W):
            term = (xp_ref[pl.ds(SUBLANES - (CONV_W - 1) + j, C), col:col + LANES]
                    * cw_ref[j:j + 1, col:col + LANES])
            acc = term if acc is None else acc + term
        return _silu(acc)

    def head_group(hs):
        n = len(hs)
        qs = [conv(h * HEAD_DIM) for h in hs]
        ks = [conv(W + h * HEAD_DIM) for h in hs]
        vs = [conv(2 * W + h * HEAD_DIM) for h in hs]
        qs = [q * lax.rsqrt(jnp.sum(q * q, axis=-1, keepdims=True) + EPS) * SCALE for q in qs]
        ks = [jnp.where(valid_col, k * lax.rsqrt(jnp.sum(k * k, axis=-1, keepdims=True) + EPS), 0.0) for k in ks]
        gcs = [gcum[:, SMALL_A + h:SMALL_A + h + 1] for h in hs]
        bhs = [beta_all[:, SMALL_B + h:SMALL_B + h + 1] for h in hs]
        grs = [gcum_t[SMALL_A + h:SMALL_A + h + 1, :] for h in hs]
        decays = [jnp.where(tri, jnp.exp(jnp.where(tri, gc - gr, 0.0)), 0.0) for gc, gr in zip(gcs, grs)]
        egs = [jnp.exp(gc) for gc in gcs]

        kbs = [k * b for k, b in zip(ks, bhs)]
        kqs = [mm_nt(jnp.concatenate([kb, q], axis=0), k) for kb, q, k in zip(kbs, qs, ks)]
        nmats = [jnp.where(strict, -(kq[0:C] * d), 0.0) for kq, d in zip(kqs, decays)]
        aqks = [jnp.where(tri, kq[C:2 * C] * d, 0.0) for kq, d in zip(kqs, decays)]
        pinvs = [eye + nm for nm in nmats]
        npows = nmats
        for _ in range(n_sq):
            npows = [mm(np_, np_) for np_ in npows]
            pinvs = [p + mm(p, np_) for p, np_ in zip(pinvs, npows)]
        uws = [mm(p, jnp.concatenate([v * b, kb * eg], axis=1))
               for p, v, b, kb, eg in zip(pinvs, vs, bhs, kbs, egs)]

        ss = [st_ref[h] for h in hs]
        wss = [mm(jnp.concatenate([uw[:, HEAD_DIM:2 * HEAD_DIM], q * eg], axis=0), s)
               for uw, q, eg, s in zip(uws, qs, egs, ss)]
        v_news = [uw[:, 0:HEAD_DIM] - ws[0:C] for uw, ws in zip(uws, wss)]
        os_ = [ws[C:2 * C] + mm(aqk, vn) for ws, aqk, vn in zip(wss, aqks, v_news)]
        for i in range(n):
            g_last = gcs[i][C - 1:C, :]
            k_dec = ks[i] * jnp.exp(g_last - gcs[i])
            st_ref[hs[i]] = ss[i] * jnp.exp(g_last) + _dot_tn(k_dec.astype(BF16), v_news[i].astype(BF16))
        for i in range(n):
            col = hs[i] * HEAD_DIM
            o = _rms(os_[i], ng_ref[...])
            o_ref[:, col:col + HEAD_DIM] = (o[0:rows] * _silu(z_ref[:, col:col + HEAD_DIM])).astype(o_ref.dtype)

    for h0 in range(0, GDN_HEADS, GDN_HEAD_GROUP):
        head_group(list(range(h0, h0 + GDN_HEAD_GROUP)))

    @pl.when(c == nc - 1)
    def _():
        sout_ref[0] = st_ref[...]


def _gdn(proj, small, conv_buf, s0, conv_w, a_log, dt_bias, norm_g, *, bsz, rows_per_seq, t_valid):
    C = GDN_CHUNK
    if rows_per_seq >= C:
        assert rows_per_seq % C == 0
        rows, nc = C, rows_per_seq // C
    else:
        assert rows_per_seq % SUBLANES == 0
        C = max(2 * SUBLANES, 1 << (rows_per_seq - 1).bit_length())
        rows, nc = rows_per_seq, 1
    wblk = GDN_WIDTH
    zeros = jnp.zeros((LANES - 2 * GDN_HEADS,), F32)
    alog_row = jnp.concatenate([jnp.zeros((GDN_HEADS,), F32), a_log, zeros]).reshape(1, LANES)
    dt_row = jnp.concatenate([jnp.zeros((GDN_HEADS,), F32), dt_bias, zeros]).reshape(1, LANES)
    row_map = lambda b, c: b * nc + c
    in_specs = [
        pl.BlockSpec((rows, wblk), lambda b, c: (row_map(b, c), COL_Q // wblk)),
        pl.BlockSpec((rows, wblk), lambda b, c: (row_map(b, c), COL_K // wblk)),
        pl.BlockSpec((rows, wblk), lambda b, c: (row_map(b, c), COL_V // wblk)),
        pl.BlockSpec((rows, wblk), lambda b, c: (row_map(b, c), COL_Z // wblk)),
        pl.BlockSpec((rows, LANES), lambda b, c: (row_map(b, c), 0)),
        pl.BlockSpec((1, CONV_W - 1, 3 * GDN_WIDTH), lambda b, c: (b, 0, 0)),
        pl.BlockSpec((1, GDN_HEADS, HEAD_DIM, HEAD_DIM), lambda b, c: (b, 0, 0, 0)),
        pl.BlockSpec((CONV_W, 3 * GDN_WIDTH), lambda b, c: (0, 0)),
        pl.BlockSpec((1, LANES), lambda b, c: (0, 0)),
        pl.BlockSpec((1, LANES), lambda b, c: (0, 0)),
        pl.BlockSpec((1, HEAD_DIM), lambda b, c: (0, 0)),
    ]
    out_specs = [
        pl.BlockSpec((rows, wblk), lambda b, c: (row_map(b, c), 0)),
        pl.BlockSpec((1, GDN_HEADS, HEAD_DIM, HEAD_DIM), lambda b, c: (b, 0, 0, 0)),
    ]
    return pl.pallas_call(
        functools.partial(_gdn_kernel, C=C, rows=rows, nc=nc, t_valid=t_valid),
        grid=(bsz, nc),
        in_specs=in_specs,
        out_specs=out_specs,
        out_shape=[jax.ShapeDtypeStruct((bsz * rows_per_seq, GDN_WIDTH), BF16),
                   jax.ShapeDtypeStruct((bsz, GDN_HEADS, HEAD_DIM, HEAD_DIM), F32)],
        scratch_shapes=[pltpu.VMEM((C + SUBLANES, 3 * GDN_WIDTH), F32),
                        pltpu.VMEM((GDN_HEADS, HEAD_DIM, HEAD_DIM), F32)],
        compiler_params=_cparams(("parallel", "arbitrary")),
        name="gdn_chunk_scan",
    )(proj, proj, proj, proj, small, conv_buf, s0, conv_w, alog_row, dt_row, norm_g.reshape(1, HEAD_DIM))


def _prep_kernel(nq_ref, cmp_ref, slc_ref, win_ref, sm_ref, qg_ref, kg_ref,
                 q_out, cmp_out, slc_out, win_out, gate_out, slc_bf_out, win_bf_out):
    tm = nq_ref.shape[0]
    qg = qg_ref[...]
    for h in range(NSA_HEADS):
        sl = slice(h * HEAD_DIM, (h + 1) * HEAD_DIM)
        q_out[:, sl] = _rms(nq_ref[:, sl], qg)
    for slab in range(KV_SLABS):
        sl = slice(slab * HEAD_DIM, (slab + 1) * HEAD_DIM)
        cmp_out[pl.ds(slab, tm, stride=KV_SLABS), :] = cmp_ref[:, sl]
    for src, dst, dst_bf, gi in ((slc_ref, slc_out, slc_bf_out, 1), (win_ref, win_out, win_bf_out, 2)):
        kg = kg_ref[gi:gi + 1, :]
        for slab in range(KV_SLABS):
            sl = slice(slab * HEAD_DIM, (slab + 1) * HEAD_DIM)
            x = _rms(src[:, sl], kg) if slab < NSA_KV_HEADS else src[:, sl]
            dst[pl.ds(slab, tm, stride=KV_SLABS), :] = x
            dst_bf[:, sl] = x.astype(BF16)
    sig = jax.nn.sigmoid(sm_ref[...])
    per = NSA_GROUP * 3
    for g in range(NSA_KV_HEADS):
        gate_out[g] = pltpu.roll(sig, LANES - (SMALL_GATE + per * g), axis=1)


def _nsa_prep(proj, small, q_norm_g, k_norm_g, *, tm):
    m = proj.shape[0]
    assert m % tm == 0
    return pl.pallas_call(
        _prep_kernel,
        grid=(m // tm,),
        in_specs=[pl.BlockSpec((tm, NSA_WIDTH), lambda i: (i, COL_NQ // NSA_WIDTH)),
                  pl.BlockSpec((tm, KV_WIDTH), lambda i: (i, COL_CMP // KV_WIDTH)),
                  pl.BlockSpec((tm, KV_WIDTH), lambda i: (i, COL_SLC // KV_WIDTH)),
                  pl.BlockSpec((tm, KV_WIDTH), lambda i: (i, COL_WIN // KV_WIDTH)),
                  pl.BlockSpec((tm, LANES), lambda i: (i, 0)),
                  pl.BlockSpec((1, HEAD_DIM), lambda i: (0, 0)),
                  pl.BlockSpec((3, HEAD_DIM), lambda i: (0, 0))],
        out_specs=[pl.BlockSpec((tm, NSA_WIDTH), lambda i: (i, 0)),
                   pl.BlockSpec((tm * KV_SLABS, HEAD_DIM), lambda i: (i, 0)),
                   pl.BlockSpec((tm * KV_SLABS, HEAD_DIM), lambda i: (i, 0)),
                   pl.BlockSpec((tm * KV_SLABS, HEAD_DIM), lambda i: (i, 0)),
                   pl.BlockSpec((NSA_KV_HEADS, tm, LANES), lambda i: (0, i, 0)),
                   pl.BlockSpec((tm, KV_WIDTH), lambda i: (i, 0)),
                   pl.BlockSpec((tm, KV_WIDTH), lambda i: (i, 0))],
        out_shape=[jax.ShapeDtypeStruct((m, NSA_WIDTH), F32),
                   jax.ShapeDtypeStruct((m * KV_SLABS, HEAD_DIM), F32),
                   jax.ShapeDtypeStruct((m * KV_SLABS, HEAD_DIM), F32),
                   jax.ShapeDtypeStruct((m * KV_SLABS, HEAD_DIM), F32),
                   jax.ShapeDtypeStruct((NSA_KV_HEADS, m, LANES), F32),
                   jax.ShapeDtypeStruct((m, KV_WIDTH), BF16),
                   jax.ShapeDtypeStruct((m, KV_WIDTH), BF16)],
        compiler_params=_cparams(("parallel",)),
        name="nsa_prep",
    )(proj, proj, proj, proj, small, q_norm_g.reshape(1, HEAD_DIM), k_norm_g)


def _pool_kernel(tbl_ref, *refs, pg):
    del tbl_ref
    page_refs = refs[:pg]
    w_ref, s0_ref, s1_ref = refs[pg:]
    sub = PAGE_SIZE // CMP_STRIDE
    out_rows = sub * KV_SLABS
    for u in range(pg):
        x = page_refs[u][...].reshape(sub, CMP_STRIDE, KV_SLABS, HEAD_DIM)
        s0_ref[0, u * out_rows:(u + 1) * out_rows, :] = jnp.sum(x * w_ref[0][None], axis=1).reshape(out_rows, HEAD_DIM)
        s1_ref[0, u * out_rows:(u + 1) * out_rows, :] = jnp.sum(x * w_ref[1][None], axis=1).reshape(out_rows, HEAD_DIM)


def _pool(rows2d, table, wexp):
    bsz, n_pages = table.shape
    pg = math.gcd(n_pages, 16)
    sub = PAGE_SIZE // CMP_STRIDE
    page_rows = PAGE_SIZE * KV_SLABS
    out_rows = pg * sub * KV_SLABS

    def page_spec(u):
        return pl.BlockSpec((page_rows, HEAD_DIM), lambda b, p, tbl: (tbl[b, p * pg + u], 0))

    grid_spec = pltpu.PrefetchScalarGridSpec(
        num_scalar_prefetch=1,
        grid=(bsz, n_pages // pg),
        in_specs=[page_spec(u) for u in range(pg)]
        + [pl.BlockSpec((2, CMP_STRIDE, KV_SLABS, HEAD_DIM), lambda b, p, tbl: (0, 0, 0, 0))],
        out_specs=[pl.BlockSpec((1, out_rows, HEAD_DIM), lambda b, p, tbl: (b, p, 0)),
                   pl.BlockSpec((1, out_rows, HEAD_DIM), lambda b, p, tbl: (b, p, 0))],
    )
    shape = jax.ShapeDtypeStruct((bsz, n_pages * sub * KV_SLABS, HEAD_DIM), F32)
    return pl.pallas_call(
        functools.partial(_pool_kernel, pg=pg), grid_spec=grid_spec, out_shape=[shape, shape],
        compiler_params=_cparams(("parallel", "arbitrary")), name="cmp_pool",
    )(table, *([rows2d] * pg), wexp)


def _cmp_fin_kernel(s0_ref, s1_ref, phi_ref, kg_ref, kc_ref, vc_ref, *, n_cmp):
    nb = s0_ref.shape[1] // KV_SLABS
    live = lax.broadcasted_iota(jnp.int32, (nb, HEAD_DIM), 0) < n_cmp
    zero_row = jnp.zeros((1, HEAD_DIM), F32)

    def pooled(slab):
        first = s0_ref[0, pl.ds(slab, nb, stride=KV_SLABS), :]
        second = s1_ref[0, pl.ds(KV_SLABS + slab, nb - 1, stride=KV_SLABS), :]
        return first + jnp.concatenate([second, zero_row], axis=0)

    groups = range(NSA_KV_HEADS)
    ks = [_dot(pooled(g), phi_ref[0, g], HI) for g in groups]
    vs = [_dot(pooled(NSA_KV_HEADS + g), phi_ref[1, g], HI) for g in groups]
    for g in groups:
        kc_ref[0, g] = jnp.where(live, _rms(ks[g], kg_ref[...]), 0.0)
        vc_ref[0, g] = jnp.where(live, vs[g], 0.0)


def _cmp_finish(s0, s1, phi, kg, *, n_cmp):
    bsz = s0.shape[0]
    nb = s0.shape[1] // KV_SLABS
    assert n_cmp <= nb - 1
    shape = jax.ShapeDtypeStruct((bsz, NSA_KV_HEADS, nb, HEAD_DIM), F32)
    return pl.pallas_call(
        functools.partial(_cmp_fin_kernel, n_cmp=n_cmp),
        grid=(bsz,),
        in_specs=[pl.BlockSpec((1, nb * KV_SLABS, HEAD_DIM), lambda b: (b, 0, 0)),
                  pl.BlockSpec((1, nb * KV_SLABS, HEAD_DIM), lambda b: (b, 0, 0)),
                  pl.BlockSpec((2, NSA_KV_HEADS, HEAD_DIM, HEAD_DIM), lambda b: (0, 0, 0, 0)),
                  pl.BlockSpec((1, HEAD_DIM), lambda b: (0, 0))],
        out_specs=[pl.BlockSpec((1, NSA_KV_HEADS, nb, HEAD_DIM), lambda b: (b, 0, 0, 0)),
                   pl.BlockSpec((1, NSA_KV_HEADS, nb, HEAD_DIM), lambda b: (b, 0, 0, 0))],
        out_shape=[shape, shape],
        compiler_params=_cparams(("parallel",)),
        name="cmp_finish",
    )(s0, s1, phi, kg.reshape(1, HEAD_DIM))


def _stack_heads(ref, g, tq, dtype):
    parts = [ref[:, (g * NSA_GROUP + r) * HEAD_DIM:(g * NSA_GROUP + r + 1) * HEAD_DIM].astype(dtype)
             for r in range(NSA_GROUP)]
    return jnp.concatenate(parts, axis=0)


def _cmp_topk_kernel(q_ref, kc_ref, vc_ref, ov_ref, ocmp_ref, sel_ref, *, tq, n_cmp, n_slc, offset, token_lanes):
    i = pl.program_id(1)
    nb = kc_ref.shape[2]
    nsp = sel_ref.shape[2]
    rows = NSA_GROUP * tq
    r4 = lax.broadcasted_iota(jnp.int32, (rows, nb), 0)
    n4 = lax.broadcasted_iota(jnp.int32, (rows, nb), 1)
    tpos = offset + i * tq + (r4 & (tq - 1))
    valid = (n4 * CMP_STRIDE + (CMP_BLOCK - 1) <= tpos) & (n4 < n_cmp)

    if token_lanes:
        nsr = -(-n_slc // SUBLANES) * SUBLANES
        shape, blk_axis, tok_axis = (nsr, tq), 0, 1
    else:
        shape, blk_axis, tok_axis = (tq, nsp), 1, 0
    blk = lax.broadcasted_iota(jnp.int32, shape, blk_axis)
    tq_pos = offset + i * tq + lax.broadcasted_iota(jnp.int32, shape, tok_axis)
    cur = tq_pos // SLC_BLOCK
    forced = (blk == 0) | ((blk <= cur) & (blk > cur - SLC_LOCAL))
    future = blk > cur
    in_range = blk < n_slc
    topk = min(SLC_TOPK, n_slc)

    groups = range(NSA_KV_HEADS)
    ss = [jnp.where(valid, _dot_nt(_stack_heads(q_ref, g, tq, F32), kc_ref[0, g], HI) * SCALE, -jnp.inf)
          for g in groups]
    ms = [jnp.max(s, axis=-1, keepdims=True) for s in ss]
    ps = [jnp.exp(s - jnp.where(m == -jnp.inf, 0.0, m)) for s, m in zip(ss, ms)]
    ps = [p / jnp.maximum(jnp.sum(p, axis=-1, keepdims=True), jnp.finfo(jnp.float32).tiny) for p in ps]
    os_ = [_dot(p.astype(BF16), vc_ref[0, g].astype(BF16)) for g, p in zip(groups, ps)]
    psums = [functools.reduce(jnp.add, [p[r * tq:(r + 1) * tq] for r in range(NSA_GROUP)]) for p in ps]
    if token_lanes:
        imps = [_dot_nt(ov_ref[0:shape[0], :], psum, HI) for psum in psums]
    else:
        imps = [_dot(psum, ov_ref[...], HI) for psum in psums]
    scores = [jnp.where(in_range, jnp.where(forced, FORCE_SCORE, jnp.where(future, -FORCE_SCORE, imp)), -jnp.inf)
              for imp in imps]
    ranks = [jnp.zeros(shape, jnp.int32) for _ in groups]
    for j in range(n_slc):
        for g in groups:
            cj = scores[g][j:j + 1, :] if token_lanes else scores[g][:, j:j + 1]
            ahead = (cj > scores[g]) | ((cj == scores[g]) & (blk > j))
            ranks[g] = ranks[g] + ahead.astype(jnp.int32)
    for g in groups:
        for r in range(NSA_GROUP):
            col = (g * NSA_GROUP + r) * HEAD_DIM
            ocmp_ref[:, col:col + HEAD_DIM] = os_[g][r * tq:(r + 1) * tq]
        sel = ((ranks[g] < topk) & in_range).astype(F32)
        if token_lanes:
            sel = jnp.concatenate([sel, jnp.zeros((nsp - shape[0], tq), F32)], axis=0).T
        sel_ref[g] = sel


def _cmp_topk(qn, kc, vc, overlap, *, bsz, rows_per_seq, tq, n_cmp, n_slc, offset):
    m = qn.shape[0]
    nt = rows_per_seq // tq
    nb = kc.shape[2]
    nsp = overlap.shape[1]
    token_lanes = tq % LANES == 0 and nsp == tq
    if token_lanes:
        overlap = overlap.T
    return pl.pallas_call(
        functools.partial(_cmp_topk_kernel, tq=tq, n_cmp=n_cmp, n_slc=n_slc, offset=offset,
                          token_lanes=token_lanes),
        grid=(bsz, nt),
        in_specs=[pl.BlockSpec((tq, NSA_WIDTH), lambda b, i: (b * nt + i, 0)),
                  pl.BlockSpec((1, NSA_KV_HEADS, nb, HEAD_DIM), lambda b, i: (b, 0, 0, 0)),
                  pl.BlockSpec((1, NSA_KV_HEADS, nb, HEAD_DIM), lambda b, i: (b, 0, 0, 0)),
                  pl.BlockSpec(overlap.shape, lambda b, i: (0, 0))],
        out_specs=[pl.BlockSpec((tq, NSA_WIDTH), lambda b, i: (b * nt + i, 0)),
                   pl.BlockSpec((NSA_KV_HEADS, tq, nsp), lambda b, i: (0, b * nt + i, 0))],
        out_shape=[jax.ShapeDtypeStruct((m, NSA_WIDTH), F32),
                   jax.ShapeDtypeStruct((NSA_KV_HEADS, m, nsp), F32)],
        compiler_params=_cparams(("parallel", "arbitrary")),
        name="cmp_attn_topk",
    )(qn, kc, vc, overlap)


def _overlap_matrix(nb, nsp, n_cmp, n_slc):
    cs = np.arange(nb) * CMP_STRIDE
    ss = np.arange(nsp) * SLC_BLOCK
    lo = np.maximum(cs[:, None], ss[None, :])
    hi = np.minimum(cs[:, None] + CMP_BLOCK, ss[None, :] + SLC_BLOCK)
    ov = (np.maximum(hi - lo, 0) / CMP_BLOCK).astype(np.float32)
    ov[n_cmp:, :] = 0.0
    ov[:, n_slc:] = 0.0
    return jnp.asarray(ov)


def _flash_init(m_ref, l_ref, acc_ref):
    m_ref[...] = jnp.full(m_ref.shape, NEG, F32)
    l_ref[...] = jnp.zeros(l_ref.shape, F32)
    acc_ref[...] = jnp.zeros(acc_ref.shape, F32)


def _lane_rep(col, width):
    tile = jnp.broadcast_to(col, (col.shape[0], LANES))
    return tile if width == LANES else jnp.concatenate([tile] * (width // LANES), axis=1)


def _flash_update(s, mask, v, m_ref, l_ref, acc_ref):
    rows, tk = s.shape
    reps = tk // LANES
    s = jnp.where(mask, s, NEG)
    m_prev = m_ref[...]
    m_new = jnp.maximum(m_prev, jnp.broadcast_to(jnp.max(s, axis=-1, keepdims=True), (rows, LANES)))
    alpha = jnp.exp(m_prev - m_new)
    p = jnp.exp(s - jnp.concatenate([m_new] * reps, axis=1))
    l_ref[...] = alpha * l_ref[...] + jnp.broadcast_to(jnp.sum(p, axis=-1, keepdims=True), (rows, LANES))
    acc_ref[...] = alpha * acc_ref[...] + _dot(p.astype(v.dtype), v)
    m_ref[...] = m_new


def _pattn_kernel(q_ref, ks_ref, vs_ref, kw_ref, vw_ref, ocmp_ref, sel_ref, gate_ref, exp_ref, o_ref,
                  sexp_ref, m_ref, l_ref, acc_ref, *, tq, tk, wlen):
    i = pl.program_id(2)
    t = ks_ref.shape[0]
    q4 = jnp.concatenate([(q_ref[:, r * HEAD_DIM:(r + 1) * HEAD_DIM] * SCALE).astype(BF16)
                          for r in range(NSA_GROUP)], axis=0)
    sexp_ref[...] = _dot(sel_ref[0].astype(BF16), exp_ref[...])
    qpos = i * tq + lax.broadcasted_iota(jnp.int32, (tq, tk), 0)
    lane = lax.broadcasted_iota(jnp.int32, (tq, tk), 1)

    def tile4(x):
        return jnp.concatenate([x] * NSA_GROUP, axis=0)

    def slc_step(j, carry):
        start = pl.multiple_of(j * tk, tk)
        s = _dot_nt(q4, ks_ref[pl.ds(start, tk), :])
        mask = (sexp_ref[:, pl.ds(start, tk)] > 0.5) & (start + lane <= qpos)
        _flash_update(s, tile4(mask), vs_ref[pl.ds(start, tk), :], m_ref, l_ref, acc_ref)
        return carry

    _flash_init(m_ref, l_ref, acc_ref)
    lax.fori_loop(0, ((i + 1) * tq + tk - 1) // tk, slc_step, 0)
    o_slc = acc_ref[...] / l_ref[...]

    wstart = pl.multiple_of(jnp.clip((i + 1) * tq - wlen, 0, t - wlen), tq)
    s = _dot_nt(q4, kw_ref[pl.ds(wstart, wlen), :])
    wdiff = (i * tq + lax.broadcasted_iota(jnp.int32, (tq, wlen), 0)
             - (wstart + lax.broadcasted_iota(jnp.int32, (tq, wlen), 1)))
    s = jnp.where(tile4((wdiff >= 0) & (wdiff < WINDOW)), s, NEG)
    p = jnp.exp(s - _lane_rep(jnp.max(s, axis=-1, keepdims=True), wlen))
    o_win = _dot(p.astype(BF16), vw_ref[pl.ds(wstart, wlen), :]) / _lane_rep(jnp.sum(p, axis=-1, keepdims=True), HEAD_DIM)

    gates = gate_ref[0]
    for r in range(NSA_GROUP):
        sl = slice(r * tq, (r + 1) * tq)
        col = slice(r * HEAD_DIM, (r + 1) * HEAD_DIM)
        o = (gates[:, 3 * r:3 * r + 1] * ocmp_ref[:, col]
             + gates[:, 3 * r + 1:3 * r + 2] * o_slc[sl]
             + gates[:, 3 * r + 2:3 * r + 3] * o_win[sl])
        o_ref[:, col] = o.astype(o_ref.dtype)


def _prompt_attn(qn, slc_bf, win_bf, ocmp, sel, gates, *, bsz, t, tq):
    m = qn.shape[0]
    nt = t // tq
    gw = NSA_GROUP * HEAD_DIM
    half = NSA_KV_HEADS
    nsp = sel.shape[2]
    n_slc = -(-t // SLC_BLOCK)
    expand = np.zeros((nsp, t), np.float32)
    expand[np.arange(t) // SLC_BLOCK, np.arange(t)] = 1.0
    assert n_slc <= nsp
    tk = next(c for c in (512, 256, 128) if t % c == 0 and c >= tq)
    wlen = min(t, WINDOW + tq)
    assert wlen % tq == 0 and t % tq == 0
    kv_spec = lambda which: pl.BlockSpec((t, HEAD_DIM), lambda b, g, i: (b, which * half + g))
    return pl.pallas_call(
        functools.partial(_pattn_kernel, tq=tq, tk=tk, wlen=wlen),
        grid=(bsz, NSA_KV_HEADS, nt),
        in_specs=[pl.BlockSpec((tq, gw), lambda b, g, i: (b * nt + i, g)),
                  kv_spec(0), kv_spec(1), kv_spec(0), kv_spec(1),
                  pl.BlockSpec((tq, gw), lambda b, g, i: (b * nt + i, g)),
                  pl.BlockSpec((1, tq, nsp), lambda b, g, i: (g, b * nt + i, 0)),
                  pl.BlockSpec((1, tq, LANES), lambda b, g, i: (g, b * nt + i, 0)),
                  pl.BlockSpec((nsp, t), lambda b, g, i: (0, 0))],
        out_specs=pl.BlockSpec((tq, gw), lambda b, g, i: (b * nt + i, g)),
        out_shape=jax.ShapeDtypeStruct((m, NSA_WIDTH), BF16),
        scratch_shapes=[pltpu.VMEM((tq, t), F32),
                        pltpu.VMEM((NSA_GROUP * tq, LANES), F32),
                        pltpu.VMEM((NSA_GROUP * tq, LANES), F32),
                        pltpu.VMEM((NSA_GROUP * tq, HEAD_DIM), F32)],
        compiler_params=_cparams(("parallel", "parallel", "arbitrary")),
        name="nsa_prompt_attn",
    )(qn, slc_bf, slc_bf, win_bf, win_bf, ocmp, sel, gates, jnp.asarray(expand, BF16))


def _sattn_kernel(tbl_ref, q_ref, *refs, tq, n_steps, pg, past):
    del tbl_ref
    page_refs = refs[:pg]
    (snew_ref, wcache_ref, wnew_ref, ocmp_ref, sel_ref, gate_ref, exp_ref, o_ref, s_buf, v_buf) = refs[pg:]
    j = pl.program_id(1)
    rows = NSA_GROUP * tq
    nsp = sel_ref.shape[2]
    n_prev = wcache_ref.shape[0] // KV_SLABS

    def tile4(x):
        return jnp.concatenate([x] * NSA_GROUP, axis=0)

    q4_bf = [_stack_heads(q_ref, g, tq, BF16) for g in range(NSA_KV_HEADS)]
    for u in range(pg):
        key0 = pl.multiple_of((j * pg + u) * PAGE_SIZE, PAGE_SIZE)
        for g in range(NSA_KV_HEADS):
            k = page_refs[u][pl.ds(g, PAGE_SIZE, stride=KV_SLABS), :].astype(BF16)
            s_buf[g * rows:(g + 1) * rows, pl.ds(key0, PAGE_SIZE)] = _dot_nt(q4_bf[g], k)
            v_buf[g, pl.ds(key0, PAGE_SIZE), :] = (
                page_refs[u][pl.ds(NSA_KV_HEADS + g, PAGE_SIZE, stride=KV_SLABS), :].astype(BF16))

    @pl.when(j == n_steps - 1)
    def _():
        blk_lane = lax.broadcasted_iota(jnp.int32, (tq, nsp), 1)
        trow = lax.broadcasted_iota(jnp.int32, (tq, tq), 0)
        tcol = lax.broadcasted_iota(jnp.int32, (tq, tq), 1)
        causal = tcol <= trow
        wrow = lax.broadcasted_iota(jnp.int32, (tq, n_prev), 0)
        wcol = lax.broadcasted_iota(jnp.int32, (tq, n_prev), 1)
        wdiff = wrow + n_prev - wcol
        wmask = (wdiff >= 0) & (wdiff < WINDOW)
        ndiff = trow - tcol
        nmask = (ndiff >= 0) & (ndiff < WINDOW)
        gates_all = gate_ref[...]
        groups = range(NSA_KV_HEADS)
        q4s = [_stack_heads(q_ref, g, tq, F32) for g in groups]

        def joint_softmax(scs, sns, vcs, vns):
            mxs = [jnp.maximum(jnp.max(sc, axis=-1, keepdims=True), jnp.max(sn, axis=-1, keepdims=True))
                   for sc, sn in zip(scs, sns)]
            pcs = [jnp.exp(sc - mx) for sc, mx in zip(scs, mxs)]
            pns = [jnp.exp(sn - mx) for sn, mx in zip(sns, mxs)]
            dens = [jnp.sum(pc, axis=-1, keepdims=True) + jnp.sum(pn, axis=-1, keepdims=True)
                    for pc, pn in zip(pcs, pns)]
            return [(_dot(pc.astype(BF16), vc) + _dot(pn, vn, HI)) / den
                    for pc, pn, vc, vn, den in zip(pcs, pns, vcs, vns, dens)]

        flags = []
        for g in groups:
            flag = jnp.zeros((tq, tq), jnp.bool_)
            for u in range(tq):
                col = jnp.sum(jnp.where(blk_lane == (past + u) // SLC_BLOCK, sel_ref[g], 0.0),
                              axis=1, keepdims=True) > 0.5
                flag = flag | (col & (tcol == u))
            flags.append(flag & causal)
        scs = [jnp.where(tile4(_dot(sel_ref[g].astype(BF16), exp_ref[...]) > 0.5),
                         s_buf[g * rows:(g + 1) * rows, :] * SCALE, NEG) for g in groups]
        sns = [jnp.where(tile4(flags[g]),
                         _dot_nt(q4s[g], snew_ref[pl.ds(g, tq, stride=KV_SLABS), :], HI) * SCALE, NEG)
               for g in groups]
        o_slcs = joint_softmax(scs, sns, [v_buf[g] for g in groups],
                               [snew_ref[pl.ds(NSA_KV_HEADS + g, tq, stride=KV_SLABS), :] for g in groups])

        scs = [jnp.where(tile4(wmask), _dot_nt(q4s[g].astype(BF16),
                                               wcache_ref[pl.ds(g, n_prev, stride=KV_SLABS), :].astype(BF16)) * SCALE,
                         NEG) for g in groups]
        sns = [jnp.where(tile4(nmask),
                         _dot_nt(q4s[g], wnew_ref[pl.ds(g, tq, stride=KV_SLABS), :], HI) * SCALE, NEG)
               for g in groups]
        o_wins = joint_softmax(
            scs, sns,
            [wcache_ref[pl.ds(NSA_KV_HEADS + g, n_prev, stride=KV_SLABS), :].astype(BF16) for g in groups],
            [wnew_ref[pl.ds(NSA_KV_HEADS + g, tq, stride=KV_SLABS), :] for g in groups])

        for g in groups:
            gates = gates_all[g]
            for r in range(NSA_GROUP):
                sl = slice(r * tq, (r + 1) * tq)
                col = slice((g * NSA_GROUP + r) * HEAD_DIM, (g * NSA_GROUP + r + 1) * HEAD_DIM)
                o = (gates[:, 3 * r:3 * r + 1] * ocmp_ref[:, col]
                     + gates[:, 3 * r + 1:3 * r + 2] * o_slcs[g][sl]
                     + gates[:, 3 * r + 2:3 * r + 3] * o_wins[g][sl])
                o_ref[:, col] = o.astype(o_ref.dtype)


def _sample_attn(qn, cache2d, page_table, slc_new, win_cache2d, win_row0, win_new, ocmp, sel, gates,
                 *, bsz, tq, past, n_prev):
    n_pages = page_table.shape[1]
    pg = math.gcd(n_pages, 16)
    n_steps = n_pages // pg
    nsp = sel.shape[2]
    rows = NSA_GROUP * tq
    page_rows = PAGE_SIZE * KV_SLABS
    expand = np.zeros((nsp, past), np.float32)
    expand[np.arange(past) // SLC_BLOCK, np.arange(past)] = 1.0

    def page_spec(u):
        return pl.BlockSpec((page_rows, HEAD_DIM), lambda b, j, tbl: (tbl[b, j * pg + u], 0))

    grid_spec = pltpu.PrefetchScalarGridSpec(
        num_scalar_prefetch=1,
        grid=(bsz, n_steps),
        in_specs=[pl.BlockSpec((tq, NSA_WIDTH), lambda b, j, tbl: (b, 0))]
        + [page_spec(u) for u in range(pg)]
        + [pl.BlockSpec((tq * KV_SLABS, HEAD_DIM), lambda b, j, tbl: (b, 0)),
           pl.BlockSpec((n_prev * KV_SLABS, HEAD_DIM), lambda b, j, tbl: (win_row0 + b, 0)),
           pl.BlockSpec((tq * KV_SLABS, HEAD_DIM), lambda b, j, tbl: (b, 0)),
           pl.BlockSpec((tq, NSA_WIDTH), lambda b, j, tbl: (b, 0)),
           pl.BlockSpec((NSA_KV_HEADS, tq, nsp), lambda b, j, tbl: (0, b, 0)),
           pl.BlockSpec((NSA_KV_HEADS, tq, LANES), lambda b, j, tbl: (0, b, 0)),
           pl.BlockSpec((nsp, past), lambda b, j, tbl: (0, 0))],
        out_specs=pl.BlockSpec((tq, NSA_WIDTH), lambda b, j, tbl: (b, 0)),
        scratch_shapes=[pltpu.VMEM((NSA_KV_HEADS * rows, past), F32),
                        pltpu.VMEM((NSA_KV_HEADS, past, HEAD_DIM), BF16)],
    )
    return pl.pallas_call(
        functools.partial(_sattn_kernel, tq=tq, n_steps=n_steps, pg=pg, past=past),
        grid_spec=grid_spec,
        out_shape=jax.ShapeDtypeStruct((bsz * tq, NSA_WIDTH), BF16),
        compiler_params=_cparams(("parallel", "arbitrary")),
        name="nsa_sample_attn",
    )(page_table, qn, *([cache2d] * pg), slc_new, win_cache2d, win_new, ocmp, sel, gates,
      jnp.asarray(expand, BF16))


def _split_w_in(w_in):
    wt = w_in.T
    pad = jnp.zeros((LANES - 2 * GDN_HEADS - 3 * NSA_HEADS, wt.shape[1]), w_in.dtype)
    wt_small = jnp.concatenate([wt[W_IN_B0:W_IN_N0], wt[W_IN_G0:], pad], axis=0)
    return wt, wt[W_IN_N0:W_IN_G0], wt_small


def _in_proj(x2d, p, *, tm):
    xn = _rms_cast(x2d, p["attn_norm_g"], tm=min(tm, 512))
    w_full, w_b, w_small = p["w_in"]
    proj_a = _mm_nw([xn], w_full, n=PROJ_A_WIDTH, tm=tm, tn=512, w_t=True)
    proj_b = _mm_nw([xn], w_b, n=PROJ_B_WIDTH, tm=tm, tn=512, w_t=True)
    small = _mm_nw([xn], w_small, n=LANES, tm=tm, tn=LANES, w_t=True)
    return proj_a, proj_b, small


def _pos_weights(cmp_pos_w):
    w = cmp_pos_w.reshape(2, CMP_BLOCK // CMP_STRIDE, CMP_STRIDE, NSA_KV_HEADS)
    w = jnp.transpose(w, (1, 2, 0, 3))
    return jnp.broadcast_to(w[..., None], w.shape + (HEAD_DIM,)).reshape(2, CMP_STRIDE, KV_SLABS, HEAD_DIM)


def _mixers_out(x2d, o_gdn, o_nsa, p, *, tm):
    w_o, w_up = p["w_o"], p["w_up"]
    x1 = _mm_nw([o_gdn, o_nsa], w_o, n=w_o.shape[1], tm=tm, tn=512, res=x2d)
    xn = _rms_cast(x1, p["mlp_norm_g"], tm=min(tm, 512))
    if "w_down_bf" not in p:
        hid, p["w_down_bf"] = _mm_nw([xn], w_up, n=w_up.shape[1], tm=tm, tn=512, relu2=True, out_dtype=BF16,
                                     cast_side=p["w_down"])
    else:
        hid = _mm_nw([xn], w_up, n=w_up.shape[1], tm=tm, tn=512, relu2=True, out_dtype=BF16)
    w_down = p["w_down_bf"]
    return _mm_res(hid, w_down, x1, tm=tm, tn=min(1024, w_down.shape[1]), tk=min(2048, w_down.shape[0]))


def _round_up(x, n):
    return -(-x // n) * n


def _prompt_layer(x, p):
    bsz, t, d = x.shape
    m = bsz * t
    x2d = x.reshape(m, d)
    proj_a, proj_b, small = _in_proj(x2d, p, tm=1024)
    conv0 = jnp.zeros((bsz, CONV_W - 1, 3 * GDN_WIDTH), F32)
    s_zero = jnp.zeros((bsz, GDN_HEADS, HEAD_DIM, HEAD_DIM), F32)
    o_gdn, s_new = _gdn(proj_a, small, conv0, s_zero, p["gdn_conv_w"], p["gdn_a_log"], p["gdn_dt_bias"],
                        p["gdn_norm_g"], bsz=bsz, rows_per_seq=t, t_valid=t)
    qn, cmp_n, slc_n, win_n, gates, slc_bf, win_bf = _nsa_prep(proj_b, small, p["q_norm_g"], p["k_norm_g"], tm=512)

    n_cmp = (t - CMP_BLOCK) // CMP_STRIDE + 1
    n_slc = -(-t // SLC_BLOCK)
    n_pages = t // PAGE_SIZE
    table = jnp.arange(bsz * n_pages, dtype=jnp.int32).reshape(bsz, n_pages)
    kv_tail = (2, NSA_KV_HEADS, HEAD_DIM)
    s0, s1 = _pool(cmp_n, table, p["cmp_w"])
    kc, vc = _cmp_finish(s0, s1, p["cmp_phi"], p["k_norm_g"][0], n_cmp=n_cmp)
    nb = kc.shape[2]
    nsp = _round_up(n_slc, LANES)
    tq = 128
    ocmp, sel = _cmp_topk(qn, kc, vc, _overlap_matrix(nb, nsp, n_cmp, n_slc), bsz=bsz, rows_per_seq=t, tq=tq,
                          n_cmp=n_cmp, n_slc=n_slc, offset=0)
    tq_attn = 2 * tq if t % (2 * tq) == 0 else tq
    o_nsa = _prompt_attn(qn, slc_bf, win_bf, ocmp, sel, gates, bsz=bsz, t=t, tq=tq_attn)
    y = _mixers_out(x2d, o_gdn, o_nsa, p, tm=1024)

    keep = min(WINDOW, t)
    return (y.reshape(bsz, t, d),
            cmp_n.reshape((bsz, t) + kv_tail),
            slc_n.reshape((bsz, t) + kv_tail),
            win_n.reshape((bsz, t) + kv_tail)[:, t - keep:],
            proj_a.reshape(bsz, t, PROJ_A_WIDTH)[:, t - (CONV_W - 1):, 0:3 * GDN_WIDTH],
            s_new)


def _sample_layer(x, layer, cache_cmp_kv, cache_slc_kv, cache_win_kv, conv_buf, s0_state, page_table, p):
    bsz, t, d = x.shape
    tp = _round_up(t, SUBLANES)
    n_pages = page_table.shape[1]
    past = n_pages * PAGE_SIZE
    assert t <= SLC_BLOCK and past % SLC_BLOCK == 0 and t >= CONV_W - 1
    x2d = jnp.pad(x, ((0, 0), (0, tp - t), (0, 0))).reshape(bsz * tp, d)
    proj_a, proj_b, small = _in_proj(x2d, p, tm=bsz * tp)
    o_gdn, s_new = _gdn(proj_a, small, conv_buf, s0_state, p["gdn_conv_w"], p["gdn_a_log"], p["gdn_dt_bias"],
                        p["gdn_norm_g"], bsz=bsz, rows_per_seq=tp, t_valid=t)
    qn, cmp_n, slc_n, win_n, gates, _, _ = _nsa_prep(proj_b, small, p["q_norm_g"], p["k_norm_g"], tm=bsz * tp)

    total = past + t
    n_cmp = (total - CMP_BLOCK) // CMP_STRIDE + 1
    n_slc = -(-total // SLC_BLOCK)
    assert (n_cmp - 1) * CMP_STRIDE + CMP_BLOCK <= past
    depth, n_phys = cache_cmp_kv.shape[:2]
    table = page_table + layer * n_phys
    cache_rows = depth * n_phys * PAGE_SIZE * KV_SLABS
    s0, s1 = _pool(cache_cmp_kv.reshape(cache_rows, HEAD_DIM), table, p["cmp_w"])
    kc, vc = _cmp_finish(s0, s1, p["cmp_phi"], p["k_norm_g"][0], n_cmp=n_cmp)
    nb = kc.shape[2]
    nsp = _round_up(n_slc, LANES)
    ocmp, sel = _cmp_topk(qn, kc, vc, _overlap_matrix(nb, nsp, n_cmp, n_slc), bsz=bsz, rows_per_seq=tp, tq=tp,
                          n_cmp=n_cmp, n_slc=n_slc, offset=past)
    n_prev = cache_win_kv.shape[2]
    o_nsa = _sample_attn(qn, cache_slc_kv.reshape(cache_rows, HEAD_DIM), table, slc_n,
                         cache_win_kv.reshape(depth * bsz * n_prev * KV_SLABS, HEAD_DIM), layer * bsz,
                         win_n, ocmp, sel, gates, bsz=bsz, tq=tp, past=past, n_prev=n_prev)
    y = _mixers_out(x2d, o_gdn, o_nsa, p, tm=bsz * tp)

    kv_tail = (2, NSA_KV_HEADS, HEAD_DIM)
    win_all = jnp.concatenate([cache_win_kv[layer], win_n.reshape((bsz, tp) + kv_tail)[:, :t]], axis=1)
    keep = min(WINDOW, n_prev + t)
    return (y.reshape(bsz, tp, d)[:, :t],
            cmp_n.reshape((bsz, tp) + kv_tail)[:, :t],
            slc_n.reshape((bsz, tp) + kv_tail)[:, :t],
            win_all[:, n_prev + t - keep:],
            proj_a.reshape(bsz, tp, PROJ_A_WIDTH)[:, t - (CONV_W - 1):t, 0:3 * GDN_WIDTH],
            s_new)


def kernel(x_prompt, x_sample, cache_cmp_kv, cache_slc_kv, cache_win_kv, cache_gdn_conv, state_gdn, page_table, attn_norm_g, w_in, gdn_conv_w, gdn_a_log, gdn_dt_bias, gdn_norm_g, q_norm_g, k_norm_g, cmp_pos_w, cmp_phi, w_o, mlp_norm_g, w_up, w_down):
    depth = w_in.shape[0]
    yp, ys = x_prompt, x_sample
    per_layer = []
    for layer in range(depth):
        p = {
            "attn_norm_g": attn_norm_g[layer], "w_in": _split_w_in(w_in[layer]),
            "gdn_conv_w": gdn_conv_w[layer], "gdn_a_log": gdn_a_log[layer], "gdn_dt_bias": gdn_dt_bias[layer],
            "gdn_norm_g": gdn_norm_g[layer], "q_norm_g": q_norm_g[layer], "k_norm_g": k_norm_g[layer],
            "cmp_w": _pos_weights(cmp_pos_w[layer]), "cmp_phi": cmp_phi[layer],
            "w_o": w_o[layer], "mlp_norm_g": mlp_norm_g[layer],
            "w_up": w_up[layer], "w_down": w_down[layer],
        }
        yp, cmp_p, slc_p, win_p, conv_p, s_p = _prompt_layer(yp, p)
        ys, cmp_s, slc_s, win_s, conv_s, s_s = _sample_layer(
            ys, layer, cache_cmp_kv, cache_slc_kv, cache_win_kv, cache_gdn_conv[layer],
            state_gdn[layer], page_table, p)
        per_layer.append((cmp_p, cmp_s, slc_p, slc_s, win_p, win_s, conv_p, conv_s, s_p, s_s))
    st = [jnp.stack(z, axis=0) for z in zip(*per_layer)]
    return (yp, ys) + tuple(st)
```

```python
import functools
import math

import numpy as np
import jax
import jax.numpy as jnp
from jax import lax
from jax.experimental import pallas as pl
from jax.experimental.pallas import tpu as pltpu

F32 = jnp.float32
BF16 = jnp.bfloat16
HI = lax.Precision.HIGHEST

HEAD_DIM = 128
GDN_HEADS = 16
NSA_HEADS = 16
NSA_KV_HEADS = 4
NSA_GROUP = NSA_HEADS // NSA_KV_HEADS
GDN_WIDTH = GDN_HEADS * HEAD_DIM
NSA_WIDTH = NSA_HEADS * HEAD_DIM
KV_WIDTH = 2 * NSA_KV_HEADS * HEAD_DIM
KV_SLABS = 2 * NSA_KV_HEADS
CONV_W = 4
GDN_CHUNK = 64
GDN_SOLVE_BLOCK = 16
GDN_HEAD_GROUP = 16
CMP_BLOCK = 32
CMP_STRIDE = 16
SLC_BLOCK = 64
SLC_TOPK = 16
SLC_LOCAL = 2
WINDOW = 512
PAGE_SIZE = 128
EPS = 1e-6
FORCE_SCORE = 1e9
NEG = -1e30
SCALE = HEAD_DIM ** -0.5

LANES = 128
SUBLANES = 8
VMEM_LIMIT = 52 * 1024 * 1024

COL_Q, COL_K, COL_V, COL_Z = 0, GDN_WIDTH, 2 * GDN_WIDTH, 3 * GDN_WIDTH
PROJ_A_WIDTH = 4 * GDN_WIDTH
COL_NQ = 0
COL_CMP = COL_NQ + NSA_WIDTH
COL_SLC = COL_CMP + KV_WIDTH
COL_WIN = COL_SLC + KV_WIDTH
PROJ_B_WIDTH = COL_WIN + KV_WIDTH
SMALL_B, SMALL_A, SMALL_GATE = 0, GDN_HEADS, 2 * GDN_HEADS
W_IN_B0 = PROJ_A_WIDTH
W_IN_N0 = W_IN_B0 + 2 * GDN_HEADS
W_IN_G0 = W_IN_N0 + PROJ_B_WIDTH


def _cparams(sem):
    return pltpu.CompilerParams(dimension_semantics=sem, vmem_limit_bytes=VMEM_LIMIT)


def _dot(a, b, prec=None):
    return jnp.dot(a, b, preferred_element_type=F32, precision=prec)


def _dot_nt(a, b, prec=None):
    return lax.dot_general(a, b, (((1,), (1,)), ((), ())), preferred_element_type=F32, precision=prec)


def _dot_tn(a, b, prec=None):
    return lax.dot_general(a, b, (((0,), (0,)), ((), ())), preferred_element_type=F32, precision=prec)


def _rms(x, g):
    return x * lax.rsqrt(jnp.mean(x * x, axis=-1, keepdims=True) + EPS) * g


def _silu(x):
    return x * jax.nn.sigmoid(x)


def _softplus(x):
    return jnp.maximum(x, 0.0) + jnp.log1p(jnp.exp(-jnp.abs(x)))


def _rms_cast_kernel(x_ref, g_ref, o_ref):
    o_ref[...] = _rms(x_ref[...], g_ref[...]).astype(o_ref.dtype)


def _rms_cast(x, g, *, tm):
    m, k = x.shape
    assert m % tm == 0
    return pl.pallas_call(
        _rms_cast_kernel,
        grid=(m // tm,),
        in_specs=[pl.BlockSpec((tm, k), lambda i: (i, 0)), pl.BlockSpec((1, k), lambda i: (0, 0))],
        out_specs=pl.BlockSpec((tm, k), lambda i: (i, 0)),
        out_shape=jax.ShapeDtypeStruct((m, k), BF16),
        compiler_params=_cparams(("parallel",)),
        name="rms_cast",
    )(x, g.reshape(1, k))


def _mm_nw_kernel(*refs, n_a, relu2, has_res, w_t, has_side):
    a_refs = refs[:n_a]
    w_ref = refs[n_a]
    r_ref = refs[n_a + 1] if has_res else None
    if has_side:
        side_ref, o_ref, side_out, wbf_ref = refs[-4:]
        side_out[...] = side_ref[...].astype(BF16)
    else:
        o_ref, wbf_ref = refs[-2:]

    @pl.when(pl.program_id(1) == 0)
    def _():
        w = w_ref[...]
        wbf_ref[...] = (w.T if w_t else w).astype(BF16)

    y, k0 = None, 0
    for a_ref in a_refs:
        part = _dot(a_ref[...], wbf_ref[k0:k0 + a_ref.shape[1], :])
        y = part if y is None else y + part
        k0 += a_ref.shape[1]
    if relu2:
        y = jnp.square(jnp.maximum(y, 0.0))
    if has_res:
        y = y + r_ref[...]
    o_ref[...] = y.astype(o_ref.dtype)


def _mm_nw(acts, w, *, n, tm, tn, res=None, relu2=False, out_dtype=F32, w_t=False, cast_side=None):
    m = acts[0].shape[0]
    k = sum(a.shape[1] for a in acts)
    assert m % tm == 0 and n % tn == 0 and w.shape[1 if w_t else 0] == k
    nm = m // tm
    in_specs = [pl.BlockSpec((tm, a.shape[1]), lambda j, i: (i, 0)) for a in acts]
    in_specs.append(pl.BlockSpec((tn, k), lambda j, i: (j, 0)) if w_t else pl.BlockSpec((k, tn), lambda j, i: (0, j)))
    args = list(acts) + [w]
    if res is not None:
        in_specs.append(pl.BlockSpec((tm, tn), lambda j, i: (i, j)))
        args.append(res)
    out_specs = pl.BlockSpec((tm, tn), lambda j, i: (i, j))
    out_shape = jax.ShapeDtypeStruct((m, n), out_dtype)
    if cast_side is not None:
        steps = (n // tn) * nm
        sr, sc = cast_side.shape
        assert sr % steps == 0 and (sr // steps) % 16 == 0
        side_spec = pl.BlockSpec((sr // steps, sc), lambda j, i: (j * nm + i, 0))
        in_specs.append(side_spec)
        args.append(cast_side)
        out_specs = [out_specs, side_spec]
        out_shape = [out_shape, jax.ShapeDtypeStruct((sr, sc), BF16)]
    return pl.pallas_call(
        functools.partial(_mm_nw_kernel, n_a=len(acts), relu2=relu2, has_res=res is not None, w_t=w_t,
                          has_side=cast_side is not None),
        grid=(n // tn, nm),
        in_specs=in_specs,
        out_specs=out_specs,
        out_shape=out_shape,
        scratch_shapes=[pltpu.VMEM((k, tn), BF16)],
        compiler_params=_cparams(("parallel", "arbitrary")),
        name="matmul_wcast",
    )(*args)


def _mm_res_kernel(a_ref, w_ref, r_ref, o_ref, acc_ref, *, nk):
    kk = pl.program_id(2)

    @pl.when(kk == 0)
    def _():
        acc_ref[...] = jnp.zeros_like(acc_ref)

    acc_ref[...] += _dot(a_ref[...], w_ref[...])

    @pl.when(kk == nk - 1)
    def _():
        o_ref[...] = r_ref[...] + acc_ref[...]


def _mm_res(a, w, res, *, tm, tn, tk):
    m, k = a.shape
    n = w.shape[1]
    assert m % tm == 0 and n % tn == 0 and k % tk == 0
    nk = k // tk
    return pl.pallas_call(
        functools.partial(_mm_res_kernel, nk=nk),
        grid=(m // tm, n // tn, nk),
        in_specs=[pl.BlockSpec((tm, tk), lambda i, j, kk: (i, kk)),
                  pl.BlockSpec((tk, tn), lambda i, j, kk: (kk, j)),
                  pl.BlockSpec((tm, tn), lambda i, j, kk: (i, j))],
        out_specs=pl.BlockSpec((tm, tn), lambda i, j, kk: (i, j)),
        out_shape=jax.ShapeDtypeStruct((m, n), F32),
        scratch_shapes=[pltpu.VMEM((tm, tn), F32)],
        compiler_params=_cparams(("parallel", "parallel", "arbitrary")),
        name="matmul_residual",
    )(a, w, res)


def _gdn_kernel(q_ref, k_ref, v_ref, z_ref, sm_ref, cbuf_ref, s0_ref, cw_ref, alog_ref, dt_ref, ng_ref,
                o_ref, sout_ref, xp_ref, st_ref, *, C, rows, nc, t_valid):
    W = GDN_WIDTH
    c = pl.program_id(1)

    @pl.when(c == 0)
    def _():
        xp_ref[...] = jnp.zeros(xp_ref.shape, F32)
        xp_ref[SUBLANES - (CONV_W - 1):SUBLANES, :] = cbuf_ref[0]
        st_ref[...] = s0_ref[0]

    @pl.when(c > 0)
    def _():
        xp_ref[0:SUBLANES, :] = xp_ref[C:C + SUBLANES, :]

    xp_ref[SUBLANES:SUBLANES + rows, 0:W] = q_ref[...]
    xp_ref[SUBLANES:SUBLANES + rows, W:2 * W] = k_ref[...]
    xp_ref[SUBLANES:SUBLANES + rows, 2 * W:3 * W] = v_ref[...]

    row = lax.broadcasted_iota(jnp.int32, (C, LANES), 0)
    valid = (c * C + row) < t_valid
    if rows < C:
        sm = jnp.concatenate([sm_ref[...], jnp.zeros((C - rows, LANES), F32)], axis=0)
    else:
        sm = sm_ref[...]
    beta_all = jnp.where(valid, jax.nn.sigmoid(sm), 0.0)
    g_all = jnp.where(valid, -jnp.exp(alog_ref[...]) * _softplus(sm + dt_ref[...]), 0.0)
    ri = lax.broadcasted_iota(jnp.int32, (C, C), 0)
    ci = lax.broadcasted_iota(jnp.int32, (C, C), 1)
    tri = ri >= ci
    strict = ri > ci
    gcum = _dot(tri.astype(F32), g_all, HI)
    li = lax.broadcasted_iota(jnp.int32, (LANES, LANES), 0)
    lj = lax.broadcasted_iota(jnp.int32, (LANES, LANES), 1)
    gcum_t = _dot_nt((li == lj).astype(F32), gcum, HI)
    valid_col = valid[:, 0:1]
    eye = (ri == ci).astype(F32)
    blk = min(C, GDN_SOLVE_BLOCK)
    same_blk = (ri // blk) == (ci // blk)
    n_blk = C // blk
    n_sq = int(math.log2(blk)) - 1

    def mm(a, b):
        return _dot(a.astype(BF16), b.astype(BF16))

    def mm_nt(a, b):
        return _dot_nt(a.astype(BF16), b.astype(BF16))

    def conv(col):
        acc = None
        for j in range(CONV_W):
            term = (xp_ref[pl.ds(SUBLANES - (CONV_W - 1) + j, C), col:col + LANES]
                    * cw_ref[j:j + 1, col:col + LANES])
            acc = term if acc is None else acc + term
        return _silu(acc)

    def head_group(hs):
        n = len(hs)
        qs = [conv(h * HEAD_DIM) for h in hs]
        ks = [conv(W + h * HEAD_DIM) for h in hs]
        vs = [conv(2 * W + h * HEAD_DIM) for h in hs]
        qs = [q * lax.rsqrt(jnp.sum(q * q, axis=-1, keepdims=True) + EPS) * SCALE for q in qs]
        ks = [jnp.where(valid_col, k * lax.rsqrt(jnp.sum(k * k, axis=-1, keepdims=True) + EPS), 0.0) for k in ks]
        gcs = [gcum[:, SMALL_A + h:SMALL_A + h + 1] for h in hs]
        bhs = [beta_all[:, SMALL_B + h:SMALL_B + h + 1] for h in hs]
        grs = [gcum_t[SMALL_A + h:SMALL_A + h + 1, :] for h in hs]
        decays = [jnp.where(tri, jnp.exp(jnp.where(tri, gc - gr, 0.0)), 0.0) for gc, gr in zip(gcs, grs)]
        egs = [jnp.exp(gc) for gc in gcs]

        kbs = [k * b for k, b in zip(ks, bhs)]
        kqs = [mm_nt(jnp.concatenate([kb, q], axis=0), k) for kb, q, k in zip(kbs, qs, ks)]
        nmats = [jnp.where(strict, -(kq[0:C] * d), 0.0) for kq, d in zip(kqs, decays)]
        aqks = [jnp.where(tri, kq[C:2 * C] * d, 0.0) for kq, d in zip(kqs, decays)]
        n_offs = [nm - jnp.where(same_blk, nm, 0.0) for nm in nmats]
        npows = [jnp.where(same_blk, nm, 0.0) for nm in nmats]
        pinvs = [eye + nd for nd in npows]
        for _ in range(n_sq):
            npows = [mm(np_, np_) for np_ in npows]
            pinvs = [p + mm(p, np_) for p, np_ in zip(pinvs, npows)]
        if n_blk > 1:
            ms_ = [mm(p, no) for p, no in zip(pinvs, n_offs)]
            series = [eye + m_ for m_ in ms_]
            terms = ms_
            for _ in range(n_blk - 2):
                terms = [mm(tm_, m_) for tm_, m_ in zip(terms, ms_)]
                series = [sr + tm_ for sr, tm_ in zip(series, terms)]
            pinvs = [mm(sr, p) for sr, p in zip(series, pinvs)]
        uws = [mm(p, jnp.concatenate([v * b, kb * eg], axis=1))
               for p, v, b, kb, eg in zip(pinvs, vs, bhs, kbs, egs)]

        ss = [st_ref[h] for h in hs]
        wss = [mm(jnp.concatenate([uw[:, HEAD_DIM:2 * HEAD_DIM], q * eg], axis=0), s)
               for uw, q, eg, s in zip(uws, qs, egs, ss)]
        v_news = [uw[:, 0:HEAD_DIM] - ws[0:C] for uw, ws in zip(uws, wss)]
        os_ = [ws[C:2 * C] + mm(aqk, vn) for ws, aqk, vn in zip(wss, aqks, v_news)]
        for i in range(n):
            g_last = gcs[i][C - 1:C, :]
            k_dec = ks[i] * jnp.exp(g_last - gcs[i])
            st_ref[hs[i]] = ss[i] * jnp.exp(g_last) + _dot_tn(k_dec.astype(BF16), v_news[i].astype(BF16))
        for i in range(n):
            col = hs[i] * HEAD_DIM
            o = _rms(os_[i], ng_ref[...])
            o_ref[:, col:col + HEAD_DIM] = (o[0:rows] * _silu(z_ref[:, col:col + HEAD_DIM])).astype(o_ref.dtype)

    for h0 in range(0, GDN_HEADS, GDN_HEAD_GROUP):
        head_group(list(range(h0, h0 + GDN_HEAD_GROUP)))

    @pl.when(c == nc - 1)
    def _():
        sout_ref[0] = st_ref[...]


def _gdn(proj, small, conv_buf, s0, conv_w, a_log, dt_bias, norm_g, *, bsz, rows_per_seq, t_valid):
    C = GDN_CHUNK
    if rows_per_seq >= C:
        assert rows_per_seq % C == 0
        rows, nc = C, rows_per_seq // C
    else:
        assert rows_per_seq % SUBLANES == 0
        C = max(2 * SUBLANES, 1 << (rows_per_seq - 1).bit_length())
        rows, nc = rows_per_seq, 1
    wblk = GDN_WIDTH
    zeros = jnp.zeros((LANES - 2 * GDN_HEADS,), F32)
    alog_row = jnp.concatenate([jnp.zeros((GDN_HEADS,), F32), a_log, zeros]).reshape(1, LANES)
    dt_row = jnp.concatenate([jnp.zeros((GDN_HEADS,), F32), dt_bias, zeros]).reshape(1, LANES)
    row_map = lambda b, c: b * nc + c
    in_specs = [
        pl.BlockSpec((rows, wblk), lambda b, c: (row_map(b, c), COL_Q // wblk)),
        pl.BlockSpec((rows, wblk), lambda b, c: (row_map(b, c), COL_K // wblk)),
        pl.BlockSpec((rows, wblk), lambda b, c: (row_map(b, c), COL_V // wblk)),
        pl.BlockSpec((rows, wblk), lambda b, c: (row_map(b, c), COL_Z // wblk)),
        pl.BlockSpec((rows, LANES), lambda b, c: (row_map(b, c), 0)),
        pl.BlockSpec((1, CONV_W - 1, 3 * GDN_WIDTH), lambda b, c: (b, 0, 0)),
        pl.BlockSpec((1, GDN_HEADS, HEAD_DIM, HEAD_DIM), lambda b, c: (b, 0, 0, 0)),
        pl.BlockSpec((CONV_W, 3 * GDN_WIDTH), lambda b, c: (0, 0)),
        pl.BlockSpec((1, LANES), lambda b, c: (0, 0)),
        pl.BlockSpec((1, LANES), lambda b, c: (0, 0)),
        pl.BlockSpec((1, HEAD_DIM), lambda b, c: (0, 0)),
    ]
    out_specs = [
        pl.BlockSpec((rows, wblk), lambda b, c: (row_map(b, c), 0)),
        pl.BlockSpec((1, GDN_HEADS, HEAD_DIM, HEAD_DIM), lambda b, c: (b, 0, 0, 0)),
    ]
    return pl.pallas_call(
        functools.partial(_gdn_kernel, C=C, rows=rows, nc=nc, t_valid=t_valid),
        grid=(bsz, nc),
        in_specs=in_specs,
        out_specs=out_specs,
        out_shape=[jax.ShapeDtypeStruct((bsz * rows_per_seq, GDN_WIDTH), BF16),
                   jax.ShapeDtypeStruct((bsz, GDN_HEADS, HEAD_DIM, HEAD_DIM), F32)],
        scratch_shapes=[pltpu.VMEM((C + SUBLANES, 3 * GDN_WIDTH), F32),
                        pltpu.VMEM((GDN_HEADS, HEAD_DIM, HEAD_DIM), F32)],
        compiler_params=_cparams(("parallel", "arbitrary")),
        name="gdn_chunk_scan",
    )(proj, proj, proj, proj, small, conv_buf, s0, conv_w, alog_row, dt_row, norm_g.reshape(1, HEAD_DIM))


def _prep_kernel(nq_ref, cmp_ref, slc_ref, win_ref, sm_ref, qg_ref, kg_ref,
                 q_out, cmp_out, slc_out, win_out, gate_out, slc_bf_out, win_bf_out):
    tm = nq_ref.shape[0]
    qg = qg_ref[...]
    for h in range(NSA_HEADS):
        sl = slice(h * HEAD_DIM, (h + 1) * HEAD_DIM)
        q_out[:, sl] = _rms(nq_ref[:, sl], qg)
    for slab in range(KV_SLABS):
        sl = slice(slab * HEAD_DIM, (slab + 1) * HEAD_DIM)
        cmp_out[pl.ds(slab, tm, stride=KV_SLABS), :] = cmp_ref[:, sl]
    for src, dst, dst_bf, gi in ((slc_ref, slc_out, slc_bf_out, 1), (win_ref, win_out, win_bf_out, 2)):
        kg = kg_ref[gi:gi + 1, :]
        for slab in range(KV_SLABS):
            sl = slice(slab * HEAD_DIM, (slab + 1) * HEAD_DIM)
            x = _rms(src[:, sl], kg) if slab < NSA_KV_HEADS else src[:, sl]
            dst[pl.ds(slab, tm, stride=KV_SLABS), :] = x
            dst_bf[:, sl] = x.astype(BF16)
    sig = jax.nn.sigmoid(sm_ref[...])
    per = NSA_GROUP * 3
    for g in range(NSA_KV_HEADS):
        gate_out[g] = pltpu.roll(sig, LANES - (SMALL_GATE + per * g), axis=1)


def _nsa_prep(proj, small, q_norm_g, k_norm_g, *, tm):
    m = proj.shape[0]
    assert m % tm == 0
    return pl.pallas_call(
        _prep_kernel,
        grid=(m // tm,),
        in_specs=[pl.BlockSpec((tm, NSA_WIDTH), lambda i: (i, COL_NQ // NSA_WIDTH)),
                  pl.BlockSpec((tm, KV_WIDTH), lambda i: (i, COL_CMP // KV_WIDTH)),
                  pl.BlockSpec((tm, KV_WIDTH), lambda i: (i, COL_SLC // KV_WIDTH)),
                  pl.BlockSpec((tm, KV_WIDTH), lambda i: (i, COL_WIN // KV_WIDTH)),
                  pl.BlockSpec((tm, LANES), lambda i: (i, 0)),
                  pl.BlockSpec((1, HEAD_DIM), lambda i: (0, 0)),
                  pl.BlockSpec((3, HEAD_DIM), lambda i: (0, 0))],
        out_specs=[pl.BlockSpec((tm, NSA_WIDTH), lambda i: (i, 0)),
                   pl.BlockSpec((tm * KV_SLABS, HEAD_DIM), lambda i: (i, 0)),
                   pl.BlockSpec((tm * KV_SLABS, HEAD_DIM), lambda i: (i, 0)),
                   pl.BlockSpec((tm * KV_SLABS, HEAD_DIM), lambda i: (i, 0)),
                   pl.BlockSpec((NSA_KV_HEADS, tm, LANES), lambda i: (0, i, 0)),
                   pl.BlockSpec((tm, KV_WIDTH), lambda i: (i, 0)),
                   pl.BlockSpec((tm, KV_WIDTH), lambda i: (i, 0))],
        out_shape=[jax.ShapeDtypeStruct((m, NSA_WIDTH), F32),
                   jax.ShapeDtypeStruct((m * KV_SLABS, HEAD_DIM), F32),
                   jax.ShapeDtypeStruct((m * KV_SLABS, HEAD_DIM), F32),
                   jax.ShapeDtypeStruct((m * KV_SLABS, HEAD_DIM), F32),
                   jax.ShapeDtypeStruct((NSA_KV_HEADS, m, LANES), F32),
                   jax.ShapeDtypeStruct((m, KV_WIDTH), BF16),
                   jax.ShapeDtypeStruct((m, KV_WIDTH), BF16)],
        compiler_params=_cparams(("parallel",)),
        name="nsa_prep",
    )(proj, proj, proj, proj, small, q_norm_g.reshape(1, HEAD_DIM), k_norm_g)


def _pool_kernel(tbl_ref, *refs, pg):
    del tbl_ref
    page_refs = refs[:pg]
    w_ref, s0_ref, s1_ref = refs[pg:]
    sub = PAGE_SIZE // CMP_STRIDE
    out_rows = sub * KV_SLABS
    for u in range(pg):
        x = page_refs[u][...].reshape(sub, CMP_STRIDE, KV_SLABS, HEAD_DIM)
        s0_ref[0, u * out_rows:(u + 1) * out_rows, :] = jnp.sum(x * w_ref[0][None], axis=1).reshape(out_rows, HEAD_DIM)
        s1_ref[0, u * out_rows:(u + 1) * out_rows, :] = jnp.sum(x * w_ref[1][None], axis=1).reshape(out_rows, HEAD_DIM)


def _pool(rows2d, table, wexp):
    bsz, n_pages = table.shape
    pg = math.gcd(n_pages, 16)
    sub = PAGE_SIZE // CMP_STRIDE
    page_rows = PAGE_SIZE * KV_SLABS
    out_rows = pg * sub * KV_SLABS

    def page_spec(u):
        return pl.BlockSpec((page_rows, HEAD_DIM), lambda b, p, tbl: (tbl[b, p * pg + u], 0))

    grid_spec = pltpu.PrefetchScalarGridSpec(
        num_scalar_prefetch=1,
        grid=(bsz, n_pages // pg),
        in_specs=[page_spec(u) for u in range(pg)]
        + [pl.BlockSpec((2, CMP_STRIDE, KV_SLABS, HEAD_DIM), lambda b, p, tbl: (0, 0, 0, 0))],
        out_specs=[pl.BlockSpec((1, out_rows, HEAD_DIM), lambda b, p, tbl: (b, p, 0)),
                   pl.BlockSpec((1, out_rows, HEAD_DIM), lambda b, p, tbl: (b, p, 0))],
    )
    shape = jax.ShapeDtypeStruct((bsz, n_pages * sub * KV_SLABS, HEAD_DIM), F32)
    return pl.pallas_call(
        functools.partial(_pool_kernel, pg=pg), grid_spec=grid_spec, out_shape=[shape, shape],
        compiler_params=_cparams(("parallel", "arbitrary")), name="cmp_pool",
    )(table, *([rows2d] * pg), wexp)


def _cmp_fin_kernel(s0_ref, s1_ref, phi_ref, kg_ref, kc_ref, vc_ref, *, n_cmp):
    nb = s0_ref.shape[1] // KV_SLABS
    live = lax.broadcasted_iota(jnp.int32, (nb, HEAD_DIM), 0) < n_cmp
    zero_row = jnp.zeros((1, HEAD_DIM), F32)

    def pooled(slab):
        first = s0_ref[0, pl.ds(slab, nb, stride=KV_SLABS), :]
        second = s1_ref[0, pl.ds(KV_SLABS + slab, nb - 1, stride=KV_SLABS), :]
        return first + jnp.concatenate([second, zero_row], axis=0)

    groups = range(NSA_KV_HEADS)
    ks = [_dot(pooled(g), phi_ref[0, g], HI) for g in groups]
    vs = [_dot(pooled(NSA_KV_HEADS + g), phi_ref[1, g], HI) for g in groups]
    for g in groups:
        kc_ref[0, g] = jnp.where(live, _rms(ks[g], kg_ref[...]), 0.0)
        vc_ref[0, g] = jnp.where(live, vs[g], 0.0)


def _cmp_finish(s0, s1, phi, kg, *, n_cmp):
    bsz = s0.shape[0]
    nb = s0.shape[1] // KV_SLABS
    assert n_cmp <= nb - 1
    shape = jax.ShapeDtypeStruct((bsz, NSA_KV_HEADS, nb, HEAD_DIM), F32)
    return pl.pallas_call(
        functools.partial(_cmp_fin_kernel, n_cmp=n_cmp),
        grid=(bsz,),
        in_specs=[pl.BlockSpec((1, nb * KV_SLABS, HEAD_DIM), lambda b: (b, 0, 0)),
                  pl.BlockSpec((1, nb * KV_SLABS, HEAD_DIM), lambda b: (b, 0, 0)),
                  pl.BlockSpec((2, NSA_KV_HEADS, HEAD_DIM, HEAD_DIM), lambda b: (0, 0, 0, 0)),
                  pl.BlockSpec((1, HEAD_DIM), lambda b: (0, 0))],
        out_specs=[pl.BlockSpec((1, NSA_KV_HEADS, nb, HEAD_DIM), lambda b: (b, 0, 0, 0)),
                   pl.BlockSpec((1, NSA_KV_HEADS, nb, HEAD_DIM), lambda b: (b, 0, 0, 0))],
        out_shape=[shape, shape],
        compiler_params=_cparams(("parallel",)),
        name="cmp_finish",
    )(s0, s1, phi, kg.reshape(1, HEAD_DIM))


def _stack_heads(ref, g, tq, dtype):
    parts = [ref[:, (g * NSA_GROUP + r) * HEAD_DIM:(g * NSA_GROUP + r + 1) * HEAD_DIM].astype(dtype)
             for r in range(NSA_GROUP)]
    return jnp.concatenate(parts, axis=0)


def _cmp_topk_kernel(q_ref, kc_ref, vc_ref, ov_ref, ocmp_ref, sel_ref, *, tq, n_cmp, n_slc, offset, token_lanes):
    i = pl.program_id(1)
    nb = kc_ref.shape[2]
    nsp = sel_ref.shape[2]
    rows = NSA_GROUP * tq
    r4 = lax.broadcasted_iota(jnp.int32, (rows, nb), 0)
    n4 = lax.broadcasted_iota(jnp.int32, (rows, nb), 1)
    tpos = offset + i * tq + (r4 & (tq - 1))
    valid = (n4 * CMP_STRIDE + (CMP_BLOCK - 1) <= tpos) & (n4 < n_cmp)

    if token_lanes:
        nsr = -(-n_slc // SUBLANES) * SUBLANES
        shape, blk_axis, tok_axis = (nsr, tq), 0, 1
    else:
        shape, blk_axis, tok_axis = (tq, nsp), 1, 0
    blk = lax.broadcasted_iota(jnp.int32, shape, blk_axis)
    tq_pos = offset + i * tq + lax.broadcasted_iota(jnp.int32, shape, tok_axis)
    cur = tq_pos // SLC_BLOCK
    forced = (blk == 0) | ((blk <= cur) & (blk > cur - SLC_LOCAL))
    future = blk > cur
    in_range = blk < n_slc
    topk = min(SLC_TOPK, n_slc)

    groups = range(NSA_KV_HEADS)
    ss = [jnp.where(valid, _dot_nt(_stack_heads(q_ref, g, tq, F32), kc_ref[0, g], HI) * SCALE, -jnp.inf)
          for g in groups]
    ms = [jnp.max(s, axis=-1, keepdims=True) for s in ss]
    ps = [jnp.exp(s - jnp.where(m == -jnp.inf, 0.0, m)) for s, m in zip(ss, ms)]
    ps = [p / jnp.maximum(jnp.sum(p, axis=-1, keepdims=True), jnp.finfo(jnp.float32).tiny) for p in ps]
    os_ = [_dot(p.astype(BF16), vc_ref[0, g].astype(BF16)) for g, p in zip(groups, ps)]
    psums = [functools.reduce(jnp.add, [p[r * tq:(r + 1) * tq] for r in range(NSA_GROUP)]) for p in ps]
    if token_lanes:
        imps = [_dot_nt(ov_ref[0:shape[0], :], psum, HI) for psum in psums]
    else:
        imps = [_dot(psum, ov_ref[...], HI) for psum in psums]
    scores = [jnp.where(in_range, jnp.where(forced, FORCE_SCORE, jnp.where(future, -FORCE_SCORE, imp)), -jnp.inf)
              for imp in imps]
    ranks = [jnp.zeros(shape, jnp.int32) for _ in groups]
    for j in range(n_slc):
        for g in groups:
            cj = scores[g][j:j + 1, :] if token_lanes else scores[g][:, j:j + 1]
            ahead = (cj > scores[g]) | ((cj == scores[g]) & (blk > j))
            ranks[g] = ranks[g] + ahead.astype(jnp.int32)
    for g in groups:
        for r in range(NSA_GROUP):
            col = (g * NSA_GROUP + r) * HEAD_DIM
            ocmp_ref[:, col:col + HEAD_DIM] = os_[g][r * tq:(r + 1) * tq]
        sel = ((ranks[g] < topk) & in_range).astype(F32)
        if token_lanes:
            sel = jnp.concatenate([sel, jnp.zeros((nsp - shape[0], tq), F32)], axis=0).T
        sel_ref[g] = sel


def _cmp_topk(qn, kc, vc, overlap, *, bsz, rows_per_seq, tq, n_cmp, n_slc, offset):
    m = qn.shape[0]
    nt = rows_per_seq // tq
    nb = kc.shape[2]
    nsp = overlap.shape[1]
    token_lanes = tq % LANES == 0 and nsp == tq
    if token_lanes:
        overlap = overlap.T
    return pl.pallas_call(
        functools.partial(_cmp_topk_kernel, tq=tq, n_cmp=n_cmp, n_slc=n_slc, offset=offset,
                          token_lanes=token_lanes),
        grid=(bsz, nt),
        in_specs=[pl.BlockSpec((tq, NSA_WIDTH), lambda b, i: (b * nt + i, 0)),
                  pl.BlockSpec((1, NSA_KV_HEADS, nb, HEAD_DIM), lambda b, i: (b, 0, 0, 0)),
                  pl.BlockSpec((1, NSA_KV_HEADS, nb, HEAD_DIM), lambda b, i: (b, 0, 0, 0)),
                  pl.BlockSpec(overlap.shape, lambda b, i: (0, 0))],
        out_specs=[pl.BlockSpec((tq, NSA_WIDTH), lambda b, i: (b * nt + i, 0)),
                   pl.BlockSpec((NSA_KV_HEADS, tq, nsp), lambda b, i: (0, b * nt + i, 0))],
        out_shape=[jax.ShapeDtypeStruct((m, NSA_WIDTH), F32),
                   jax.ShapeDtypeStruct((NSA_KV_HEADS, m, nsp), F32)],
        compiler_params=_cparams(("parallel", "arbitrary")),
        name="cmp_attn_topk",
    )(qn, kc, vc, overlap)


def _overlap_matrix(nb, nsp, n_cmp, n_slc):
    cs = np.arange(nb) * CMP_STRIDE
    ss = np.arange(nsp) * SLC_BLOCK
    lo = np.maximum(cs[:, None], ss[None, :])
    hi = np.minimum(cs[:, None] + CMP_BLOCK, ss[None, :] + SLC_BLOCK)
    ov = (np.maximum(hi - lo, 0) / CMP_BLOCK).astype(np.float32)
    ov[n_cmp:, :] = 0.0
    ov[:, n_slc:] = 0.0
    return jnp.asarray(ov)


def _flash_init(m_ref, l_ref, acc_ref):
    m_ref[...] = jnp.full(m_ref.shape, NEG, F32)
    l_ref[...] = jnp.zeros(l_ref.shape, F32)
    acc_ref[...] = jnp.zeros(acc_ref.shape, F32)


def _lane_rep(col, width):
    tile = jnp.broadcast_to(col, (col.shape[0], LANES))
    return tile if width == LANES else jnp.concatenate([tile] * (width // LANES), axis=1)


def _flash_update(s, mask, v, m_ref, l_ref, acc_ref):
    rows, tk = s.shape
    reps = tk // LANES
    s = jnp.where(mask, s, NEG)
    m_prev = m_ref[...]
    m_new = jnp.maximum(m_prev, jnp.broadcast_to(jnp.max(s, axis=-1, keepdims=True), (rows, LANES)))
    alpha = jnp.exp(m_prev - m_new)
    p = jnp.exp(s - jnp.concatenate([m_new] * reps, axis=1))
    l_ref[...] = alpha * l_ref[...] + jnp.broadcast_to(jnp.sum(p, axis=-1, keepdims=True), (rows, LANES))
    acc_ref[...] = alpha * acc_ref[...] + _dot(p.astype(v.dtype), v)
    m_ref[...] = m_new


def _pattn_kernel(q_ref, ks_ref, vs_ref, kw_ref, vw_ref, ocmp_ref, sel_ref, gate_ref, exp_ref, o_ref,
                  sexp_ref, m_ref, l_ref, acc_ref, *, tq, tk, wlen):
    i = pl.program_id(2)
    t = ks_ref.shape[0]
    q4 = jnp.concatenate([(q_ref[:, r * HEAD_DIM:(r + 1) * HEAD_DIM] * SCALE).astype(BF16)
                          for r in range(NSA_GROUP)], axis=0)
    sexp_ref[...] = _dot(sel_ref[0].astype(BF16), exp_ref[...])
    qpos = i * tq + lax.broadcasted_iota(jnp.int32, (tq, tk), 0)
    lane = lax.broadcasted_iota(jnp.int32, (tq, tk), 1)

    def tile4(x):
        return jnp.concatenate([x] * NSA_GROUP, axis=0)

    def slc_step(j, carry):
        start = pl.multiple_of(j * tk, tk)
        s = _dot_nt(q4, ks_ref[pl.ds(start, tk), :])
        mask = (sexp_ref[:, pl.ds(start, tk)] > 0.5) & (start + lane <= qpos)
        _flash_update(s, tile4(mask), vs_ref[pl.ds(start, tk), :], m_ref, l_ref, acc_ref)
        return carry

    _flash_init(m_ref, l_ref, acc_ref)
    lax.fori_loop(0, ((i + 1) * tq + tk - 1) // tk, slc_step, 0)
    o_slc = acc_ref[...] / l_ref[...]

    wstart = pl.multiple_of(jnp.clip((i + 1) * tq - wlen, 0, t - wlen), tq)
    s = _dot_nt(q4, kw_ref[pl.ds(wstart, wlen), :])
    wdiff = (i * tq + lax.broadcasted_iota(jnp.int32, (tq, wlen), 0)
             - (wstart + lax.broadcasted_iota(jnp.int32, (tq, wlen), 1)))
    s = jnp.where(tile4((wdiff >= 0) & (wdiff < WINDOW)), s, NEG)
    p = jnp.exp(s - _lane_rep(jnp.max(s, axis=-1, keepdims=True), wlen))
    o_win = _dot(p.astype(BF16), vw_ref[pl.ds(wstart, wlen), :]) / _lane_rep(jnp.sum(p, axis=-1, keepdims=True), HEAD_DIM)

    gates = gate_ref[0]
    for r in range(NSA_GROUP):
        sl = slice(r * tq, (r + 1) * tq)
        col = slice(r * HEAD_DIM, (r + 1) * HEAD_DIM)
        o = (gates[:, 3 * r:3 * r + 1] * ocmp_ref[:, col]
             + gates[:, 3 * r + 1:3 * r + 2] * o_slc[sl]
             + gates[:, 3 * r + 2:3 * r + 3] * o_win[sl])
        o_ref[:, col] = o.astype(o_ref.dtype)


def _prompt_attn(qn, slc_bf, win_bf, ocmp, sel, gates, *, bsz, t, tq):
    m = qn.shape[0]
    nt = t // tq
    gw = NSA_GROUP * HEAD_DIM
    half = NSA_KV_HEADS
    nsp = sel.shape[2]
    n_slc = -(-t // SLC_BLOCK)
    expand = np.zeros((nsp, t), np.float32)
    expand[np.arange(t) // SLC_BLOCK, np.arange(t)] = 1.0
    assert n_slc <= nsp
    tk = next(c for c in (512, 256, 128) if t % c == 0 and c >= tq)
    wlen = min(t, WINDOW + tq)
    assert wlen % tq == 0 and t % tq == 0
    kv_spec = lambda which: pl.BlockSpec((t, HEAD_DIM), lambda b, g, i: (b, which * half + g))
    return pl.pallas_call(
        functools.partial(_pattn_kernel, tq=tq, tk=tk, wlen=wlen),
        grid=(bsz, NSA_KV_HEADS, nt),
        in_specs=[pl.BlockSpec((tq, gw), lambda b, g, i: (b * nt + i, g)),
                  kv_spec(0), kv_spec(1), kv_spec(0), kv_spec(1),
                  pl.BlockSpec((tq, gw), lambda b, g, i: (b * nt + i, g)),
                  pl.BlockSpec((1, tq, nsp), lambda b, g, i: (g, b * nt + i, 0)),
                  pl.BlockSpec((1, tq, LANES), lambda b, g, i: (g, b * nt + i, 0)),
                  pl.BlockSpec((nsp, t), lambda b, g, i: (0, 0))],
        out_specs=pl.BlockSpec((tq, gw), lambda b, g, i: (b * nt + i, g)),
        out_shape=jax.ShapeDtypeStruct((m, NSA_WIDTH), BF16),
        scratch_shapes=[pltpu.VMEM((tq, t), F32),
                        pltpu.VMEM((NSA_GROUP * tq, LANES), F32),
                        pltpu.VMEM((NSA_GROUP * tq, LANES), F32),
                        pltpu.VMEM((NSA_GROUP * tq, HEAD_DIM), F32)],
        compiler_params=_cparams(("parallel", "parallel", "arbitrary")),
        name="nsa_prompt_attn",
    )(qn, slc_bf, slc_bf, win_bf, win_bf, ocmp, sel, gates, jnp.asarray(expand, BF16))


def _sattn_kernel(tbl_ref, q_ref, *refs, tq, n_steps, pg, past):
    del tbl_ref
    page_refs = refs[:pg]
    (snew_ref, wcache_ref, wnew_ref, ocmp_ref, sel_ref, gate_ref, exp_ref, o_ref, s_buf, v_buf) = refs[pg:]
    j = pl.program_id(1)
    rows = NSA_GROUP * tq
    nsp = sel_ref.shape[2]
    n_prev = wcache_ref.shape[0] // KV_SLABS

    def tile4(x):
        return jnp.concatenate([x] * NSA_GROUP, axis=0)

    q4_bf = [_stack_heads(q_ref, g, tq, BF16) for g in range(NSA_KV_HEADS)]
    for u in range(pg):
        key0 = pl.multiple_of((j * pg + u) * PAGE_SIZE, PAGE_SIZE)
        for g in range(NSA_KV_HEADS):
            k = page_refs[u][pl.ds(g, PAGE_SIZE, stride=KV_SLABS), :].astype(BF16)
            s_buf[g * rows:(g + 1) * rows, pl.ds(key0, PAGE_SIZE)] = _dot_nt(q4_bf[g], k)
            v_buf[g, pl.ds(key0, PAGE_SIZE), :] = (
                page_refs[u][pl.ds(NSA_KV_HEADS + g, PAGE_SIZE, stride=KV_SLABS), :].astype(BF16))

    @pl.when(j == n_steps - 1)
    def _():
        blk_lane = lax.broadcasted_iota(jnp.int32, (tq, nsp), 1)
        trow = lax.broadcasted_iota(jnp.int32, (tq, tq), 0)
        tcol = lax.broadcasted_iota(jnp.int32, (tq, tq), 1)
        causal = tcol <= trow
        wrow = lax.broadcasted_iota(jnp.int32, (tq, n_prev), 0)
        wcol = lax.broadcasted_iota(jnp.int32, (tq, n_prev), 1)
        wdiff = wrow + n_prev - wcol
        wmask = (wdiff >= 0) & (wdiff < WINDOW)
        ndiff = trow - tcol
        nmask = (ndiff >= 0) & (ndiff < WINDOW)
        gates_all = gate_ref[...]
        groups = range(NSA_KV_HEADS)
        q4s = [_stack_heads(q_ref, g, tq, F32) for g in groups]

        def joint_softmax(scs, sns, vcs, vns):
            mxs = [jnp.maximum(jnp.max(sc, axis=-1, keepdims=True), jnp.max(sn, axis=-1, keepdims=True))
                   for sc, sn in zip(scs, sns)]
            pcs = [jnp.exp(sc - mx) for sc, mx in zip(scs, mxs)]
            pns = [jnp.exp(sn - mx) for sn, mx in zip(sns, mxs)]
            dens = [jnp.sum(pc, axis=-1, keepdims=True) + jnp.sum(pn, axis=-1, keepdims=True)
                    for pc, pn in zip(pcs, pns)]
            return [(_dot(pc.astype(BF16), vc) + _dot(pn, vn, HI)) / den
                    for pc, pn, vc, vn, den in zip(pcs, pns, vcs, vns, dens)]

        flags = []
        for g in groups:
            flag = jnp.zeros((tq, tq), jnp.bool_)
            for u in range(tq):
                col = jnp.sum(jnp.where(blk_lane == (past + u) // SLC_BLOCK, sel_ref[g], 0.0),
                              axis=1, keepdims=True) > 0.5
                flag = flag | (col & (tcol == u))
            flags.append(flag & causal)
        scs = [jnp.where(tile4(_dot(sel_ref[g].astype(BF16), exp_ref[...]) > 0.5),
                         s_buf[g * rows:(g + 1) * rows, :] * SCALE, NEG) for g in groups]
        sns = [jnp.where(tile4(flags[g]),
                         _dot_nt(q4s[g], snew_ref[pl.ds(g, tq, stride=KV_SLABS), :], HI) * SCALE, NEG)
               for g in groups]
        o_slcs = joint_softmax(scs, sns, [v_buf[g] for g in groups],
                               [snew_ref[pl.ds(NSA_KV_HEADS + g, tq, stride=KV_SLABS), :] for g in groups])

        scs = [jnp.where(tile4(wmask), _dot_nt(q4s[g].astype(BF16),
                                               wcache_ref[pl.ds(g, n_prev, stride=KV_SLABS), :].astype(BF16)) * SCALE,
                         NEG) for g in groups]
        sns = [jnp.where(tile4(nmask),
                         _dot_nt(q4s[g], wnew_ref[pl.ds(g, tq, stride=KV_SLABS), :], HI) * SCALE, NEG)
               for g in groups]
        o_wins = joint_softmax(
            scs, sns,
            [wcache_ref[pl.ds(NSA_KV_HEADS + g, n_prev, stride=KV_SLABS), :].astype(BF16) for g in groups],
            [wnew_ref[pl.ds(NSA_KV_HEADS + g, tq, stride=KV_SLABS), :] for g in groups])

        for g in groups:
            gates = gates_all[g]
            for r in range(NSA_GROUP):
                sl = slice(r * tq, (r + 1) * tq)
                col = slice((g * NSA_GROUP + r) * HEAD_DIM, (g * NSA_GROUP + r + 1) * HEAD_DIM)
                o = (gates[:, 3 * r:3 * r + 1] * ocmp_ref[:, col]
                     + gates[:, 3 * r + 1:3 * r + 2] * o_slcs[g][sl]
                     + gates[:, 3 * r + 2:3 * r + 3] * o_wins[g][sl])
                o_ref[:, col] = o.astype(o_ref.dtype)


def _sample_attn(qn, cache2d, page_table, slc_new, win_cache2d, win_row0, win_new, ocmp, sel, gates,
                 *, bsz, tq, past, n_prev):
    n_pages = page_table.shape[1]
    pg = math.gcd(n_pages, 16)
    n_steps = n_pages // pg
    nsp = sel.shape[2]
    rows = NSA_GROUP * tq
    page_rows = PAGE_SIZE * KV_SLABS
    expand = np.zeros((nsp, past), np.float32)
    expand[np.arange(past) // SLC_BLOCK, np.arange(past)] = 1.0

    def page_spec(u):
        return pl.BlockSpec((page_rows, HEAD_DIM), lambda b, j, tbl: (tbl[b, j * pg + u], 0))

    grid_spec = pltpu.PrefetchScalarGridSpec(
        num_scalar_prefetch=1,
        grid=(bsz, n_steps),
        in_specs=[pl.BlockSpec((tq, NSA_WIDTH), lambda b, j, tbl: (b, 0))]
        + [page_spec(u) for u in range(pg)]
        + [pl.BlockSpec((tq * KV_SLABS, HEAD_DIM), lambda b, j, tbl: (b, 0)),
           pl.BlockSpec((n_prev * KV_SLABS, HEAD_DIM), lambda b, j, tbl: (win_row0 + b, 0)),
           pl.BlockSpec((tq * KV_SLABS, HEAD_DIM), lambda b, j, tbl: (b, 0)),
           pl.BlockSpec((tq, NSA_WIDTH), lambda b, j, tbl: (b, 0)),
           pl.BlockSpec((NSA_KV_HEADS, tq, nsp), lambda b, j, tbl: (0, b, 0)),
           pl.BlockSpec((NSA_KV_HEADS, tq, LANES), lambda b, j, tbl: (0, b, 0)),
           pl.BlockSpec((nsp, past), lambda b, j, tbl: (0, 0))],
        out_specs=pl.BlockSpec((tq, NSA_WIDTH), lambda b, j, tbl: (b, 0)),
        scratch_shapes=[pltpu.VMEM((NSA_KV_HEADS * rows, past), F32),
                        pltpu.VMEM((NSA_KV_HEADS, past, HEAD_DIM), BF16)],
    )
    return pl.pallas_call(
        functools.partial(_sattn_kernel, tq=tq, n_steps=n_steps, pg=pg, past=past),
        grid_spec=grid_spec,
        out_shape=jax.ShapeDtypeStruct((bsz * tq, NSA_WIDTH), BF16),
        compiler_params=_cparams(("parallel", "arbitrary")),
        name="nsa_sample_attn",
    )(page_table, qn, *([cache2d] * pg), slc_new, win_cache2d, win_new, ocmp, sel, gates,
      jnp.asarray(expand, BF16))


def _split_w_in(w_in):
    wt = w_in.T
    pad = jnp.zeros((LANES - 2 * GDN_HEADS - 3 * NSA_HEADS, wt.shape[1]), w_in.dtype)
    wt_small = jnp.concatenate([wt[W_IN_B0:W_IN_N0], wt[W_IN_G0:], pad], axis=0)
    return wt, wt[W_IN_N0:W_IN_G0], wt_small


def _in_proj(x2d, p, *, tm):
    xn = _rms_cast(x2d, p["attn_norm_g"], tm=min(tm, 512))
    w_full, w_b, w_small = p["w_in"]
    proj_a = _mm_nw([xn], w_full, n=PROJ_A_WIDTH, tm=tm, tn=512, w_t=True)
    proj_b = _mm_nw([xn], w_b, n=PROJ_B_WIDTH, tm=tm, tn=512, w_t=True)
    small = _mm_nw([xn], w_small, n=LANES, tm=tm, tn=LANES, w_t=True)
    return proj_a, proj_b, small


def _pos_weights(cmp_pos_w):
    w = cmp_pos_w.reshape(2, CMP_BLOCK // CMP_STRIDE, CMP_STRIDE, NSA_KV_HEADS)
    w = jnp.transpose(w, (1, 2, 0, 3))
    return jnp.broadcast_to(w[..., None], w.shape + (HEAD_DIM,)).reshape(2, CMP_STRIDE, KV_SLABS, HEAD_DIM)


def _mixers_out(x2d, o_gdn, o_nsa, p, *, tm):
    w_o, w_up = p["w_o"], p["w_up"]
    x1 = _mm_nw([o_gdn, o_nsa], w_o, n=w_o.shape[1], tm=tm, tn=512, res=x2d)
    xn = _rms_cast(x1, p["mlp_norm_g"], tm=min(tm, 512))
    if "w_down_bf" not in p:
        hid, p["w_down_bf"] = _mm_nw([xn], w_up, n=w_up.shape[1], tm=tm, tn=512, relu2=True, out_dtype=BF16,
                                     cast_side=p["w_down"])
    else:
        hid = _mm_nw([xn], w_up, n=w_up.shape[1], tm=tm, tn=512, relu2=True, out_dtype=BF16)
    w_down = p["w_down_bf"]
    return _mm_res(hid, w_down, x1, tm=tm, tn=min(1024, w_down.shape[1]), tk=min(2048, w_down.shape[0]))


def _round_up(x, n):
    return -(-x // n) * n


def _prompt_layer(x, p):
    bsz, t, d = x.shape
    m = bsz * t
    x2d = x.reshape(m, d)
    proj_a, proj_b, small = _in_proj(x2d, p, tm=1024)
    conv0 = jnp.zeros((bsz, CONV_W - 1, 3 * GDN_WIDTH), F32)
    s_zero = jnp.zeros((bsz, GDN_HEADS, HEAD_DIM, HEAD_DIM), F32)
    o_gdn, s_new = _gdn(proj_a, small, conv0, s_zero, p["gdn_conv_w"], p["gdn_a_log"], p["gdn_dt_bias"],
                        p["gdn_norm_g"], bsz=bsz, rows_per_seq=t, t_valid=t)
    qn, cmp_n, slc_n, win_n, gates, slc_bf, win_bf = _nsa_prep(proj_b, small, p["q_norm_g"], p["k_norm_g"], tm=512)

    n_cmp = (t - CMP_BLOCK) // CMP_STRIDE + 1
    n_slc = -(-t // SLC_BLOCK)
    n_pages = t // PAGE_SIZE
    table = jnp.arange(bsz * n_pages, dtype=jnp.int32).reshape(bsz, n_pages)
    kv_tail = (2, NSA_KV_HEADS, HEAD_DIM)
    s0, s1 = _pool(cmp_n, table, p["cmp_w"])
    kc, vc = _cmp_finish(s0, s1, p["cmp_phi"], p["k_norm_g"][0], n_cmp=n_cmp)
    nb = kc.shape[2]
    nsp = _round_up(n_slc, LANES)
    tq = 128
    ocmp, sel = _cmp_topk(qn, kc, vc, _overlap_matrix(nb, nsp, n_cmp, n_slc), bsz=bsz, rows_per_seq=t, tq=tq,
                          n_cmp=n_cmp, n_slc=n_slc, offset=0)
    tq_attn = 2 * tq if t % (2 * tq) == 0 else tq
    o_nsa = _prompt_attn(qn, slc_bf, win_bf, ocmp, sel, gates, bsz=bsz, t=t, tq=tq_attn)
    y = _mixers_out(x2d, o_gdn, o_nsa, p, tm=1024)

    keep = min(WINDOW, t)
    return (y.reshape(bsz, t, d),
            cmp_n.reshape((bsz, t) + kv_tail),
            slc_n.reshape((bsz, t) + kv_tail),
            win_n.reshape((bsz, t) + kv_tail)[:, t - keep:],
            proj_a.reshape(bsz, t, PROJ_A_WIDTH)[:, t - (CONV_W - 1):, 0:3 * GDN_WIDTH],
            s_new)


def _sample_layer(x, layer, cache_cmp_kv, cache_slc_kv, cache_win_kv, conv_buf, s0_state, page_table, p):
    bsz, t, d = x.shape
    tp = _round_up(t, SUBLANES)
    n_pages = page_table.shape[1]
    past = n_pages * PAGE_SIZE
    assert t <= SLC_BLOCK and past % SLC_BLOCK == 0 and t >= CONV_W - 1
    x2d = jnp.pad(x, ((0, 0), (0, tp - t), (0, 0))).reshape(bsz * tp, d)
    proj_a, proj_b, small = _in_proj(x2d, p, tm=bsz * tp)
    o_gdn, s_new = _gdn(proj_a, small, conv_buf, s0_state, p["gdn_conv_w"], p["gdn_a_log"], p["gdn_dt_bias"],
                        p["gdn_norm_g"], bsz=bsz, rows_per_seq=tp, t_valid=t)
    qn, cmp_n, slc_n, win_n, gates, _, _ = _nsa_prep(proj_b, small, p["q_norm_g"], p["k_norm_g"], tm=bsz * tp)

    total = past + t
    n_cmp = (total - CMP_BLOCK) // CMP_STRIDE + 1
    n_slc = -(-total // SLC_BLOCK)
    assert (n_cmp - 1) * CMP_STRIDE + CMP_BLOCK <= past
    depth, n_phys = cache_cmp_kv.shape[:2]
    table = page_table + layer * n_phys
    cache_rows = depth * n_phys * PAGE_SIZE * KV_SLABS
    s0, s1 = _pool(cache_cmp_kv.reshape(cache_rows, HEAD_DIM), table, p["cmp_w"])
    kc, vc = _cmp_finish(s0, s1, p["cmp_phi"], p["k_norm_g"][0], n_cmp=n_cmp)
    nb = kc.shape[2]
    nsp = _round_up(n_slc, LANES)
    ocmp, sel = _cmp_topk(qn, kc, vc, _overlap_matrix(nb, nsp, n_cmp, n_slc), bsz=bsz, rows_per_seq=tp, tq=tp,
                          n_cmp=n_cmp, n_slc=n_slc, offset=past)
    n_prev = cache_win_kv.shape[2]
    o_nsa = _sample_attn(qn, cache_slc_kv.reshape(cache_rows, HEAD_DIM), table, slc_n,
                         cache_win_kv.reshape(depth * bsz * n_prev * KV_SLABS, HEAD_DIM), layer * bsz,
                         win_n, ocmp, sel, gates, bsz=bsz, tq=tp, past=past, n_prev=n_prev)
    y = _mixers_out(x2d, o_gdn, o_nsa, p, tm=bsz * tp)

    kv_tail = (2, NSA_KV_HEADS, HEAD_DIM)
    win_all = jnp.concatenate([cache_win_kv[layer], win_n.reshape((bsz, tp) + kv_tail)[:, :t]], axis=1)
    keep = min(WINDOW, n_prev + t)
    return (y.reshape(bsz, tp, d)[:, :t],
            cmp_n.reshape((bsz, tp) + kv_tail)[:, :t],
            slc_n.reshape((bsz, tp) + kv_tail)[:, :t],
            win_all[:, n_prev + t - keep:],
            proj_a.reshape(bsz, tp, PROJ_A_WIDTH)[:, t - (CONV_W - 1):t, 0:3 * GDN_WIDTH],
            s_new)


def kernel(x_prompt, x_sample, cache_cmp_kv, cache_slc_kv, cache_win_kv, cache_gdn_conv, state_gdn, page_table, attn_norm_g, w_in, gdn_conv_w, gdn_a_log, gdn_dt_bias, gdn_norm_g, q_norm_g, k_norm_g, cmp_pos_w, cmp_phi, w_o, mlp_norm_g, w_up, w_down):
    depth = w_in.shape[0]
    yp, ys = x_prompt, x_sample
    per_layer = []
    for layer in range(depth):
        p = {
            "attn_norm_g": attn_norm_g[layer], "w_in": _split_w_in(w_in[layer]),
            "gdn_conv_w": gdn_conv_w[layer], "gdn_a_log": gdn_a_log[layer], "gdn_dt_bias": gdn_dt_bias[layer],
            "gdn_norm_g": gdn_norm_g[layer], "q_norm_g": q_norm_g[layer], "k_norm_g": k_norm_g[layer],
            "cmp_w": _pos_weights(cmp_pos_w[layer]), "cmp_phi": cmp_phi[layer],
            "w_o": w_o[layer], "mlp_norm_g": mlp_norm_g[layer],
            "w_up": w_up[layer], "w_down": w_down[layer],
        }
        yp, cmp_p, slc_p, win_p, conv_p, s_p = _prompt_layer(yp, p)
        ys, cmp_s, slc_s, win_s, conv_s, s_s = _sample_layer(
            ys, layer, cache_cmp_kv, cache_slc_kv, cache_win_kv, cache_gdn_conv[layer],
            state_gdn[layer], page_table, p)
        per_layer.append((cmp_p, cmp_s, slc_p, slc_s, win_p, win_s, conv_p, conv_s, s_p, s_s))
    st = [jnp.stack(z, axis=0) for z in zip(*per_layer)]
    return (yp, ys) + tuple(st)
```
